```python
import jax, jax.numpy as jnp
from jax import lax
import numpy as np

D_MODEL = 1024
BATCH = 2
SEQ = 8192
DEPTH = 4
DEC_BATCH = 32
DEC_SEQ = 1
PAST_LEN = 8192
PAGE_SIZE = 128

HEAD_DIM = 64
W_RET = D_MODEL // 4
W_LRU = D_MODEL // 2
W_FOX = D_MODEL // 4
H_RET = W_RET // HEAD_DIM
H_LRU = W_LRU // HEAD_DIM
H_FOX = W_FOX // HEAD_DIM
W_MIX = W_RET + W_LRU + W_FOX
D_FF = 2816
CONV_W = 4
LRU_C = 8.0
RET_CHUNK = 128
Q_BLOCK = 128
ROPE_BASE = 10000.0
EPS = 1e-6
NEG_INF = -1e30
FOX_SCALE = HEAD_DIM ** -0.5
IN_SPLITS = (W_RET, 2 * W_RET, 3 * W_RET, 4 * W_RET,
             4 * W_RET + W_LRU, 4 * W_RET + 2 * W_LRU,
             4 * W_RET + 2 * W_LRU + W_FOX, 4 * W_RET + 2 * W_LRU + 2 * W_FOX,
             4 * W_RET + 2 * W_LRU + 3 * W_FOX)
IN_WIDTH = 4 * W_RET + 2 * W_LRU + 3 * W_FOX + H_FOX

kernel_name = 'hybrid_ret_rglru_fox_macaron_step'


def rms_norm(x, g):
    x32 = x.astype(jnp.float32)
    y = x32 * lax.rsqrt(jnp.mean(x32 * x32, axis=-1, keepdims=True) + EPS)
    return (y * g.astype(jnp.float32)).astype(x.dtype)


def swiglu(x, wg, wu, wd):
    return (jax.nn.silu(x @ wg) * (x @ wu)) @ wd


def rope(x, pos):
    half = HEAD_DIM // 2
    inv = ROPE_BASE ** (-jnp.arange(half, dtype=jnp.float32) / half)
    ang = pos.astype(jnp.float32)[:, None] * inv[None, :]
    cos = jnp.cos(ang)[None, :, None, :]
    sin = jnp.sin(ang)[None, :, None, :]
    x32 = x.astype(jnp.float32)
    x1, x2 = x32[..., :half], x32[..., half:]
    return jnp.concatenate([x1 * cos - x2 * sin, x1 * sin + x2 * cos], axis=-1)


def retention(q, k, v, s0):
    B, L = q.shape[0], q.shape[1]
    C = RET_CHUNK if L % RET_CHUNK == 0 else L
    n = L // C
    log_g = jnp.log1p(-jnp.exp2(-5.0 - jnp.arange(H_RET, dtype=jnp.float32)))
    idx = jnp.arange(C, dtype=jnp.float32)
    diff = idx[:, None] - idx[None, :]
    decay = jnp.where(diff >= 0, jnp.exp(jnp.maximum(diff, 0.0)[None] * log_g[:, None, None]), 0.0)
    xi = jnp.exp((idx[:, None] + 1.0) * log_g[None, :])
    zeta = jnp.exp((C - 1.0 - idx[:, None]) * log_g[None, :])
    g_chunk = jnp.exp(C * log_g)

    def to_chunks(t):
        return jnp.swapaxes(t.reshape(B, n, C, H_RET, HEAD_DIM), 0, 1)

    def step(s, inp):
        qc, kc, vc = inp
        att = jnp.einsum('bihd,bjhd->bhij', qc, kc) * decay
        o = (jnp.einsum('bhij,bjhe->bihe', att, vc)
             + jnp.einsum('bihd,bhde->bihe', qc, s) * xi[None, :, :, None])
        s = s * g_chunk[None, :, None, None] + jnp.einsum('bjhd,bjhe->bhde', kc * zeta[None, :, :, None], vc)
        return s, o

    s_fin, o = lax.scan(step, s0, (to_chunks(q), to_chunks(k), to_chunks(v)))
    return jnp.swapaxes(o, 0, 1).reshape(B, L, H_RET, HEAD_DIM), s_fin


def causal_conv(x, buf, w, b):
    L = x.shape[1]
    xp = jnp.concatenate([buf.astype(x.dtype), x], axis=1)
    y = b
    for j in range(CONV_W):
        y = y + xp[:, j:j + L] * w[j]
    return y, xp[:, -(CONV_W - 1):]


def _lin_combine(c1, c2):
    a1, b1 = c1
    a2, b2 = c2
    return a1 * a2, a2 * b1 + b2


def rg_lru(x, h0, wr, br, wi, bi, lam):
    B, L = x.shape[0], x.shape[1]
    x32 = x.astype(jnp.float32)
    xb = x32.reshape(B, L, H_LRU, HEAD_DIM)
    r = jax.nn.sigmoid(jnp.einsum('blnd,nde->blne', xb, wr.astype(jnp.float32)).reshape(B, L, W_LRU) + br)
    i = jax.nn.sigmoid(jnp.einsum('blnd,nde->blne', xb, wi.astype(jnp.float32)).reshape(B, L, W_LRU) + bi)
    log_a = -LRU_C * r * jax.nn.softplus(-lam.astype(jnp.float32))
    a = jnp.exp(log_a)
    u = jnp.sqrt(-jnp.expm1(2.0 * log_a)) * (i * x32)
    a_cum, b_cum = lax.associative_scan(_lin_combine, (a, u), axis=1)
    h = a_cum * h0.astype(jnp.float32)[:, None, :] + b_cum
    return h, h[:, -1]


def fox_block(q, cq, qpos, k, v, ck, kpos):
    s = jnp.einsum('bqhd,bkhd->bhqk', q, k).astype(jnp.float32) * FOX_SCALE
    s = s + jnp.swapaxes(cq, 1, 2)[..., :, None] - jnp.swapaxes(ck, 1, 2)[..., None, :]
    s = jnp.where(kpos[None, :] <= qpos[:, None], s, NEG_INF)
    p = jax.nn.softmax(s, axis=-1).astype(v.dtype)
    return jnp.einsum('bhqk,bkhd->bqhd', p, v)


def mixer(hn, pos, s_ret, h_lru, conv_buf, past, w_in, ret_gn, conv_w, conv_b,
          wr, br, wi, bi, lam, fox_qn, fox_kn, fox_bf, w_out):
    B, L = hn.shape[0], hn.shape[1]
    proj = hn @ w_in
    rq, rk, rv, rg, lx, lg, fq, fk, fv, ff = jnp.split(proj, IN_SPLITS, axis=-1)

    def heads(t, h):
        return t.reshape(B, L, h, HEAD_DIM)

    q = rope(heads(rq, H_RET), pos)
    k = rope(heads(rk, H_RET), pos) * (HEAD_DIM ** -0.5)
    o, s_ret_new = retention(q, k, heads(rv, H_RET).astype(jnp.float32), s_ret.astype(jnp.float32))
    mu = jnp.mean(o, axis=-1, keepdims=True)
    var = jnp.mean(jnp.square(o - mu), axis=-1, keepdims=True)
    o = ((o - mu) * lax.rsqrt(var + EPS)).reshape(B, L, W_RET) * ret_gn.astype(jnp.float32)
    y_ret = o.astype(hn.dtype) * jax.nn.silu(rg)

    xc, conv_new = causal_conv(lx, conv_buf, conv_w, conv_b)
    h, h_last = rg_lru(xc, h_lru, wr, br, wi, bi, lam)
    y_lru = h.astype(hn.dtype) * jax.nn.gelu(lg)

    qf = rms_norm(heads(fq, H_FOX), fox_qn)
    kf = rms_norm(heads(fk, H_FOX), fox_kn)
    vf = heads(fv, H_FOX)
    logf = jax.nn.log_sigmoid(ff.astype(jnp.float32) + fox_bf.astype(jnp.float32))
    if past is None:
        c = jnp.cumsum(logf, axis=1)
        kpos = jnp.arange(L)

        def blk(bi_):
            st = bi_ * Q_BLOCK
            return fox_block(lax.dynamic_slice_in_dim(qf, st, Q_BLOCK, 1),
                             lax.dynamic_slice_in_dim(c, st, Q_BLOCK, 1),
                             st + jnp.arange(Q_BLOCK), kf, vf, c, kpos)

        of = lax.map(blk, jnp.arange(L // Q_BLOCK))
        of = jnp.swapaxes(of, 0, 1).reshape(B, L, H_FOX, HEAD_DIM)
    else:
        pk, pv, plf = past
        P = pk.shape[1]
        k_all = jnp.concatenate([pk.astype(kf.dtype), kf], axis=1)
        v_all = jnp.concatenate([pv.astype(vf.dtype), vf], axis=1)
        c_all = jnp.cumsum(jnp.concatenate([plf.astype(jnp.float32), logf], axis=1), axis=1)
        of = fox_block(qf, c_all[:, P:], P + jnp.arange(L), k_all, v_all, c_all, jnp.arange(P + L))
    y_fox = of.reshape(B, L, W_FOX)

    out = jnp.concatenate([y_ret, y_lru, y_fox], axis=-1) @ w_out
    return out, (kf, vf, logf, s_ret_new, h_last, conv_new)


def layer(x, pos, s_ret, h_lru, conv_buf, past, ffn1, g_mix, mix_w, ffn2):
    x = x + 0.5 * swiglu(rms_norm(x, ffn1[0]), ffn1[1], ffn1[2], ffn1[3])
    m, st = mixer(rms_norm(x, g_mix), pos, s_ret, h_lru, conv_buf, past, *mix_w)
    x = x + m
    x = x + 0.5 * swiglu(rms_norm(x, ffn2[0]), ffn2[1], ffn2[2], ffn2[3])
    return x, st


def setup_inputs(seed: int = 0) -> dict:
    key = jax.random.key(seed)
    ks = jax.random.split(key, 32)
    n_pages = PAST_LEN // PAGE_SIZE
    n_used = DEC_BATCH * n_pages
    n_pool = n_used + max(1, n_used // 4)
    f32 = jnp.float32

    def nrm(k, shape, scale):
        return jax.random.normal(k, shape, f32) * scale

    def gain(k, shape):
        return 1.0 + 0.05 * jax.random.normal(k, shape, f32)

    u = jax.random.uniform(ks[22], (DEPTH, W_LRU), f32, minval=0.9, maxval=0.999)
    a0 = u ** (1.0 / LRU_C)
    lam = jnp.log(a0) - jnp.log1p(-a0)
    page_table = jax.random.permutation(ks[8], n_pool)[:n_used].reshape(DEC_BATCH, n_pages).astype(jnp.int32)
    return {
        'x_prompt': nrm(ks[0], (BATCH, SEQ, D_MODEL), 1.0),
        'x_sample': nrm(ks[1], (DEC_BATCH, DEC_SEQ, D_MODEL), 1.0),
        'cache_k': nrm(ks[2], (DEPTH, n_pool, PAGE_SIZE, H_FOX, HEAD_DIM), 1.0),
        'cache_v': nrm(ks[3], (DEPTH, n_pool, PAGE_SIZE, H_FOX, HEAD_DIM), 1.0),
        'cache_logf': jax.nn.log_sigmoid(3.0 + jax.random.normal(ks[4], (DEPTH, n_pool, PAGE_SIZE, H_FOX), f32)),
        'state_ret': nrm(ks[5], (DEPTH, DEC_BATCH, H_RET, HEAD_DIM, HEAD_DIM), 0.5),
        'state_lru': nrm(ks[6], (DEPTH, DEC_BATCH, W_LRU), 0.5),
        'state_conv': nrm(ks[7], (DEPTH, DEC_BATCH, CONV_W - 1, W_LRU), 1.0),
        'page_table': page_table,
        'norm_ffn1': gain(ks[9], (DEPTH, D_MODEL)),
        'ffn1_gate': nrm(ks[10], (DEPTH, D_MODEL, D_FF), D_MODEL ** -0.5),
        'ffn1_up': nrm(ks[11], (DEPTH, D_MODEL, D_FF), D_MODEL ** -0.5),
        'ffn1_down': nrm(ks[12], (DEPTH, D_FF, D_MODEL), D_FF ** -0.5),
        'norm_mix': gain(ks[13], (DEPTH, D_MODEL)),
        'w_in': nrm(ks[14], (DEPTH, D_MODEL, IN_WIDTH), D_MODEL ** -0.5),
        'ret_gn': gain(ks[15], (DEPTH, W_RET)),
        'conv_w': nrm(ks[16], (DEPTH, CONV_W, W_LRU), CONV_W ** -0.5),
        'conv_b': nrm(ks[17], (DEPTH, W_LRU), 0.01),
        'lru_wr': nrm(ks[18], (DEPTH, H_LRU, HEAD_DIM, HEAD_DIM), HEAD_DIM ** -0.5),
        'lru_br': nrm(ks[19], (DEPTH, W_LRU), 0.01),
        'lru_wi': nrm(ks[20], (DEPTH, H_LRU, HEAD_DIM, HEAD_DIM), HEAD_DIM ** -0.5),
        'lru_bi': nrm(ks[21], (DEPTH, W_LRU), 0.01),
        'lru_lambda': lam,
        'fox_qn': gain(ks[23], (DEPTH, HEAD_DIM)),
        'fox_kn': gain(ks[24], (DEPTH, HEAD_DIM)),
        'fox_bf': 3.0 + 0.1 * jax.random.normal(ks[25], (DEPTH, H_FOX), f32),
        'w_out': nrm(ks[26], (DEPTH, W_MIX, D_MODEL), W_MIX ** -0.5),
        'norm_ffn2': gain(ks[27], (DEPTH, D_MODEL)),
        'ffn2_gate': nrm(ks[28], (DEPTH, D_MODEL, D_FF), D_MODEL ** -0.5),
        'ffn2_up': nrm(ks[29], (DEPTH, D_MODEL, D_FF), D_MODEL ** -0.5),
        'ffn2_down': nrm(ks[30], (DEPTH, D_FF, D_MODEL), D_FF ** -0.5),
    }


def reference(x_prompt, x_sample, cache_k, cache_v, cache_logf, state_ret, state_lru, state_conv,
              page_table, norm_ffn1, ffn1_gate, ffn1_up, ffn1_down, norm_mix, w_in, ret_gn,
              conv_w, conv_b, lru_wr, lru_br, lru_wi, lru_bi, lru_lambda, fox_qn, fox_kn, fox_bf,
              w_out, norm_ffn2, ffn2_gate, ffn2_up, ffn2_down):
    bp, lp = x_prompt.shape[0], x_prompt.shape[1]
    bs, ls = x_sample.shape[0], x_sample.shape[1]
    past_len = page_table.shape[1] * PAGE_SIZE
    pos_p = jnp.arange(lp)
    pos_s = past_len + jnp.arange(ls)
    zr = jnp.zeros((bp, H_RET, HEAD_DIM, HEAD_DIM), jnp.float32)
    zh = jnp.zeros((bp, W_LRU), jnp.float32)
    zc = jnp.zeros((bp, CONV_W - 1, W_LRU), x_prompt.dtype)

    xp, xs = x_prompt, x_sample
    st_p, st_s = [], []
    for l in range(DEPTH):
        ffn1 = (norm_ffn1[l], ffn1_gate[l], ffn1_up[l], ffn1_down[l])
        ffn2 = (norm_ffn2[l], ffn2_gate[l], ffn2_up[l], ffn2_down[l])
        mix_w = (w_in[l], ret_gn[l], conv_w[l], conv_b[l], lru_wr[l], lru_br[l], lru_wi[l],
                 lru_bi[l], lru_lambda[l], fox_qn[l], fox_kn[l], fox_bf[l], w_out[l])
        xp, sp = layer(xp, pos_p, zr, zh, zc, None, ffn1, norm_mix[l], mix_w, ffn2)
        past = (cache_k[l][page_table].reshape(bs, past_len, H_FOX, HEAD_DIM),
                cache_v[l][page_table].reshape(bs, past_len, H_FOX, HEAD_DIM),
                cache_logf[l][page_table].reshape(bs, past_len, H_FOX))
        xs, ss = layer(xs, pos_s, state_ret[l], state_lru[l], state_conv[l], past,
                       ffn1, norm_mix[l], mix_w, ffn2)
        st_p.append(sp)
        st_s.append(ss)

    def stk(sts, j):
        return jnp.stack([s[j] for s in sts], axis=0)

    return (xp, xs,
            stk(st_p, 0), stk(st_p, 1), stk(st_p, 2), stk(st_p, 3), stk(st_p, 4), stk(st_p, 5),
            stk(st_s, 0), stk(st_s, 1), stk(st_s, 2), stk(st_s, 3), stk(st_s, 4), stk(st_s, 5))
```

```python
import functools

import numpy as np
import jax
import jax.numpy as jnp
from jax import lax
from jax.experimental import pallas as pl
from jax.experimental.pallas import tpu as pltpu

F32 = jnp.float32
BF16 = jnp.bfloat16

HEAD_DIM = 64
N_HEADS = 4
W_HEAD = N_HEADS * HEAD_DIM
CONV_W = 4
LRU_C = 8.0
RET_CHUNK = 128
ROPE_BASE = 10000.0
EPS = 1e-6
NEG_INF = -1e30
FOX_SCALE = HEAD_DIM ** -0.5
SAMPLE_ROWS = 16
PAGES_PER_STEP = 8
VMEM_LIMIT = 48 * 1024 * 1024

COL_RQ, COL_RK, COL_RV, COL_RG = 0, 1, 2, 3
COL_LX, COL_LG = 2, 3
COL_FQ, COL_FK, COL_FV = 8, 9, 10
COL_FF = 22
PROJ_W = 23 * 128


def _dot(a, b):
    return jnp.dot(a, b, preferred_element_type=F32)


def _dot_nt(a, b):
    return lax.dot_general(a, b, (((1,), (1,)), ((), ())), preferred_element_type=F32)


def _dot_tn(a, b):
    return lax.dot_general(a, b, (((0,), (0,)), ((), ())), preferred_element_type=F32)


def _sigmoid(x):
    return 1.0 / (1.0 + jnp.exp(-x))


def _split2(x):
    hi = x.astype(BF16)
    lo = (x - hi.astype(F32)).astype(BF16)
    return hi, lo


def _split3(x):
    hi = x.astype(BF16)
    r = x - hi.astype(F32)
    mid = r.astype(BF16)
    lo = (r - mid.astype(F32)).astype(BF16)
    return hi, mid, lo


def _params(*sem):
    return pltpu.CompilerParams(dimension_semantics=sem, vmem_limit_bytes=VMEM_LIMIT)


def _ffn_body(has_mix, *refs):
    if has_mix:
        (x_ref, yr_ref, yl_ref, yf_ref, wo_ref, g_ref, wg_ref, wu_ref, wd_ref,
         o_ref, xres_ref, xn_ref, acc_ref) = refs
    else:
        x_ref, g_ref, wg_ref, wu_ref, wd_ref, o_ref, xres_ref, xn_ref, acc_ref = refs
    j = pl.program_id(1)

    @pl.when(j == 0)
    def _():
        x = x_ref[...]
        if has_mix:
            w1 = yr_ref.shape[1]
            w2 = w1 + yl_ref.shape[1]
            x = (x + _dot(yr_ref[...], wo_ref[0:w1, :]) + _dot(yl_ref[...], wo_ref[w1:w2, :])
                 + _dot(yf_ref[...], wo_ref[w2:, :]))
        xres_ref[...] = x
        ms = jnp.mean(x * x, axis=-1, keepdims=True)
        xn_ref[...] = (x * lax.rsqrt(ms + EPS) * g_ref[...]).astype(BF16)
        acc_ref[...] = jnp.zeros_like(acc_ref)

    xn = xn_ref[...]
    gate = _dot(xn, wg_ref[...])
    up = _dot(xn, wu_ref[...])
    hid = (gate * _sigmoid(gate) * up).astype(BF16)
    acc_ref[...] += _dot(hid, wd_ref[...])

    @pl.when(j == pl.num_programs(1) - 1)
    def _():
        o_ref[...] = xres_ref[...] + 0.5 * acc_ref[...]


def _ffn(x, g, wg, wu, wd, mix=None, tm=512, tf=256):
    m, d = x.shape
    f = wg.shape[1]
    tm = min(tm, m)
    grid = (m // tm, f // tf)
    row = lambda i, j: (i, 0)
    const = lambda i, j: (0, 0)
    in_specs = [pl.BlockSpec((tm, d), row)]
    args = [x]
    if mix is not None:
        yr, yl, yf, wo = mix
        in_specs += [pl.BlockSpec((tm, yr.shape[1]), row), pl.BlockSpec((tm, yl.shape[1]), row),
                     pl.BlockSpec((tm, yf.shape[1]), row), pl.BlockSpec(wo.shape, const)]
        args += [yr, yl, yf, wo]
    in_specs += [pl.BlockSpec((1, d), const),
                 pl.BlockSpec((d, tf), lambda i, j: (0, j)),
                 pl.BlockSpec((d, tf), lambda i, j: (0, j)),
                 pl.BlockSpec((tf, d), lambda i, j: (j, 0))]
    args += [g.reshape(1, d), wg, wu, wd]
    return pl.pallas_call(
        functools.partial(_ffn_body, mix is not None),
        grid=grid,
        in_specs=in_specs,
        out_specs=pl.BlockSpec((tm, d), row),
        out_shape=jax.ShapeDtypeStruct((m, d), F32),
        scratch_shapes=[pltpu.VMEM((tm, d), F32), pltpu.VMEM((tm, d), BF16), pltpu.VMEM((tm, d), F32)],
        compiler_params=_params("parallel", "arbitrary"),
        name="ffn_mix" if mix is not None else "ffn",
    )(*args)


def _inproj_body(x_ref, g_ref, w_ref, o_ref):
    x = x_ref[...]
    ms = jnp.mean(x * x, axis=-1, keepdims=True)
    xn = (x * lax.rsqrt(ms + EPS) * g_ref[...]).astype(BF16)
    o_ref[...] = _dot(xn, w_ref[...])


def _inproj(x, g, w, tm=512):
    m, d = x.shape
    n = w.shape[1]
    tm = min(tm, m)
    return pl.pallas_call(
        _inproj_body,
        grid=(m // tm,),
        in_specs=[pl.BlockSpec((tm, d), lambda i: (i, 0)),
                  pl.BlockSpec((1, d), lambda i: (0, 0)),
                  pl.BlockSpec((d, n), lambda i: (0, 0))],
        out_specs=pl.BlockSpec((tm, n), lambda i: (i, 0)),
        out_shape=jax.ShapeDtypeStruct((m, n), F32),
        compiler_params=_params("parallel"),
        name="inproj",
    )(x, g.reshape(1, d), w)


def _head_norm(x, g):
    outs = []
    for h in range(N_HEADS):
        xh = x[:, h * HEAD_DIM:(h + 1) * HEAD_DIM]
        ms = jnp.mean(xh * xh, axis=-1, keepdims=True)
        outs.append(xh * lax.rsqrt(ms + EPS) * g)
    return outs


def _fox_prep_body(prompt, tm, fq_ref, fk_ref, fv_ref, ff_ref, gq_ref, gk_ref, bf_ref, *rest):
    lane = lax.broadcasted_iota(jnp.int32, (tm, 128), 1)
    z = ff_ref[...] + bf_ref[...]
    logf = jnp.minimum(z, 0.0) - jnp.log1p(jnp.exp(-jnp.abs(z)))
    logf = jnp.where(lane < N_HEADS, logf, 0.0)
    qh = _head_norm(fq_ref[...], gq_ref[...])
    kh = _head_norm(fk_ref[...], gk_ref[...])
    if not prompt:
        kn_ref, logf_ref, qn_ref = rest
        kn_ref[...] = jnp.concatenate(kh, axis=-1)
        qn_ref[...] = jnp.concatenate(qh, axis=-1)
        logf_ref[...] = logf
        return
    kn_ref, logf_ref, qa_ref, ka_ref, va_ref, carry_ref = rest
    kn_ref[...] = jnp.concatenate(kh, axis=-1)
    logf_ref[...] = logf

    @pl.when(pl.program_id(1) == 0)
    def _():
        carry_ref[...] = jnp.zeros_like(carry_ref)

    r_i = lax.broadcasted_iota(jnp.int32, (tm, tm), 0)
    c_i = lax.broadcasted_iota(jnp.int32, (tm, tm), 1)
    tri = jnp.where(c_i <= r_i, 1.0, 0.0).astype(BF16)
    hi, mid, lo = _split3(logf)
    c = _dot(tri, hi) + _dot(tri, mid) + _dot(tri, lo) + carry_ref[0:1, :]
    carry_ref[0:1, :] = c[tm - 1:tm, :]

    v = fv_ref[...]
    lane64 = lax.broadcasted_iota(jnp.int32, (tm, HEAD_DIM), 1)
    v_tail = jnp.where(lane64 == 0, 1.0, 0.0)
    for h in range(N_HEADS):
        ch = c[:, h:h + 1]
        chi = ch.astype(BF16).astype(F32)
        r1 = ch - chi
        cmid = r1.astype(BF16).astype(F32)
        clo = r1 - cmid
        q_tail = jnp.where(lane64 == 0, chi, jnp.where(lane64 == 1, cmid, jnp.where(
            lane64 == 2, clo, jnp.where(lane64 < 6, 1.0, 0.0))))
        k_tail = jnp.where(lane64 < 3, 1.0, jnp.where(lane64 == 3, -chi, jnp.where(
            lane64 == 4, -cmid, jnp.where(lane64 == 5, -clo, 0.0))))
        qa_ref[h] = jnp.concatenate([qh[h] * FOX_SCALE, q_tail], axis=-1).astype(BF16)
        ka_ref[h] = jnp.concatenate([kh[h], k_tail], axis=-1).astype(BF16)
        va_ref[h] = jnp.concatenate([v[:, h * HEAD_DIM:(h + 1) * HEAD_DIM], v_tail], axis=-1).astype(BF16)


def _fox_prep(proj, gq, gk, bf, nb, l, prompt, tm=256):
    m = proj.shape[0]
    tm = min(tm, l)
    nl = l // tm
    rows = lambda b, i: b * nl + i
    in_specs = [pl.BlockSpec((tm, W_HEAD), lambda b, i: (rows(b, i), COL_FQ)),
                pl.BlockSpec((tm, W_HEAD), lambda b, i: (rows(b, i), COL_FK)),
                pl.BlockSpec((tm, W_HEAD), lambda b, i: (rows(b, i), COL_FV)),
                pl.BlockSpec((tm, 128), lambda b, i: (rows(b, i), COL_FF)),
                pl.BlockSpec((1, HEAD_DIM), lambda b, i: (0, 0)),
                pl.BlockSpec((1, HEAD_DIM), lambda b, i: (0, 0)),
                pl.BlockSpec((1, 128), lambda b, i: (0, 0))]
    row_spec = lambda w: pl.BlockSpec((tm, w), lambda b, i: (rows(b, i), 0))
    out_specs = [row_spec(W_HEAD), row_spec(128)]
    out_shape = [jax.ShapeDtypeStruct((m, W_HEAD), F32), jax.ShapeDtypeStruct((m, 128), F32)]
    scratch = []
    if prompt:
        aug = pl.BlockSpec((None, N_HEADS, tm, 128), lambda b, i: (b, 0, i, 0))
        out_specs += [aug, aug, aug]
        out_shape += [jax.ShapeDtypeStruct((nb, N_HEADS, l, 128), BF16)] * 3
        scratch = [pltpu.VMEM((8, 128), F32)]
    else:
        out_specs += [row_spec(W_HEAD)]
        out_shape += [jax.ShapeDtypeStruct((m, W_HEAD), F32)]
    bf_pad = jnp.zeros((1, 128), F32).at[0, :N_HEADS].set(bf)
    return pl.pallas_call(
        functools.partial(_fox_prep_body, prompt, tm),
        grid=(nb, nl),
        in_specs=in_specs,
        out_specs=out_specs,
        out_shape=out_shape,
        scratch_shapes=scratch,
        compiler_params=_params("arbitrary", "arbitrary"),
        name="fox_prep_prompt" if prompt else "fox_prep_sample",
    )(proj, proj, proj, proj, gq.reshape(1, HEAD_DIM), gk.reshape(1, HEAD_DIM), bf_pad)


def _fox_attn_body(tq, tk, qa_ref, ka_ref, va_ref, o_ref, m_ref, acc_ref):
    qi = pl.program_id(2)
    n_full = (qi * tq) // tk
    outs = []
    for h in range(2):
        q = qa_ref[h]
        m_ref[...] = jnp.full_like(m_ref, NEG_INF)
        acc_ref[...] = jnp.zeros_like(acc_ref)

        def step(j, masked, h=h, q=q):
            k0 = pl.multiple_of(j * tk, tk)
            k = ka_ref[h, pl.ds(k0, tk), :]
            v = va_ref[h, pl.ds(k0, tk), :]
            s = _dot_nt(q, k)
            if masked:
                qpos = qi * tq + lax.broadcasted_iota(jnp.int32, (tq, tk), 0)
                kpos = j * tk + lax.broadcasted_iota(jnp.int32, (tq, tk), 1)
                s = jnp.where(kpos <= qpos, s, NEG_INF)
            m_old = m_ref[...]
            m_new = jnp.maximum(m_old, jnp.max(s, axis=-1, keepdims=True))
            alpha = jnp.exp(m_old - m_new)
            p = jnp.exp(s - m_new)
            acc_ref[...] = alpha * acc_ref[...] + _dot(p.astype(BF16), v)
            m_ref[...] = m_new

        def full_step(j, carry):
            step(j, False)
            return carry

        lax.fori_loop(0, n_full, full_step, 0)
        for d in range(tq // tk):
            step(n_full + d, True)
        acc = acc_ref[...]
        outs.append(acc[:, :HEAD_DIM] / acc[:, HEAD_DIM:HEAD_DIM + 1])
    o_ref[...] = jnp.concatenate(outs, axis=-1).astype(BF16)


def _fox_attn(qa, ka, va, tq=512, tk=512):
    nb, nh, l, _ = qa.shape
    tq = min(tq, l)
    tk = min(tk, tq)
    nq = l // tq
    return pl.pallas_call(
        functools.partial(_fox_attn_body, tq, tk),
        grid=(nb, nh // 2, nq),
        in_specs=[pl.BlockSpec((None, 2, tq, 128), lambda b, hp, i: (b, hp, i, 0)),
                  pl.BlockSpec((None, 2, l, 128), lambda b, hp, i: (b, hp, 0, 0)),
                  pl.BlockSpec((None, 2, l, 128), lambda b, hp, i: (b, hp, 0, 0))],
        out_specs=pl.BlockSpec((tq, 128), lambda b, hp, i: (b * nq + i, hp)),
        out_shape=jax.ShapeDtypeStruct((nb * l, W_HEAD), BF16),
        scratch_shapes=[pltpu.VMEM((tq, 1), F32), pltpu.VMEM((tq, 128), F32)],
        compiler_params=_params("parallel", "parallel", "arbitrary"),
        name="fox_attn",
    )(qa, ka, va)


def _ret_tables(pos0, l_pad, l_real, chunk):
    c_real = chunk if l_real % chunk == 0 else l_real
    half = HEAD_DIM // 2
    inv = ROPE_BASE ** (-np.arange(half, dtype=np.float64) / half)
    inv_l = np.tile(inv, 2 * N_HEADS)
    sgn = np.tile(np.concatenate([-np.ones(half), np.ones(half)]), N_HEADS)
    n_a = l_pad // chunk
    ang_a = (pos0 + chunk * np.arange(n_a))[:, None] * inv_l[None, :]
    ang_b = np.arange(chunk)[:, None] * inv_l[None, :]
    log_g = np.log1p(-np.exp2(-5.0 - np.arange(N_HEADS, dtype=np.float64)))
    idx = np.arange(chunk, dtype=np.float64)
    diff = idx[:, None] - idx[None, :]
    decay = np.where(diff >= 0, np.exp(np.maximum(diff, 0.0)[None] * log_g[:, None, None]), 0.0)
    lg_l = np.repeat(log_g, HEAD_DIM)
    xi = np.exp((idx[:, None] + 1.0) * lg_l[None, :])
    zeta = np.exp((c_real - 1.0 - idx[:, None]) * lg_l[None, :])
    head = np.arange(W_HEAD) // HEAD_DIM
    bd = (head[:, None] == head[None, :]).astype(np.float64)
    gmat = bd * np.exp(c_real * lg_l)[:, None]
    f = lambda a: jnp.asarray(a, dtype=F32)
    return dict(ca=f(np.cos(ang_a)), sa=f(sgn * np.sin(ang_a)), cb=f(np.cos(ang_b)), sb=f(sgn * np.sin(ang_b)),
                decay=f(decay), xi=f(xi), zeta=f(zeta), gmat=f(gmat), bd=f(bd),
                bdavg=jnp.asarray(bd / HEAD_DIM, dtype=BF16))


def _ret_body(tm, ch, q_ref, k_ref, v_ref, g_ref, ca_ref, sa_ref, cb_ref, sb_ref, dec_ref, xi_ref,
              zeta_ref, gmat_ref, bd_ref, bdavg_ref, gn_ref, s0_ref, y_ref, sout_ref, st_ref):
    i = pl.program_id(1)
    n_chunk = tm // ch

    @pl.when(i == 0)
    def _():
        s0 = s0_ref[...]
        st_ref[...] = jnp.concatenate([s0] * N_HEADS, axis=-1) * bd_ref[...]

    lane = lax.broadcasted_iota(jnp.int32, (ch, W_HEAD), 1)
    lane128 = lax.broadcasted_iota(jnp.int32, (ch, 128), 1)
    first_half = (lane128 % HEAD_DIM) < (HEAD_DIM // 2)

    def swap_halves(x):
        parts = []
        for blk in range(W_HEAD // 128):
            xb = x[:, blk * 128:(blk + 1) * 128]
            parts.append(jnp.where(first_half, pltpu.roll(xb, 128 - HEAD_DIM // 2, 1),
                                   pltpu.roll(xb, HEAD_DIM // 2, 1)))
        return jnp.concatenate(parts, axis=-1)

    def gmean(t):
        hi, lo = _split2(t)
        return _dot(hi, bdavg_ref[...]) + _dot(lo, bdavg_ref[...])

    def chunk(c, carry):
        r0 = pl.multiple_of(c * ch, ch)
        a = i * n_chunk + c
        c_a = ca_ref[pl.ds(a, 1), :]
        s_a = sa_ref[pl.ds(a, 1), :]
        cos = c_a * cb_ref[...] - s_a * sb_ref[...]
        sin = s_a * cb_ref[...] + c_a * sb_ref[...]
        q = q_ref[pl.ds(r0, ch), :]
        k = k_ref[pl.ds(r0, ch), :]
        q = q * cos + swap_halves(q) * sin
        k = (k * cos + swap_halves(k) * sin) * FOX_SCALE
        qb = q.astype(BF16)
        kb = k.astype(BF16)
        vb = v_ref[pl.ds(r0, ch), :].astype(BF16)
        st = st_ref[...]
        o = _dot(qb, st.astype(BF16)) * xi_ref[...]
        for h in range(N_HEADS):
            mh = (lane // HEAD_DIM) == h
            qm = jnp.where(mh, q, 0.0).astype(BF16)
            att = _dot_nt(qm, kb) * dec_ref[h]
            o = o + jnp.where(mh, _dot(att.astype(BF16), vb), 0.0)
        kz = (k * zeta_ref[...]).astype(BF16)
        st_ref[...] = st * gmat_ref[...] + _dot_tn(kz, vb) * bd_ref[...]
        mu = gmean(o)
        d = o - mu
        var = gmean(d * d)
        on = d * lax.rsqrt(var + EPS) * gn_ref[...]
        gate = g_ref[pl.ds(r0, ch), :]
        y_ref[pl.ds(r0, ch), :] = (on * (gate * _sigmoid(gate))).astype(BF16)
        return carry

    lax.fori_loop(0, n_chunk, chunk, 0)

    @pl.when(i == pl.num_programs(1) - 1)
    def _():
        st = st_ref[...]
        acc = st[:, 0:HEAD_DIM]
        for h in range(1, N_HEADS):
            acc = acc + st[:, h * HEAD_DIM:(h + 1) * HEAD_DIM]
        sout_ref[...] = acc


def _retention(proj, gn, s0, nb, l, l_real, pos0, tm=512):
    m = proj.shape[0]
    tm = min(tm, l)
    ch = min(RET_CHUNK, tm)
    nl = l // tm
    t = _ret_tables(pos0, l, l_real, ch)
    rows = lambda b, i: b * nl + i
    const2 = lambda b, i: (0, 0)
    col = lambda cidx: pl.BlockSpec((tm, W_HEAD), lambda b, i: (rows(b, i), cidx))
    full = lambda a: pl.BlockSpec(a.shape, (lambda b, i: (0,) * a.ndim))
    in_specs = [col(COL_RQ), col(COL_RK), col(COL_RV), col(COL_RG),
                full(t["ca"]), full(t["sa"]), full(t["cb"]), full(t["sb"]), full(t["decay"]),
                full(t["xi"]), full(t["zeta"]), full(t["gmat"]), full(t["bd"]), full(t["bdavg"]),
                pl.BlockSpec((1, W_HEAD), const2),
                pl.BlockSpec((None, W_HEAD, HEAD_DIM), lambda b, i: (b, 0, 0))]
    return pl.pallas_call(
        functools.partial(_ret_body, tm, ch),
        grid=(nb, nl),
        in_specs=in_specs,
        out_specs=[pl.BlockSpec((tm, W_HEAD), lambda b, i: (rows(b, i), 0)),
                   pl.BlockSpec((None, W_HEAD, HEAD_DIM), lambda b, i: (b, 0, 0))],
        out_shape=[jax.ShapeDtypeStruct((m, W_HEAD), BF16),
                   jax.ShapeDtypeStruct((nb, W_HEAD, HEAD_DIM), F32)],
        scratch_shapes=[pltpu.VMEM((W_HEAD, W_HEAD), F32)],
        compiler_params=_params("arbitrary", "arbitrary"),
        name="retention",
    )(proj, proj, proj, proj, t["ca"], t["sa"], t["cb"], t["sb"], t["decay"], t["xi"], t["zeta"],
      t["gmat"], t["bd"], t["bdavg"], gn.reshape(1, W_HEAD), s0)


def _lru_body(tm, l_real, lx_ref, lg_ref, cw_ref, cb_ref, wr_ref, wi_ref, br_ref, bi_ref, lam_ref,
              conv0_ref, h0_ref, y_ref, hlast_ref, convnew_ref, xpad_ref, a_ref, b_ref, h_ref, hcar_ref):
    i = pl.program_id(1)
    w = lx_ref.shape[1]
    t_last, r_last = (l_real - 1) // tm, (l_real - 1) % tm

    @pl.when(i == 0)
    def _():
        xpad_ref[0:8, :] = conv0_ref[...]
        hcar_ref[...] = h0_ref[...]

    x = lx_ref[...]
    xpad_ref[8:8 + tm, :] = x
    xc = cb_ref[...] + cw_ref[CONV_W - 1:CONV_W, :] * x
    for j in range(1, CONV_W):
        xc = xc + cw_ref[CONV_W - 1 - j:CONV_W - j, :] * xpad_ref[pl.ds(8 - j, tm), :]

    xb = xc.astype(BF16)
    r = _sigmoid(_dot(xb, wr_ref[...]) + br_ref[...])
    ig = _sigmoid(_dot(xb, wi_ref[...]) + bi_ref[...])
    z = -lam_ref[...]
    softplus = jnp.maximum(z, 0.0) + jnp.log1p(jnp.exp(-jnp.abs(z)))
    log_a = (-LRU_C) * r * softplus
    a = jnp.exp(log_a)
    th = jnp.tanh(log_a)
    u = jnp.sqrt(-2.0 * th / (1.0 - th)) * (ig * xc)

    rowmod = lax.broadcasted_iota(jnp.int32, (tm, w), 0) % 8
    for sh in (1, 2, 4):
        a_sh = pltpu.roll(a, sh, 0)
        u_sh = pltpu.roll(u, sh, 0)
        valid = rowmod >= sh
        u = jnp.where(valid, a * u_sh + u, u)
        a = jnp.where(valid, a * a_sh, a)
    a_ref[...] = a
    b_ref[...] = u

    def group(gi, h):
        r0 = pl.multiple_of(gi * 8, 8)
        hg = a_ref[pl.ds(r0, 8), :] * h + b_ref[pl.ds(r0, 8), :]
        h_ref[pl.ds(r0, 8), :] = hg
        return hg[7:8, :]

    hcar_ref[...] = lax.fori_loop(0, tm // 8, group, hcar_ref[...])

    g = lg_ref[...]
    gelu = 0.5 * g * (1.0 + jnp.tanh(0.7978845608028654 * (g + 0.044715 * (g * g * g))))
    y_ref[...] = (h_ref[...] * gelu).astype(BF16)

    @pl.when(i == t_last)
    def _():
        hlast_ref[...] = h_ref[r_last:r_last + 1, :]
        convnew_ref[...] = xpad_ref[pl.ds(r_last + 1, 8), :]

    xpad_ref[0:8, :] = xpad_ref[tm:tm + 8, :]


def _lru(proj, cw, cb, wr, wi, br, bi, lam, conv0, h0, nb, l, l_real, tm=512):
    m = proj.shape[0]
    w = cw.shape[1]
    tm = min(tm, l)
    nl = l // tm
    rows = lambda b, i: b * nl + i
    const2 = lambda b, i: (0, 0)
    vec = pl.BlockSpec((1, w), const2)
    per_b = lambda r: pl.BlockSpec((None, r, w), lambda b, i: (b, 0, 0))
    return pl.pallas_call(
        functools.partial(_lru_body, tm, l_real),
        grid=(nb, nl),
        in_specs=[pl.BlockSpec((tm, w), lambda b, i: (rows(b, i), COL_LX)),
                  pl.BlockSpec((tm, w), lambda b, i: (rows(b, i), COL_LG)),
                  pl.BlockSpec((CONV_W, w), const2), vec,
                  pl.BlockSpec((w, w), const2), pl.BlockSpec((w, w), const2), vec, vec, vec,
                  per_b(8), per_b(1)],
        out_specs=[pl.BlockSpec((tm, w), lambda b, i: (rows(b, i), 0)), per_b(1), per_b(8)],
        out_shape=[jax.ShapeDtypeStruct((m, w), BF16), jax.ShapeDtypeStruct((nb, 1, w), F32),
                   jax.ShapeDtypeStruct((nb, 8, w), F32)],
        scratch_shapes=[pltpu.VMEM((tm + 8, w), F32), pltpu.VMEM((tm, w), F32), pltpu.VMEM((tm, w), F32),
                        pltpu.VMEM((tm, w), F32), pltpu.VMEM((1, w), F32)],
        compiler_params=_params("arbitrary", "arbitrary"),
        name="lru",
    )(proj, proj, cw, cb.reshape(1, w), wr, wi, br.reshape(1, w), bi.reshape(1, w), lam.reshape(1, w),
      conv0, h0)


def _pad16(x):
    return jnp.concatenate([x, jnp.zeros_like(x)], axis=0)


def _decode_body(n_pg, pt_ref, qn_ref, knew_ref, vnew_ref, lfnew_ref, *rest):
    del pt_ref
    k_refs, v_refs, lf_refs = rest[:n_pg], rest[n_pg:2 * n_pg], rest[2 * n_pg:3 * n_pg]
    o_ref, m_ref, l_ref, acc_ref, car_ref, qm_ref = rest[3 * n_pg:]
    p_i = pl.program_id(1)
    row = lax.broadcasted_iota(jnp.int32, (8, W_HEAD), 0)
    lane = lax.broadcasted_iota(jnp.int32, (8, W_HEAD), 1)
    own = (lane // HEAD_DIM) == row

    @pl.when(p_i == 0)
    def _():
        q = jnp.where(own, jnp.broadcast_to(qn_ref[0:1, :] * FOX_SCALE, (8, W_HEAD)), 0.0)
        qm_ref[...] = _pad16(q).astype(BF16)
        m_ref[...] = jnp.full_like(m_ref, NEG_INF)
        l_ref[...] = jnp.zeros_like(l_ref)
        acc_ref[...] = jnp.zeros_like(acc_ref)
        car_ref[...] = jnp.zeros_like(car_ref)

    qm = qm_ref[...]
    r_i = lax.broadcasted_iota(jnp.int32, (128, 128), 0)
    c_i = lax.broadcasted_iota(jnp.int32, (128, 128), 1)
    upper = jnp.where(r_i <= c_i, 1.0, 0.0).astype(BF16)

    def update(s, pv_fn):
        m_old = m_ref[...]
        m_new = jnp.maximum(m_old, jnp.max(s, axis=-1, keepdims=True))
        alpha = jnp.exp(m_old - m_new)
        p = jnp.exp(s - m_new)
        l_ref[...] = alpha * l_ref[...] + jnp.sum(p, axis=-1, keepdims=True)
        acc_ref[...] = alpha * acc_ref[...] + pv_fn(p)
        m_ref[...] = m_new

    for g in range(n_pg):
        kb = k_refs[g][...].astype(BF16)
        vb = v_refs[g][...].astype(BF16)
        hi, mid, lo = _split3(_pad16(lf_refs[g][...]))
        c = (_dot(hi, upper) + _dot(mid, upper) + _dot(lo, upper))[0:8, :] + car_ref[...]
        car_ref[...] = c[:, 127:128]
        s = _dot_nt(qm, kb)[0:8, :] - c
        update(s, lambda p, vb=vb: _dot(_pad16(p).astype(BF16), vb)[0:8, :])

    @pl.when(p_i == pl.num_programs(1) - 1)
    def _():
        lane128 = lax.broadcasted_iota(jnp.int32, (8, 128), 1)
        row128 = lax.broadcasted_iota(jnp.int32, (8, 128), 0)
        lf_col = jnp.sum(jnp.where(lane128 == row128, jnp.broadcast_to(lfnew_ref[0:1, :], (8, 128)), 0.0),
                         axis=-1, keepdims=True)
        k_new = knew_ref[0:1, :].astype(BF16).astype(F32)
        v_new = vnew_ref[0:1, :].astype(BF16).astype(F32)
        s_new = jnp.sum(qm[0:8, :].astype(F32) * k_new, axis=-1, keepdims=True) - (car_ref[...] + lf_col)
        update(s_new, lambda p: p.astype(BF16).astype(F32) * v_new)
        o_ref[...] = jnp.sum(jnp.where(own, acc_ref[...] / l_ref[...], 0.0), axis=0, keepdims=True)


def _fox_decode(layer, page_table, qn, kn, proj, logf, cache_k, cache_v, cache_lft, nb):
    n_pg = PAGES_PER_STEP
    n_pages = page_table.shape[1]
    page = cache_k.shape[2]
    sr = SAMPLE_ROWS
    tok = lambda cidx: (lambda b, p, pt: (b, cidx))
    in_specs = [pl.BlockSpec((sr, W_HEAD), tok(0)), pl.BlockSpec((sr, W_HEAD), tok(0)),
                pl.BlockSpec((sr, W_HEAD), tok(COL_FV)), pl.BlockSpec((sr, 128), tok(0))]
    args = [qn, kn, proj, logf]
    for arr, rws in ((cache_k, page), (cache_v, page), (cache_lft, 8)):
        for g in range(n_pg):
            in_specs.append(pl.BlockSpec(
                (None, None, rws, arr.shape[3]),
                lambda b, p, pt, g=g: (layer, pt[b, p * n_pg + g], 0, 0)))
            args.append(arr)
    grid_spec = pltpu.PrefetchScalarGridSpec(
        num_scalar_prefetch=1,
        grid=(nb, n_pages // n_pg),
        in_specs=in_specs,
        out_specs=pl.BlockSpec((None, 1, W_HEAD), lambda b, p, pt: (b, 0, 0)),
        scratch_shapes=[pltpu.VMEM((8, 1), F32), pltpu.VMEM((8, 1), F32), pltpu.VMEM((8, W_HEAD), F32),
                        pltpu.VMEM((8, 1), F32), pltpu.VMEM((16, W_HEAD), BF16)])
    return pl.pallas_call(
        functools.partial(_decode_body, n_pg),
        grid_spec=grid_spec,
        out_shape=jax.ShapeDtypeStruct((nb, 1, W_HEAD), F32),
        compiler_params=_params("arbitrary", "arbitrary"),
        name="fox_decode",
    )(page_table, *args)


def _block_diag(w):
    n, d, e = w.shape
    eye = jnp.eye(n, dtype=w.dtype)
    return (eye[:, None, :, None] * w[:, :, None, :]).reshape(n * d, n * e)


def _mixer_common(proj, nb, l, l_real, pos0, s0, conv0, h0, lw):
    y_ret, s_new = _retention(proj, lw["ret_gn"], s0, nb, l, l_real, pos0)
    y_lru, h_last, conv_new = _lru(proj, lw["conv_w"], lw["conv_b"], lw["wr"], lw["wi"], lw["br"], lw["bi"],
                                   lw["lam"], conv0, h0, nb, l, l_real)
    return y_ret, y_lru, s_new, h_last, conv_new


def kernel(x_prompt, x_sample, cache_k, cache_v, cache_logf, state_ret, state_lru, state_conv, page_table, norm_ffn1, ffn1_gate, ffn1_up, ffn1_down, norm_mix, w_in, ret_gn, conv_w, conv_b, lru_wr, lru_br, lru_wi, lru_bi, lru_lambda, fox_qn, fox_kn, fox_bf, w_out, norm_ffn2, ffn2_gate, ffn2_up, ffn2_down):
    bp, lp, d = x_prompt.shape
    bs, ls, _ = x_sample.shape
    assert ls == 1
    depth = w_in.shape[0]
    n_pool, page = cache_k.shape[1], cache_k.shape[2]
    past_len = page_table.shape[1] * page
    w_lru = conv_w.shape[2]
    sr = SAMPLE_ROWS
    in_width = w_in.shape[2]

    w_in_b = jnp.pad(w_in, ((0, 0), (0, 0), (0, PROJ_W - in_width))).astype(BF16)
    g1, u1, d1 = ffn1_gate.astype(BF16), ffn1_up.astype(BF16), ffn1_down.astype(BF16)
    g2, u2, d2 = ffn2_gate.astype(BF16), ffn2_up.astype(BF16), ffn2_down.astype(BF16)
    w_out_b = w_out.astype(BF16)
    cache_k4 = cache_k.reshape(depth, n_pool, page, W_HEAD)
    cache_v4 = cache_v.reshape(depth, n_pool, page, W_HEAD)
    cache_lft = jnp.pad(jnp.swapaxes(cache_logf, 2, 3), ((0, 0), (0, 0), (0, 8 - N_HEADS), (0, 0)))

    xp = x_prompt.reshape(bp * lp, d)
    xs = x_sample.reshape(bs, d)
    zero_s = jnp.zeros((bp, W_HEAD, HEAD_DIM), F32)
    zero_c = jnp.zeros((bp, 8, w_lru), F32)
    zero_h = jnp.zeros((bp, 1, w_lru), F32)

    outs_p = [[] for _ in range(6)]
    outs_s = [[] for _ in range(6)]
    for l in range(depth):
        lw = dict(ret_gn=ret_gn[l], conv_w=conv_w[l], conv_b=conv_b[l],
                  wr=_block_diag(lru_wr[l]).astype(BF16), wi=_block_diag(lru_wi[l]).astype(BF16),
                  br=lru_br[l], bi=lru_bi[l], lam=lru_lambda[l])

        xp = _ffn(xp, norm_ffn1[l], g1[l], u1[l], d1[l])
        proj = _inproj(xp, norm_mix[l], w_in_b[l])
        kn, logf, qa, ka, va = _fox_prep(proj, fox_qn[l], fox_kn[l], fox_bf[l], bp, lp, True)
        y_fox = _fox_attn(qa, ka, va)
        y_ret, y_lru, s_new, h_last, conv_new = _mixer_common(proj, bp, lp, lp, 0, zero_s, zero_c, zero_h, lw)
        xp = _ffn(xp, norm_ffn2[l], g2[l], u2[l], d2[l], mix=(y_ret, y_lru, y_fox, w_out_b[l]))
        outs_p[0].append(kn.reshape(bp, lp, N_HEADS, HEAD_DIM))
        outs_p[1].append(proj[:, COL_FV * W_HEAD:(COL_FV + 1) * W_HEAD].reshape(bp, lp, N_HEADS, HEAD_DIM))
        outs_p[2].append(logf[:, :N_HEADS].reshape(bp, lp, N_HEADS))
        outs_p[3].append(s_new.reshape(bp, N_HEADS, HEAD_DIM, HEAD_DIM))
        outs_p[4].append(h_last.reshape(bp, w_lru))
        outs_p[5].append(conv_new[:, 8 - (CONV_W - 1):, :])

        xs = _ffn(xs, norm_ffn1[l], g1[l], u1[l], d1[l])
        xs_pad = jnp.pad(xs[:, None, :], ((0, 0), (0, sr - 1), (0, 0))).reshape(bs * sr, d)
        proj_s = _inproj(xs_pad, norm_mix[l], w_in_b[l])
        kn_s, logf_s, qn_s = _fox_prep(proj_s, fox_qn[l], fox_kn[l], fox_bf[l], bs, sr, False)
        y_fox_s = _fox_decode(l, page_table, qn_s, kn_s, proj_s, logf_s, cache_k4, cache_v4, cache_lft, bs)
        s0 = state_ret[l].reshape(bs, W_HEAD, HEAD_DIM)
        conv0 = jnp.pad(state_conv[l], ((0, 0), (8 - (CONV_W - 1), 0), (0, 0)))
        h0 = state_lru[l].reshape(bs, 1, w_lru)
        y_ret_s, y_lru_s, s_new_s, h_last_s, conv_new_s = _mixer_common(
            proj_s, bs, sr, 1, past_len, s0, conv0, h0, lw)
        first = lambda a: a.reshape(bs, sr, a.shape[-1])[:, 0, :]
        xs = _ffn(xs, norm_ffn2[l], g2[l], u2[l], d2[l],
                  mix=(first(y_ret_s), first(y_lru_s), y_fox_s.reshape(bs, W_HEAD).astype(BF16), w_out_b[l]))
        outs_s[0].append(first(kn_s).reshape(bs, 1, N_HEADS, HEAD_DIM))
        outs_s[1].append(first(proj_s)[:, COL_FV * W_HEAD:(COL_FV + 1) * W_HEAD].reshape(bs, 1, N_HEADS, HEAD_DIM))
        outs_s[2].append(first(logf_s)[:, :N_HEADS].reshape(bs, 1, N_HEADS))
        outs_s[3].append(s_new_s.reshape(bs, N_HEADS, HEAD_DIM, HEAD_DIM))
        outs_s[4].append(h_last_s.reshape(bs, w_lru))
        outs_s[5].append(conv_new_s[:, 8 - (CONV_W - 1):, :])

    stk = lambda lst: jnp.stack(lst, axis=0)
    return (xp.reshape(bp, lp, d), xs.reshape(bs, 1, d),
            *[stk(o) for o in outs_p], *[stk(o) for o in outs_s])
```

```python
import functools

import numpy as np
import jax
import jax.numpy as jnp
from jax import lax
from jax.experimental import pallas as pl
from jax.experimental.pallas import tpu as pltpu

F32 = jnp.float32
BF16 = jnp.bfloat16

HEAD_DIM = 64
N_HEADS = 4
W_HEAD = N_HEADS * HEAD_DIM
CONV_W = 4
LRU_C = 8.0
RET_CHUNK = 128
ROPE_BASE = 10000.0
EPS = 1e-6
NEG_INF = -1e30
FOX_SCALE = HEAD_DIM ** -0.5
SAMPLE_ROWS = 16
PAGES_PER_STEP = 16
VMEM_LIMIT = 48 * 1024 * 1024

COL_RQ, COL_RK, COL_RV, COL_RG = 0, 1, 2, 3
COL_LX, COL_LG = 2, 3
COL_FQ, COL_FK, COL_FV = 8, 9, 10
COL_FF = 22
PROJ_W = 23 * 128


def _dot(a, b):
    return jnp.dot(a, b, preferred_element_type=F32)


def _dot_nt(a, b):
    return lax.dot_general(a, b, (((1,), (1,)), ((), ())), preferred_element_type=F32)


def _dot_tn(a, b):
    return lax.dot_general(a, b, (((0,), (0,)), ((), ())), preferred_element_type=F32)


def _sigmoid(x):
    return 1.0 / (1.0 + jnp.exp(-x))


def _split2(x):
    hi = x.astype(BF16)
    lo = (x - hi.astype(F32)).astype(BF16)
    return hi, lo


def _split3(x):
    hi = x.astype(BF16)
    r = x - hi.astype(F32)
    mid = r.astype(BF16)
    lo = (r - mid.astype(F32)).astype(BF16)
    return hi, mid, lo


def _params(*sem):
    return pltpu.CompilerParams(dimension_semantics=sem, vmem_limit_bytes=VMEM_LIMIT)


def _ffn_body(has_mix, tf, *refs):
    if has_mix:
        (x_ref, yr_ref, yl_ref, yf_ref, wo_ref, g_ref, wg_ref, wu_ref, wd_ref,
         o_ref, xres_ref, xn_ref, hid_ref) = refs
    else:
        x_ref, g_ref, wg_ref, wu_ref, wd_ref, o_ref, xres_ref, xn_ref, hid_ref = refs
    x = x_ref[...]
    if has_mix:
        w1 = yr_ref.shape[1]
        w2 = w1 + yl_ref.shape[1]
        x = (x + _dot(yr_ref[...], wo_ref[0:w1, :]) + _dot(yl_ref[...], wo_ref[w1:w2, :])
             + _dot(yf_ref[...], wo_ref[w2:, :]))
    xres_ref[...] = x
    ms = jnp.mean(x * x, axis=-1, keepdims=True)
    xn_ref[...] = (x * lax.rsqrt(ms + EPS) * g_ref[...]).astype(BF16)

    xn = xn_ref[...]
    for j in range(wg_ref.shape[1] // tf):
        gate = _dot(xn, wg_ref[:, j * tf:(j + 1) * tf])
        up = _dot(xn, wu_ref[:, j * tf:(j + 1) * tf])
        hid_ref[:, j * tf:(j + 1) * tf] = (gate * _sigmoid(gate) * up).astype(BF16)
    o_ref[...] = xres_ref[...] + 0.5 * _dot(hid_ref[...], wd_ref[...])


def _ffn(x, g, wg, wu, wd, mix=None, tm=512, tf=256):
    m, d = x.shape
    f = wg.shape[1]
    tm = min(tm, m)
    row = lambda i: (i, 0)
    resident = lambda a: pl.BlockSpec(a.shape, lambda i: (0, 0), pipeline_mode=pl.Buffered(1))
    in_specs = [pl.BlockSpec((tm, d), row)]
    args = [x]
    if mix is not None:
        yr, yl, yf, wo = mix
        in_specs += [pl.BlockSpec((tm, yr.shape[1]), row), pl.BlockSpec((tm, yl.shape[1]), row),
                     pl.BlockSpec((tm, yf.shape[1]), row), resident(wo)]
        args += [yr, yl, yf, wo]
    g2 = g.reshape(1, d)
    in_specs += [resident(g2), resident(wg), resident(wu), resident(wd)]
    args += [g2, wg, wu, wd]
    return pl.pallas_call(
        functools.partial(_ffn_body, mix is not None, tf),
        grid=(m // tm,),
        in_specs=in_specs,
        out_specs=pl.BlockSpec((tm, d), row),
        out_shape=jax.ShapeDtypeStruct((m, d), F32),
        scratch_shapes=[pltpu.VMEM((tm, d), F32), pltpu.VMEM((tm, d), BF16), pltpu.VMEM((tm, f), BF16)],
        compiler_params=_params("parallel"),
        name="ffn_mix" if mix is not None else "ffn",
    )(*args)


def _inproj_body(x_ref, g_ref, w_ref, o_ref):
    x = x_ref[...]
    ms = jnp.mean(x * x, axis=-1, keepdims=True)
    xn = (x * lax.rsqrt(ms + EPS) * g_ref[...]).astype(BF16)
    o_ref[...] = _dot(xn, w_ref[...])


def _inproj(x, g, w, tm=512):
    m, d = x.shape
    n = w.shape[1]
    tm = min(tm, m)
    return pl.pallas_call(
        _inproj_body,
        grid=(m // tm,),
        in_specs=[pl.BlockSpec((tm, d), lambda i: (i, 0)),
                  pl.BlockSpec((1, d), lambda i: (0, 0)),
                  pl.BlockSpec((d, n), lambda i: (0, 0))],
        out_specs=pl.BlockSpec((tm, n), lambda i: (i, 0)),
        out_shape=jax.ShapeDtypeStruct((m, n), F32),
        compiler_params=_params("parallel"),
        name="inproj",
    )(x, g.reshape(1, d), w)


def _head_norm(x, g):
    outs = []
    for h in range(N_HEADS):
        xh = x[:, h * HEAD_DIM:(h + 1) * HEAD_DIM]
        ms = jnp.mean(xh * xh, axis=-1, keepdims=True)
        outs.append(xh * lax.rsqrt(ms + EPS) * g)
    return outs


def _fox_prep_body(prompt, tm, fq_ref, fk_ref, fv_ref, ff_ref, gq_ref, gk_ref, bf_ref, *rest):
    lane = lax.broadcasted_iota(jnp.int32, (tm, 128), 1)
    z = ff_ref[...] + bf_ref[...]
    logf = jnp.minimum(z, 0.0) - jnp.log1p(jnp.exp(-jnp.abs(z)))
    logf = jnp.where(lane < N_HEADS, logf, 0.0)
    qh = _head_norm(fq_ref[...], gq_ref[...])
    kh = _head_norm(fk_ref[...], gk_ref[...])
    if not prompt:
        kn_ref, logf_ref, qn_ref = rest
        kn_ref[...] = jnp.concatenate(kh, axis=-1)
        qn_ref[...] = jnp.concatenate(qh, axis=-1)
        logf_ref[...] = logf
        return
    kn_ref, logf_ref, qa_ref, ka_ref, va_ref, carry_ref = rest
    kn_ref[...] = jnp.concatenate(kh, axis=-1)
    logf_ref[...] = logf

    @pl.when(pl.program_id(1) == 0)
    def _():
        carry_ref[...] = jnp.zeros_like(carry_ref)

    r_i = lax.broadcasted_iota(jnp.int32, (tm, tm), 0)
    c_i = lax.broadcasted_iota(jnp.int32, (tm, tm), 1)
    tri = jnp.where(c_i <= r_i, 1.0, 0.0).astype(BF16)
    hi, mid, lo = _split3(logf)
    c = _dot(tri, hi) + _dot(tri, mid) + _dot(tri, lo) + carry_ref[0:1, :]
    carry_ref[0:1, :] = c[tm - 1:tm, :]

    v = fv_ref[...]
    lane64 = lax.broadcasted_iota(jnp.int32, (tm, HEAD_DIM), 1)
    v_tail = jnp.where(lane64 == 0, 1.0, 0.0)
    for h in range(N_HEADS):
        ch = c[:, h:h + 1]
        chi = ch.astype(BF16).astype(F32)
        r1 = ch - chi
        cmid = r1.astype(BF16).astype(F32)
        clo = r1 - cmid
        q_tail = jnp.where(lane64 == 0, chi, jnp.where(lane64 == 1, cmid, jnp.where(
            lane64 == 2, clo, jnp.where(lane64 < 6, 1.0, 0.0))))
        k_tail = jnp.where(lane64 < 3, 1.0, jnp.where(lane64 == 3, -chi, jnp.where(
            lane64 == 4, -cmid, jnp.where(lane64 == 5, -clo, 0.0))))
        qa_ref[h] = jnp.concatenate([qh[h] * FOX_SCALE, q_tail], axis=-1).astype(BF16)
        ka_ref[h] = jnp.concatenate([kh[h], k_tail], axis=-1).astype(BF16)
        va_ref[h] = jnp.concatenate([v[:, h * HEAD_DIM:(h + 1) * HEAD_DIM], v_tail], axis=-1).astype(BF16)


def _fox_prep(proj, gq, gk, bf, nb, l, prompt, tm=256):
    m = proj.shape[0]
    tm = min(tm, l)
    nl = l // tm
    rows = lambda b, i: b * nl + i
    in_specs = [pl.BlockSpec((tm, W_HEAD), lambda b, i: (rows(b, i), COL_FQ)),
                pl.BlockSpec((tm, W_HEAD), lambda b, i: (rows(b, i), COL_FK)),
                pl.BlockSpec((tm, W_HEAD), lambda b, i: (rows(b, i), COL_FV)),
                pl.BlockSpec((tm, 128), lambda b, i: (rows(b, i), COL_FF)),
                pl.BlockSpec((1, HEAD_DIM), lambda b, i: (0, 0)),
                pl.BlockSpec((1, HEAD_DIM), lambda b, i: (0, 0)),
                pl.BlockSpec((1, 128), lambda b, i: (0, 0))]
    row_spec = lambda w: pl.BlockSpec((tm, w), lambda b, i: (rows(b, i), 0))
    out_specs = [row_spec(W_HEAD), row_spec(128)]
    out_shape = [jax.ShapeDtypeStruct((m, W_HEAD), F32), jax.ShapeDtypeStruct((m, 128), F32)]
    scratch = []
    if prompt:
        aug = pl.BlockSpec((None, N_HEADS, tm, 128), lambda b, i: (b, 0, i, 0))
        out_specs += [aug, aug, aug]
        out_shape += [jax.ShapeDtypeStruct((nb, N_HEADS, l, 128), BF16)] * 3
        scratch = [pltpu.VMEM((8, 128), F32)]
    else:
        out_specs += [row_spec(W_HEAD)]
        out_shape += [jax.ShapeDtypeStruct((m, W_HEAD), F32)]
    bf_pad = jnp.zeros((1, 128), F32).at[0, :N_HEADS].set(bf)
    return pl.pallas_call(
        functools.partial(_fox_prep_body, prompt, tm),
        grid=(nb, nl),
        in_specs=in_specs,
        out_specs=out_specs,
        out_shape=out_shape,
        scratch_shapes=scratch,
        compiler_params=_params("arbitrary", "arbitrary"),
        name="fox_prep_prompt" if prompt else "fox_prep_sample",
    )(proj, proj, proj, proj, gq.reshape(1, HEAD_DIM), gk.reshape(1, HEAD_DIM), bf_pad)


def _fox_attn_body(tq, tk, n_sub, qa_ref, ka_ref, va_ref, o_ref, m_ref, acc_ref):
    qi = pl.program_id(2)
    n_full = (qi * tq) // tk
    rs = tq // n_sub
    m_ref[...] = jnp.full_like(m_ref, NEG_INF)
    acc_ref[...] = jnp.zeros_like(acc_ref)

    def step(j, diag):
        k0 = pl.multiple_of(j * tk, tk)
        for h in range(2):
            k = ka_ref[h, pl.ds(k0, tk), :]
            v = va_ref[h, pl.ds(k0, tk), :]
            for r in range(n_sub):
                if diag is not None and diag * tk > r * rs + rs - 1:
                    continue
                c = h * n_sub + r
                s = _dot_nt(qa_ref[h, r * rs:(r + 1) * rs, :], k)
                if diag is not None and diag * tk + tk - 1 > r * rs:
                    qpos = r * rs + lax.broadcasted_iota(jnp.int32, (rs, tk), 0)
                    kpos = diag * tk + lax.broadcasted_iota(jnp.int32, (rs, tk), 1)
                    s = jnp.where(kpos <= qpos, s, NEG_INF)
                m_old = m_ref[c]
                m_new = jnp.maximum(m_old, jnp.max(s, axis=-1, keepdims=True))
                alpha = jnp.exp(m_old - m_new)
                p = jnp.exp(s - m_new)
                acc_ref[c] = alpha * acc_ref[c] + _dot(p.astype(BF16), v)
                m_ref[c] = m_new

    def full_step(j, carry):
        step(j, None)
        return carry

    lax.fori_loop(0, n_full, full_step, 0)
    for d in range(tq // tk):
        step(n_full + d, d)
    heads = []
    for h in range(2):
        rows = []
        for r in range(n_sub):
            acc = acc_ref[h * n_sub + r]
            rows.append(acc[:, :HEAD_DIM] / acc[:, HEAD_DIM:HEAD_DIM + 1])
        heads.append(jnp.concatenate(rows, axis=0))
    o_ref[...] = jnp.concatenate(heads, axis=-1).astype(BF16)


def _fox_attn(qa, ka, va, tq=512, tk=512, n_sub=2):
    nb, nh, l, _ = qa.shape
    tq = min(tq, l)
    tk = min(tk, tq)
    nq = l // tq
    rs = tq // n_sub
    return pl.pallas_call(
        functools.partial(_fox_attn_body, tq, tk, n_sub),
        grid=(nb, nh // 2, nq),
        in_specs=[pl.BlockSpec((None, 2, tq, 128), lambda b, hp, i: (b, hp, i, 0)),
                  pl.BlockSpec((None, 2, l, 128), lambda b, hp, i: (b, hp, 0, 0)),
                  pl.BlockSpec((None, 2, l, 128), lambda b, hp, i: (b, hp, 0, 0))],
        out_specs=pl.BlockSpec((tq, 128), lambda b, hp, i: (b * nq + i, hp)),
        out_shape=jax.ShapeDtypeStruct((nb * l, W_HEAD), BF16),
        scratch_shapes=[pltpu.VMEM((2 * n_sub, rs, 1), F32), pltpu.VMEM((2 * n_sub, rs, 128), F32)],
        compiler_params=_params("parallel", "parallel", "arbitrary"),
        name="fox_attn",
    )(qa, ka, va)


def _ret_tables(pos0, l_pad, l_real, chunk):
    c_real = chunk if l_real % chunk == 0 else l_real
    half = HEAD_DIM // 2
    inv = ROPE_BASE ** (-np.arange(half, dtype=np.float64) / half)
    inv_l = np.tile(inv, 2 * N_HEADS)
    sgn = np.tile(np.concatenate([-np.ones(half), np.ones(half)]), N_HEADS)
    n_a = l_pad // chunk
    ang_a = (pos0 + chunk * np.arange(n_a))[:, None] * inv_l[None, :]
    ang_b = np.arange(chunk)[:, None] * inv_l[None, :]
    log_g = np.log1p(-np.exp2(-5.0 - np.arange(N_HEADS, dtype=np.float64)))
    idx = np.arange(chunk, dtype=np.float64)
    diff = idx[:, None] - idx[None, :]
    decay = np.where(diff >= 0, np.exp(np.maximum(diff, 0.0)[None] * log_g[:, None, None]), 0.0)
    lg_l = np.repeat(log_g, HEAD_DIM)
    xi = np.exp((idx[:, None] + 1.0) * lg_l[None, :])
    zeta = np.exp((c_real - 1.0 - idx[:, None]) * lg_l[None, :])
    head = np.arange(W_HEAD) // HEAD_DIM
    bd = (head[:, None] == head[None, :]).astype(np.float64)
    gmat = bd * np.exp(c_real * lg_l)[:, None]
    f = lambda a: jnp.asarray(a, dtype=F32)
    return dict(ca=f(np.cos(ang_a)), sa=f(sgn * np.sin(ang_a)), cb=f(np.cos(ang_b)), sb=f(sgn * np.sin(ang_b)),
                decay=f(decay), xi=f(xi), zeta=f(zeta), gmat=f(gmat), bd=f(bd),
                bdavg=jnp.asarray(bd / HEAD_DIM, dtype=BF16))


def _ret_body(tm, ch, q_ref, k_ref, v_ref, g_ref, ca_ref, sa_ref, cb_ref, sb_ref, dec_ref, xi_ref,
              zeta_ref, gmat_ref, bd_ref, bdavg_ref, gn_ref, s0_ref, y_ref, sout_ref, st_ref):
    i = pl.program_id(1)
    n_chunk = tm // ch

    @pl.when(i == 0)
    def _():
        s0 = s0_ref[...]
        st_ref[...] = jnp.concatenate([s0] * N_HEADS, axis=-1) * bd_ref[...]

    lane = lax.broadcasted_iota(jnp.int32, (ch, W_HEAD), 1)
    lane128 = lax.broadcasted_iota(jnp.int32, (ch, 128), 1)
    first_half = (lane128 % HEAD_DIM) < (HEAD_DIM // 2)

    def swap_halves(x):
        parts = []
        for blk in range(W_HEAD // 128):
            xb = x[:, blk * 128:(blk + 1) * 128]
            parts.append(jnp.where(first_half, pltpu.roll(xb, 128 - HEAD_DIM // 2, 1),
                                   pltpu.roll(xb, HEAD_DIM // 2, 1)))
        return jnp.concatenate(parts, axis=-1)

    def gmean(t):
        hi, lo = _split2(t)
        return _dot(hi, bdavg_ref[...]) + _dot(lo, bdavg_ref[...])

    def chunk(c, carry):
        r0 = pl.multiple_of(c * ch, ch)
        a = i * n_chunk + c
        c_a = ca_ref[pl.ds(a, 1), :]
        s_a = sa_ref[pl.ds(a, 1), :]
        cos = c_a * cb_ref[...] - s_a * sb_ref[...]
        sin = s_a * cb_ref[...] + c_a * sb_ref[...]
        q = q_ref[pl.ds(r0, ch), :]
        k = k_ref[pl.ds(r0, ch), :]
        q = q * cos + swap_halves(q) * sin
        k = (k * cos + swap_halves(k) * sin) * FOX_SCALE
        qb = q.astype(BF16)
        kb = k.astype(BF16)
        vb = v_ref[pl.ds(r0, ch), :].astype(BF16)
        st = st_ref[...]
        o = _dot(qb, st.astype(BF16)) * xi_ref[...]
        for h in range(N_HEADS):
            mh = (lane // HEAD_DIM) == h
            qm = jnp.where(mh, q, 0.0).astype(BF16)
            att = _dot_nt(qm, kb) * dec_ref[h]
            o = o + jnp.where(mh, _dot(att.astype(BF16), vb), 0.0)
        kz = (k * zeta_ref[...]).astype(BF16)
        st_ref[...] = st * gmat_ref[...] + _dot_tn(kz, vb) * bd_ref[...]
        mu = gmean(o)
        d = o - mu
        var = gmean(d * d)
        on = d * lax.rsqrt(var + EPS) * gn_ref[...]
        gate = g_ref[pl.ds(r0, ch), :]
        y_ref[pl.ds(r0, ch), :] = (on * (gate * _sigmoid(gate))).astype(BF16)
        return carry

    lax.fori_loop(0, n_chunk, chunk, 0)

    @pl.when(i == pl.num_programs(1) - 1)
    def _():
        st = st_ref[...]
        acc = st[:, 0:HEAD_DIM]
        for h in range(1, N_HEADS):
            acc = acc + st[:, h * HEAD_DIM:(h + 1) * HEAD_DIM]
        sout_ref[...] = acc


def _retention(proj, gn, s0, nb, l, l_real, pos0, tm=512):
    m = proj.shape[0]
    tm = min(tm, l)
    ch = min(RET_CHUNK, tm)
    nl = l // tm
    t = _ret_tables(pos0, l, l_real, ch)
    rows = lambda b, i: b * nl + i
    const2 = lambda b, i: (0, 0)
    col = lambda cidx: pl.BlockSpec((tm, W_HEAD), lambda b, i: (rows(b, i), cidx))
    full = lambda a: pl.BlockSpec(a.shape, (lambda b, i: (0,) * a.ndim))
    in_specs = [col(COL_RQ), col(COL_RK), col(COL_RV), col(COL_RG),
                full(t["ca"]), full(t["sa"]), full(t["cb"]), full(t["sb"]), full(t["decay"]),
                full(t["xi"]), full(t["zeta"]), full(t["gmat"]), full(t["bd"]), full(t["bdavg"]),
                pl.BlockSpec((1, W_HEAD), const2),
                pl.BlockSpec((None, W_HEAD, HEAD_DIM), lambda b, i: (b, 0, 0))]
    return pl.pallas_call(
        functools.partial(_ret_body, tm, ch),
        grid=(nb, nl),
        in_specs=in_specs,
        out_specs=[pl.BlockSpec((tm, W_HEAD), lambda b, i: (rows(b, i), 0)),
                   pl.BlockSpec((None, W_HEAD, HEAD_DIM), lambda b, i: (b, 0, 0))],
        out_shape=[jax.ShapeDtypeStruct((m, W_HEAD), BF16),
                   jax.ShapeDtypeStruct((nb, W_HEAD, HEAD_DIM), F32)],
        scratch_shapes=[pltpu.VMEM((W_HEAD, W_HEAD), F32)],
        compiler_params=_params("arbitrary", "arbitrary"),
        name="retention",
    )(proj, proj, proj, proj, t["ca"], t["sa"], t["cb"], t["sb"], t["decay"], t["xi"], t["zeta"],
      t["gmat"], t["bd"], t["bdavg"], gn.reshape(1, W_HEAD), s0)


def _lru_body(tm, l_real, lx_ref, lg_ref, cw_ref, cb_ref, wr_ref, wi_ref, br_ref, bi_ref, lam_ref,
              conv0_ref, h0_ref, y_ref, hlast_ref, convnew_ref, xpad_ref, a_ref, b_ref, h_ref, hcar_ref):
    i = pl.program_id(1)
    w = lx_ref.shape[1]
    t_last, r_last = (l_real - 1) // tm, (l_real - 1) % tm

    @pl.when(i == 0)
    def _():
        xpad_ref[0:8, :] = conv0_ref[...]
        hcar_ref[...] = h0_ref[...]

    x = lx_ref[...]
    xpad_ref[8:8 + tm, :] = x
    xc = cb_ref[...] + cw_ref[CONV_W - 1:CONV_W, :] * x
    for j in range(1, CONV_W):
        xc = xc + cw_ref[CONV_W - 1 - j:CONV_W - j, :] * xpad_ref[pl.ds(8 - j, tm), :]

    xb = xc.astype(BF16)
    r = _sigmoid(_dot(xb, wr_ref[...]) + br_ref[...])
    ig = _sigmoid(_dot(xb, wi_ref[...]) + bi_ref[...])
    z = -lam_ref[...]
    softplus = jnp.maximum(z, 0.0) + jnp.log1p(jnp.exp(-jnp.abs(z)))
    log_a = (-LRU_C) * r * softplus
    a = jnp.exp(log_a)
    th = jnp.tanh(log_a)
    u = jnp.sqrt(-2.0 * th / (1.0 - th)) * (ig * xc)

    rowmod = lax.broadcasted_iota(jnp.int32, (tm, w), 0) % 8
    for sh in (1, 2, 4):
        a_sh = pltpu.roll(a, sh, 0)
        u_sh = pltpu.roll(u, sh, 0)
        valid = rowmod >= sh
        u = jnp.where(valid, a * u_sh + u, u)
        a = jnp.where(valid, a * a_sh, a)
    a_ref[...] = a
    b_ref[...] = u

    def group(gi, h):
        r0 = pl.multiple_of(gi * 8, 8)
        hg = a_ref[pl.ds(r0, 8), :] * h + b_ref[pl.ds(r0, 8), :]
        h_ref[pl.ds(r0, 8), :] = hg
        return hg[7:8, :]

    hcar_ref[...] = lax.fori_loop(0, tm // 8, group, hcar_ref[...])

    g = lg_ref[...]
    gelu = 0.5 * g * (1.0 + jnp.tanh(0.7978845608028654 * (g + 0.044715 * (g * g * g))))
    y_ref[...] = (h_ref[...] * gelu).astype(BF16)

    @pl.when(i == t_last)
    def _():
        hlast_ref[...] = h_ref[r_last:r_last + 1, :]
        convnew_ref[...] = xpad_ref[pl.ds(r_last + 1, 8), :]

    xpad_ref[0:8, :] = xpad_ref[tm:tm + 8, :]


def _lru(proj, cw, cb, wr, wi, br, bi, lam, conv0, h0, nb, l, l_real, tm=512):
    m = proj.shape[0]
    w = cw.shape[1]
    tm = min(tm, l)
    nl = l // tm
    rows = lambda b, i: b * nl + i
    const2 = lambda b, i: (0, 0)
    vec = pl.BlockSpec((1, w), const2)
    per_b = lambda r: pl.BlockSpec((None, r, w), lambda b, i: (b, 0, 0))
    return pl.pallas_call(
        functools.partial(_lru_body, tm, l_real),
        grid=(nb, nl),
        in_specs=[pl.BlockSpec((tm, w), lambda b, i: (rows(b, i), COL_LX)),
                  pl.BlockSpec((tm, w), lambda b, i: (rows(b, i), COL_LG)),
                  pl.BlockSpec((CONV_W, w), const2), vec,
                  pl.BlockSpec((w, w), const2), pl.BlockSpec((w, w), const2), vec, vec, vec,
                  per_b(8), per_b(1)],
        out_specs=[pl.BlockSpec((tm, w), lambda b, i: (rows(b, i), 0)), per_b(1), per_b(8)],
        out_shape=[jax.ShapeDtypeStruct((m, w), BF16), jax.ShapeDtypeStruct((nb, 1, w), F32),
                   jax.ShapeDtypeStruct((nb, 8, w), F32)],
        scratch_shapes=[pltpu.VMEM((tm + 8, w), F32), pltpu.VMEM((tm, w), F32), pltpu.VMEM((tm, w), F32),
                        pltpu.VMEM((tm, w), F32), pltpu.VMEM((1, w), F32)],
        compiler_params=_params("arbitrary", "arbitrary"),
        name="lru",
    )(proj, proj, cw, cb.reshape(1, w), wr, wi, br.reshape(1, w), bi.reshape(1, w), lam.reshape(1, w),
      conv0, h0)


def _pad16(x):
    return jnp.concatenate([x, jnp.zeros_like(x)], axis=0)


def _decode_body(n_pg, pt_ref, qn_ref, knew_ref, vnew_ref, lfnew_ref, *rest):
    del pt_ref
    k_refs, v_refs, lf_refs = rest[:n_pg], rest[n_pg:2 * n_pg], rest[2 * n_pg:3 * n_pg]
    o_ref, m_ref, l_ref, acc_ref, car_ref, qm_ref = rest[3 * n_pg:]
    p_i = pl.program_id(1)
    page = k_refs[0].shape[1]
    row = lax.broadcasted_iota(jnp.int32, (8, W_HEAD), 0)
    lane = lax.broadcasted_iota(jnp.int32, (8, W_HEAD), 1)
    own = (lane // HEAD_DIM) == row

    @pl.when(p_i == 0)
    def _():
        q = jnp.where(own, jnp.broadcast_to(qn_ref[0:1, :] * FOX_SCALE, (8, W_HEAD)), 0.0)
        qm_ref[...] = _pad16(q).astype(BF16)
        m_ref[...] = jnp.full_like(m_ref, NEG_INF)
        l_ref[...] = jnp.zeros_like(l_ref)
        acc_ref[...] = jnp.zeros_like(acc_ref)
        car_ref[...] = jnp.zeros_like(car_ref)

    qm = qm_ref[...]

    def update(s, pv_fn):
        m_old = m_ref[...]
        m_new = jnp.maximum(m_old, jnp.max(s, axis=-1, keepdims=True))
        alpha = jnp.exp(m_old - m_new)
        p = jnp.exp(s - m_new)
        l_ref[...] = alpha * l_ref[...] + jnp.sum(p, axis=-1, keepdims=True)
        acc_ref[...] = alpha * acc_ref[...] + pv_fn(p)
        m_ref[...] = m_new

    r_i = lax.broadcasted_iota(jnp.int32, (page, page), 0)
    c_i = lax.broadcasted_iota(jnp.int32, (page, page), 1)
    upper = jnp.where(r_i <= c_i, 1.0, 0.0).astype(BF16)
    hi, mid, lo = _split3(jnp.concatenate([lf_refs[g][...] for g in range(n_pg)], axis=0))
    c_all = _dot(hi, upper) + _dot(mid, upper) + _dot(lo, upper)
    nr = n_pg * 8
    rr = lax.broadcasted_iota(jnp.int32, (nr, nr), 0)
    cc = lax.broadcasted_iota(jnp.int32, (nr, nr), 1)
    before = jnp.where((rr % 8 == cc % 8) & (cc // 8 < rr // 8), 1.0, 0.0).astype(BF16)
    chi, cmid, clo = _split3(c_all)
    pref = _dot(before, chi) + _dot(before, cmid) + _dot(before, clo)
    c_all = c_all + pref[:, page - 1:page] + jnp.concatenate([car_ref[...]] * n_pg, axis=0)
    car_ref[...] = c_all[nr - 8:nr, page - 1:page]
    s_parts = [_dot(qm, k_refs[g][...].astype(BF16))[0:8, :] - c_all[g * 8:(g + 1) * 8, :]
               for g in range(n_pg)]

    def pv_pages(p):
        pb = _pad16(p).astype(BF16)
        pv = _dot_nt(pb[:, 0:page], v_refs[0][...].astype(BF16))
        for g in range(1, n_pg):
            pv = pv + _dot_nt(pb[:, g * page:(g + 1) * page], v_refs[g][...].astype(BF16))
        return pv[0:8, :]

    update(jnp.concatenate(s_parts, axis=-1), pv_pages)

    @pl.when(p_i == pl.num_programs(1) - 1)
    def _():
        lane128 = lax.broadcasted_iota(jnp.int32, (8, 128), 1)
        row128 = lax.broadcasted_iota(jnp.int32, (8, 128), 0)
        lf_col = jnp.sum(jnp.where(lane128 == row128, jnp.broadcast_to(lfnew_ref[0:1, :], (8, 128)), 0.0),
                         axis=-1, keepdims=True)
        k_new = knew_ref[0:1, :].astype(BF16).astype(F32)
        v_new = vnew_ref[0:1, :].astype(BF16).astype(F32)
        s_new = jnp.sum(qm[0:8, :].astype(F32) * k_new, axis=-1, keepdims=True) - (car_ref[...] + lf_col)
        update(s_new, lambda p: p.astype(BF16).astype(F32) * v_new)
        o_ref[...] = jnp.sum(jnp.where(own, acc_ref[...] / l_ref[...], 0.0), axis=0, keepdims=True)


def _fox_decode(layer, page_table, qn, kn, proj, logf, cache_kt, cache_vt, cache_lft, nb):
    n_pages = page_table.shape[1]
    n_pg = min(PAGES_PER_STEP, n_pages)
    sr = SAMPLE_ROWS
    tok = lambda cidx: (lambda b, p, pt: (b, cidx))
    in_specs = [pl.BlockSpec((sr, W_HEAD), tok(0)), pl.BlockSpec((sr, W_HEAD), tok(0)),
                pl.BlockSpec((sr, W_HEAD), tok(COL_FV)), pl.BlockSpec((sr, 128), tok(0))]
    args = [qn, kn, proj, logf]
    for arr in (cache_kt, cache_vt, cache_lft):
        for g in range(n_pg):
            in_specs.append(pl.BlockSpec(
                (None, None) + arr.shape[2:],
                lambda b, p, pt, g=g: (layer, pt[b, p * n_pg + g], 0, 0)))
            args.append(arr)
    grid_spec = pltpu.PrefetchScalarGridSpec(
        num_scalar_prefetch=1,
        grid=(nb, n_pages // n_pg),
        in_specs=in_specs,
        out_specs=pl.BlockSpec((None, 1, W_HEAD), lambda b, p, pt: (b, 0, 0)),
        scratch_shapes=[pltpu.VMEM((8, 1), F32), pltpu.VMEM((8, 1), F32), pltpu.VMEM((8, W_HEAD), F32),
                        pltpu.VMEM((8, 1), F32), pltpu.VMEM((16, W_HEAD), BF16)])
    return pl.pallas_call(
        functools.partial(_decode_body, n_pg),
        grid_spec=grid_spec,
        out_shape=jax.ShapeDtypeStruct((nb, 1, W_HEAD), F32),
        compiler_params=_params("arbitrary", "arbitrary"),
        name="fox_decode",
    )(page_table, *args)


def _block_diag(w):
    n, d, e = w.shape
    eye = jnp.eye(n, dtype=w.dtype)
    return (eye[:, None, :, None] * w[:, :, None, :]).reshape(n * d, n * e)


def _mixer_common(proj, nb, l, l_real, pos0, s0, conv0, h0, lw):
    y_ret, s_new = _retention(proj, lw["ret_gn"], s0, nb, l, l_real, pos0)
    y_lru, h_last, conv_new = _lru(proj, lw["conv_w"], lw["conv_b"], lw["wr"], lw["wi"], lw["br"], lw["bi"],
                                   lw["lam"], conv0, h0, nb, l, l_real)
    return y_ret, y_lru, s_new, h_last, conv_new


def kernel(x_prompt, x_sample, cache_k, cache_v, cache_logf, state_ret, state_lru, state_conv, page_table, norm_ffn1, ffn1_gate, ffn1_up, ffn1_down, norm_mix, w_in, ret_gn, conv_w, conv_b, lru_wr, lru_br, lru_wi, lru_bi, lru_lambda, fox_qn, fox_kn, fox_bf, w_out, norm_ffn2, ffn2_gate, ffn2_up, ffn2_down):
    bp, lp, d = x_prompt.shape
    bs, ls, _ = x_sample.shape
    assert ls == 1
    depth = w_in.shape[0]
    n_pool, page = cache_k.shape[1], cache_k.shape[2]
    past_len = page_table.shape[1] * page
    w_lru = conv_w.shape[2]
    sr = SAMPLE_ROWS
    in_width = w_in.shape[2]

    w_in_b = jnp.pad(w_in, ((0, 0), (0, 0), (0, PROJ_W - in_width))).astype(BF16)
    g1, u1, d1 = ffn1_gate.astype(BF16), ffn1_up.astype(BF16), ffn1_down.astype(BF16)
    g2, u2, d2 = ffn2_gate.astype(BF16), ffn2_up.astype(BF16), ffn2_down.astype(BF16)
    w_out_b = w_out.astype(BF16)
    cache_k4 = jnp.transpose(cache_k, (0, 1, 3, 4, 2)).reshape(depth, n_pool, W_HEAD, page)
    cache_v4 = jnp.transpose(cache_v, (0, 1, 3, 4, 2)).reshape(depth, n_pool, W_HEAD, page)
    cache_lft = jnp.pad(jnp.swapaxes(cache_logf, 2, 3), ((0, 0), (0, 0), (0, 8 - N_HEADS), (0, 0)))

    xp = x_prompt.reshape(bp * lp, d)
    xs = x_sample.reshape(bs, d)
    zero_s = jnp.zeros((bp, W_HEAD, HEAD_DIM), F32)
    zero_c = jnp.zeros((bp, 8, w_lru), F32)
    zero_h = jnp.zeros((bp, 1, w_lru), F32)

    outs_p = [[] for _ in range(6)]
    outs_s = [[] for _ in range(6)]
    for l in range(depth):
        lw = dict(ret_gn=ret_gn[l], conv_w=conv_w[l], conv_b=conv_b[l],
                  wr=_block_diag(lru_wr[l]).astype(BF16), wi=_block_diag(lru_wi[l]).astype(BF16),
                  br=lru_br[l], bi=lru_bi[l], lam=lru_lambda[l])

        xp = _ffn(xp, norm_ffn1[l], g1[l], u1[l], d1[l])
        proj = _inproj(xp, norm_mix[l], w_in_b[l])
        kn, logf, qa, ka, va = _fox_prep(proj, fox_qn[l], fox_kn[l], fox_bf[l], bp, lp, True)
        y_fox = _fox_attn(qa, ka, va)
        y_ret, y_lru, s_new, h_last, conv_new = _mixer_common(proj, bp, lp, lp, 0, zero_s, zero_c, zero_h, lw)
        xp = _ffn(xp, norm_ffn2[l], g2[l], u2[l], d2[l], mix=(y_ret, y_lru, y_fox, w_out_b[l]))
        outs_p[0].append(kn.reshape(bp, lp, N_HEADS, HEAD_DIM))
        outs_p[1].append(proj[:, COL_FV * W_HEAD:(COL_FV + 1) * W_HEAD].reshape(bp, lp, N_HEADS, HEAD_DIM))
        outs_p[2].append(logf[:, :N_HEADS].reshape(bp, lp, N_HEADS))
        outs_p[3].append(s_new.reshape(bp, N_HEADS, HEAD_DIM, HEAD_DIM))
        outs_p[4].append(h_last.reshape(bp, w_lru))
        outs_p[5].append(conv_new[:, 8 - (CONV_W - 1):, :])

        xs = _ffn(xs, norm_ffn1[l], g1[l], u1[l], d1[l])
        xs_pad = jnp.pad(xs[:, None, :], ((0, 0), (0, sr - 1), (0, 0))).reshape(bs * sr, d)
        proj_s = _inproj(xs_pad, norm_mix[l], w_in_b[l])
        kn_s, logf_s, qn_s = _fox_prep(proj_s, fox_qn[l], fox_kn[l], fox_bf[l], bs, sr, False)
        y_fox_s = _fox_decode(l, page_table, qn_s, kn_s, proj_s, logf_s, cache_k4, cache_v4, cache_lft, bs)
        s0 = state_ret[l].reshape(bs, W_HEAD, HEAD_DIM)
        conv0 = jnp.pad(state_conv[l], ((0, 0), (8 - (CONV_W - 1), 0), (0, 0)))
        h0 = state_lru[l].reshape(bs, 1, w_lru)
        y_ret_s, y_lru_s, s_new_s, h_last_s, conv_new_s = _mixer_common(
            proj_s, bs, sr, 1, past_len, s0, conv0, h0, lw)
        first = lambda a: a.reshape(bs, sr, a.shape[-1])[:, 0, :]
        xs = _ffn(xs, norm_ffn2[l], g2[l], u2[l], d2[l],
                  mix=(first(y_ret_s), first(y_lru_s), y_fox_s.reshape(bs, W_HEAD).astype(BF16), w_out_b[l]))
        outs_s[0].append(first(kn_s).reshape(bs, 1, N_HEADS, HEAD_DIM))
        outs_s[1].append(first(proj_s)[:, COL_FV * W_HEAD:(COL_FV + 1) * W_HEAD].reshape(bs, 1, N_HEADS, HEAD_DIM))
        outs_s[2].append(first(logf_s)[:, :N_HEADS].reshape(bs, 1, N_HEADS))
        outs_s[3].append(s_new_s.reshape(bs, N_HEADS, HEAD_DIM, HEAD_DIM))
        outs_s[4].append(h_last_s.reshape(bs, w_lru))
        outs_s[5].append(conv_new_s[:, 8 - (CONV_W - 1):, :])

    stk = lambda lst: jnp.stack(lst, axis=0)
    return (xp.reshape(bp, lp, d), xs.reshape(bs, 1, d),
            *[stk(o) for o in outs_p], *[stk(o) for o in outs_s])
```

```python
import functools

import numpy as np
import jax
import jax.numpy as jnp
from jax import lax
from jax.experimental import pallas as pl
from jax.experimental.pallas import tpu as pltpu

F32 = jnp.float32
BF16 = jnp.bfloat16

HEAD_DIM = 64
N_HEADS = 4
W_HEAD = N_HEADS * HEAD_DIM
CONV_W = 4
LRU_C = 8.0
RET_CHUNK = 128
ROPE_BASE = 10000.0
EPS = 1e-6
NEG_INF = -1e30
FOX_SCALE = HEAD_DIM ** -0.5
SAMPLE_ROWS = 16
PAGES_PER_STEP = 16
VMEM_LIMIT = 48 * 1024 * 1024

COL_RQ, COL_RK, COL_RV, COL_RG = 0, 1, 2, 3
COL_LX, COL_LG = 2, 3
COL_FQ, COL_FK, COL_FV = 8, 9, 10
COL_FF = 22
PROJ_W = 23 * 128


def _dot(a, b):
    return jnp.dot(a, b, preferred_element_type=F32)


def _dot_nt(a, b):
    return lax.dot_general(a, b, (((1,), (1,)), ((), ())), preferred_element_type=F32)


def _dot_tn(a, b):
    return lax.dot_general(a, b, (((0,), (0,)), ((), ())), preferred_element_type=F32)


def _sigmoid(x):
    return 1.0 / (1.0 + jnp.exp(-x))


def _split2(x):
    hi = x.astype(BF16)
    lo = (x - hi.astype(F32)).astype(BF16)
    return hi, lo


def _split3(x):
    hi = x.astype(BF16)
    r = x - hi.astype(F32)
    mid = r.astype(BF16)
    lo = (r - mid.astype(F32)).astype(BF16)
    return hi, mid, lo


def _params(*sem):
    return pltpu.CompilerParams(dimension_semantics=sem, vmem_limit_bytes=VMEM_LIMIT)


def _ffn_body(has_mix, tf, *refs):
    if has_mix:
        (x_ref, yr_ref, yl_ref, yf_ref, wo_ref, g_ref, wg_ref, wu_ref, wd_ref,
         o_ref, xres_ref, xn_ref, hid_ref) = refs
    else:
        x_ref, g_ref, wg_ref, wu_ref, wd_ref, o_ref, xres_ref, xn_ref, hid_ref = refs
    x = x_ref[...]
    if has_mix:
        w1 = yr_ref.shape[1]
        w2 = w1 + yl_ref.shape[1]
        x = (x + _dot(yr_ref[...], wo_ref[0:w1, :]) + _dot(yl_ref[...], wo_ref[w1:w2, :])
             + _dot(yf_ref[...], wo_ref[w2:, :]))
    xres_ref[...] = x
    ms = jnp.mean(x * x, axis=-1, keepdims=True)
    xn_ref[...] = (x * lax.rsqrt(ms + EPS) * g_ref[...]).astype(BF16)

    xn = xn_ref[...]
    for j in range(wg_ref.shape[1] // tf):
        gate = _dot(xn, wg_ref[:, j * tf:(j + 1) * tf])
        up = _dot(xn, wu_ref[:, j * tf:(j + 1) * tf])
        hid_ref[:, j * tf:(j + 1) * tf] = (gate * _sigmoid(gate) * up).astype(BF16)
    o_ref[...] = xres_ref[...] + 0.5 * _dot(hid_ref[...], wd_ref[...])


def _ffn(x, g, wg, wu, wd, mix=None, tm=512, tf=256):
    m, d = x.shape
    f = wg.shape[1]
    tm = min(tm, m)
    row = lambda i: (i, 0)
    resident = lambda a: pl.BlockSpec(a.shape, lambda i: (0, 0), pipeline_mode=pl.Buffered(1))
    in_specs = [pl.BlockSpec((tm, d), row)]
    args = [x]
    if mix is not None:
        yr, yl, yf, wo = mix
        in_specs += [pl.BlockSpec((tm, yr.shape[1]), row), pl.BlockSpec((tm, yl.shape[1]), row),
                     pl.BlockSpec((tm, yf.shape[1]), row), resident(wo)]
        args += [yr, yl, yf, wo]
    g2 = g.reshape(1, d)
    in_specs += [resident(g2), resident(wg), resident(wu), resident(wd)]
    args += [g2, wg, wu, wd]
    return pl.pallas_call(
        functools.partial(_ffn_body, mix is not None, tf),
        grid=(m // tm,),
        in_specs=in_specs,
        out_specs=pl.BlockSpec((tm, d), row),
        out_shape=jax.ShapeDtypeStruct((m, d), F32),
        scratch_shapes=[pltpu.VMEM((tm, d), F32), pltpu.VMEM((tm, d), BF16), pltpu.VMEM((tm, f), BF16)],
        compiler_params=_params("parallel"),
        name="ffn_mix" if mix is not None else "ffn",
    )(*args)


def _inproj_body(x_ref, g_ref, w_ref, o_ref):
    x = x_ref[...]
    ms = jnp.mean(x * x, axis=-1, keepdims=True)
    xn = (x * lax.rsqrt(ms + EPS) * g_ref[...]).astype(BF16)
    o_ref[...] = _dot(xn, w_ref[...])


def _inproj(x, g, w, tm=512):
    m, d = x.shape
    n = w.shape[1]
    tm = min(tm, m)
    return pl.pallas_call(
        _inproj_body,
        grid=(m // tm,),
        in_specs=[pl.BlockSpec((tm, d), lambda i: (i, 0)),
                  pl.BlockSpec((1, d), lambda i: (0, 0)),
                  pl.BlockSpec((d, n), lambda i: (0, 0))],
        out_specs=pl.BlockSpec((tm, n), lambda i: (i, 0)),
        out_shape=jax.ShapeDtypeStruct((m, n), F32),
        compiler_params=_params("parallel"),
        name="inproj",
    )(x, g.reshape(1, d), w)


def _head_norm(x, g):
    outs = []
    for h in range(N_HEADS):
        xh = x[:, h * HEAD_DIM:(h + 1) * HEAD_DIM]
        ms = jnp.mean(xh * xh, axis=-1, keepdims=True)
        outs.append(xh * lax.rsqrt(ms + EPS) * g)
    return outs


def _fox_prep_body(prompt, tm, fq_ref, fk_ref, fv_ref, ff_ref, gq_ref, gk_ref, bf_ref, *rest):
    lane = lax.broadcasted_iota(jnp.int32, (tm, 128), 1)
    z = ff_ref[...] + bf_ref[...]
    logf = jnp.minimum(z, 0.0) - jnp.log1p(jnp.exp(-jnp.abs(z)))
    logf = jnp.where(lane < N_HEADS, logf, 0.0)
    qh = _head_norm(fq_ref[...], gq_ref[...])
    kh = _head_norm(fk_ref[...], gk_ref[...])
    if not prompt:
        kn_ref, logf_ref, qn_ref = rest
        kn_ref[...] = jnp.concatenate(kh, axis=-1)
        qn_ref[...] = jnp.concatenate(qh, axis=-1)
        logf_ref[...] = logf
        return
    kn_ref, logf_ref, qa_ref, ka_ref, va_ref, carry_ref = rest
    kn_ref[...] = jnp.concatenate(kh, axis=-1)
    logf_ref[...] = logf

    @pl.when(pl.program_id(1) == 0)
    def _():
        carry_ref[...] = jnp.zeros_like(carry_ref)

    r_i = lax.broadcasted_iota(jnp.int32, (tm, tm), 0)
    c_i = lax.broadcasted_iota(jnp.int32, (tm, tm), 1)
    tri = jnp.where(c_i <= r_i, 1.0, 0.0).astype(BF16)
    hi, mid, lo = _split3(logf)
    c = _dot(tri, hi) + _dot(tri, mid) + _dot(tri, lo) + carry_ref[0:1, :]
    carry_ref[0:1, :] = c[tm - 1:tm, :]

    v = fv_ref[...]
    lane64 = lax.broadcasted_iota(jnp.int32, (tm, HEAD_DIM), 1)
    v_tail = jnp.where(lane64 == 0, 1.0, 0.0)
    for h in range(N_HEADS):
        ch = c[:, h:h + 1]
        chi = ch.astype(BF16).astype(F32)
        r1 = ch - chi
        cmid = r1.astype(BF16).astype(F32)
        clo = r1 - cmid
        q_tail = jnp.where(lane64 == 0, chi, jnp.where(lane64 == 1, cmid, jnp.where(
            lane64 == 2, clo, jnp.where(lane64 < 6, 1.0, 0.0))))
        k_tail = jnp.where(lane64 < 3, 1.0, jnp.where(lane64 == 3, -chi, jnp.where(
            lane64 == 4, -cmid, jnp.where(lane64 == 5, -clo, 0.0))))
        qa_ref[h] = jnp.concatenate([qh[h] * FOX_SCALE, q_tail], axis=-1).T.astype(BF16)
        ka_ref[h] = jnp.concatenate([kh[h], k_tail], axis=-1).astype(BF16)
        va_ref[h] = jnp.concatenate([v[:, h * HEAD_DIM:(h + 1) * HEAD_DIM], v_tail], axis=-1).T.astype(BF16)


def _fox_prep(proj, gq, gk, bf, nb, l, prompt, tm=256):
    m = proj.shape[0]
    tm = min(tm, l)
    nl = l // tm
    rows = lambda b, i: b * nl + i
    in_specs = [pl.BlockSpec((tm, W_HEAD), lambda b, i: (rows(b, i), COL_FQ)),
                pl.BlockSpec((tm, W_HEAD), lambda b, i: (rows(b, i), COL_FK)),
                pl.BlockSpec((tm, W_HEAD), lambda b, i: (rows(b, i), COL_FV)),
                pl.BlockSpec((tm, 128), lambda b, i: (rows(b, i), COL_FF)),
                pl.BlockSpec((1, HEAD_DIM), lambda b, i: (0, 0)),
                pl.BlockSpec((1, HEAD_DIM), lambda b, i: (0, 0)),
                pl.BlockSpec((1, 128), lambda b, i: (0, 0))]
    row_spec = lambda w: pl.BlockSpec((tm, w), lambda b, i: (rows(b, i), 0))
    out_specs = [row_spec(W_HEAD), row_spec(128)]
    out_shape = [jax.ShapeDtypeStruct((m, W_HEAD), F32), jax.ShapeDtypeStruct((m, 128), F32)]
    scratch = []
    if prompt:
        aug = pl.BlockSpec((None, N_HEADS, tm, 128), lambda b, i: (b, 0, i, 0))
        aug_t = pl.BlockSpec((None, N_HEADS, 128, tm), lambda b, i: (b, 0, 0, i))
        out_specs += [aug_t, aug, aug_t]
        out_shape += [jax.ShapeDtypeStruct((nb, N_HEADS, 128, l), BF16),
                      jax.ShapeDtypeStruct((nb, N_HEADS, l, 128), BF16),
                      jax.ShapeDtypeStruct((nb, N_HEADS, 128, l), BF16)]
        scratch = [pltpu.VMEM((8, 128), F32)]
    else:
        out_specs += [row_spec(W_HEAD)]
        out_shape += [jax.ShapeDtypeStruct((m, W_HEAD), F32)]
    bf_pad = jnp.zeros((1, 128), F32).at[0, :N_HEADS].set(bf)
    return pl.pallas_call(
        functools.partial(_fox_prep_body, prompt, tm),
        grid=(nb, nl),
        in_specs=in_specs,
        out_specs=out_specs,
        out_shape=out_shape,
        scratch_shapes=scratch,
        compiler_params=_params("arbitrary", "arbitrary"),
        name="fox_prep_prompt" if prompt else "fox_prep_sample",
    )(proj, proj, proj, proj, gq.reshape(1, HEAD_DIM), gk.reshape(1, HEAD_DIM), bf_pad)


def _fox_attn_body(tq, qt_ref, ka_ref, vt_ref, o_ref, s_ref, m_ref, acc_ref):
    qi = pl.program_id(2)
    tk = tq // 2
    heads = range(2)
    m_ref[...] = jnp.full_like(m_ref, NEG_INF)
    acc_ref[...] = jnp.zeros_like(acc_ref)

    def scores(slot, j):
        k0 = pl.multiple_of(j * tk, tk)
        for h in heads:
            s_ref[slot, h] = _dot(ka_ref[h, pl.ds(k0, tk), :], qt_ref[h])

    def absorb(h, s, v_t, lanes):
        m_old = m_ref[h, :, lanes]
        m_new = jnp.maximum(m_old, jnp.max(s, axis=0, keepdims=True))
        alpha = jnp.exp(m_old - m_new)
        p = jnp.exp(s - m_new)
        acc_ref[h, :, lanes] = alpha * acc_ref[h, :, lanes] + _dot(v_t, p.astype(BF16))
        m_ref[h, :, lanes] = m_new

    def consume(slot, j):
        k0 = pl.multiple_of(j * tk, tk)
        for h in heads:
            absorb(h, s_ref[slot, h], vt_ref[h, :, pl.ds(k0, tk)], slice(None))

    scores(0, 0)

    def pair(jj, carry):
        scores(1, 2 * jj + 1)
        consume(0, 2 * jj)
        scores(0, 2 * jj + 2)
        consume(1, 2 * jj + 1)
        return carry

    lax.fori_loop(0, qi, pair, 0)

    d0 = pl.multiple_of(2 * qi * tk, tk)
    d1 = pl.multiple_of((2 * qi + 1) * tk, tk)
    upper_half = slice(tk, tq)
    s_late = [_dot(ka_ref[h, pl.ds(d1, tk), :], qt_ref[h, :, upper_half]) for h in heads]
    def causal(n):
        return lax.broadcasted_iota(jnp.int32, (tk, n), 0) <= lax.broadcasted_iota(jnp.int32, (tk, n), 1)

    for h in heads:
        absorb(h, jnp.where(causal(tq), s_ref[0, h], NEG_INF), vt_ref[h, :, pl.ds(d0, tk)], slice(None))
    for h in heads:
        absorb(h, jnp.where(causal(tk), s_late[h], NEG_INF), vt_ref[h, :, pl.ds(d1, tk)], upper_half)

    outs = []
    for h in heads:
        acc = acc_ref[h]
        outs.append(acc[0:HEAD_DIM, :] / acc[HEAD_DIM:HEAD_DIM + 1, :])
    o_ref[...] = jnp.concatenate(outs, axis=0).T.astype(BF16)


def _fox_attn(qt, ka, vt, tq=512):
    nb, nh, l, _ = ka.shape
    tq = min(tq, l)
    nq = l // tq
    return pl.pallas_call(
        functools.partial(_fox_attn_body, tq),
        grid=(nb, nh // 2, nq),
        in_specs=[pl.BlockSpec((None, 2, 128, tq), lambda b, hp, i: (b, hp, 0, i)),
                  pl.BlockSpec((None, 2, l, 128), lambda b, hp, i: (b, hp, 0, 0)),
                  pl.BlockSpec((None, 2, 128, l), lambda b, hp, i: (b, hp, 0, 0))],
        out_specs=pl.BlockSpec((tq, 128), lambda b, hp, i: (b * nq + i, hp)),
        out_shape=jax.ShapeDtypeStruct((nb * l, W_HEAD), BF16),
        scratch_shapes=[pltpu.VMEM((2, 2, tq // 2, tq), F32), pltpu.VMEM((2, 1, tq), F32),
                        pltpu.VMEM((2, 128, tq), F32)],
        compiler_params=_params("parallel", "parallel", "arbitrary"),
        name="fox_attn",
    )(qt, ka, vt)


def _ret_tables(pos0, l_pad, l_real, chunk):
    c_real = chunk if l_real % chunk == 0 else l_real
    half = HEAD_DIM // 2
    inv = ROPE_BASE ** (-np.arange(half, dtype=np.float64) / half)
    inv_l = np.tile(inv, 2 * N_HEADS)
    sgn = np.tile(np.concatenate([-np.ones(half), np.ones(half)]), N_HEADS)
    n_a = l_pad // chunk
    ang_a = (pos0 + chunk * np.arange(n_a))[:, None] * inv_l[None, :]
    ang_b = np.arange(chunk)[:, None] * inv_l[None, :]
    log_g = np.log1p(-np.exp2(-5.0 - np.arange(N_HEADS, dtype=np.float64)))
    idx = np.arange(chunk, dtype=np.float64)
    diff = idx[:, None] - idx[None, :]
    decay = np.where(diff >= 0, np.exp(np.maximum(diff, 0.0)[None] * log_g[:, None, None]), 0.0)
    lg_l = np.repeat(log_g, HEAD_DIM)
    xi = np.exp((idx[:, None] + 1.0) * lg_l[None, :])
    zeta = np.exp((c_real - 1.0 - idx[:, None]) * lg_l[None, :])
    head = np.arange(W_HEAD) // HEAD_DIM
    bd = (head[:, None] == head[None, :]).astype(np.float64)
    gmat = bd * np.exp(c_real * lg_l)[:, None]
    f = lambda a: jnp.asarray(a, dtype=F32)
    return dict(ca=f(np.cos(ang_a)), sa=f(sgn * np.sin(ang_a)), cb=f(np.cos(ang_b)), sb=f(sgn * np.sin(ang_b)),
                decay=f(decay), xi=f(xi), zeta=f(zeta), gmat=f(gmat), bd=f(bd),
                bdavg=jnp.asarray(bd / HEAD_DIM, dtype=BF16))


def _ret_body(tm, ch, q_ref, k_ref, v_ref, g_ref, ca_ref, sa_ref, cb_ref, sb_ref, dec_ref, xi_ref,
              zeta_ref, gmat_ref, bd_ref, bdavg_ref, gn_ref, s0_ref, y_ref, sout_ref, st_ref):
    i = pl.program_id(1)
    n_chunk = tm // ch

    @pl.when(i == 0)
    def _():
        s0 = s0_ref[...]
        st_ref[...] = jnp.concatenate([s0] * N_HEADS, axis=-1) * bd_ref[...]

    lane = lax.broadcasted_iota(jnp.int32, (ch, W_HEAD), 1)
    lane128 = lax.broadcasted_iota(jnp.int32, (ch, 128), 1)
    first_half = (lane128 % HEAD_DIM) < (HEAD_DIM // 2)

    def swap_halves(x):
        parts = []
        for blk in range(W_HEAD // 128):
            xb = x[:, blk * 128:(blk + 1) * 128]
            parts.append(jnp.where(first_half, pltpu.roll(xb, 128 - HEAD_DIM // 2, 1),
                                   pltpu.roll(xb, HEAD_DIM // 2, 1)))
        return jnp.concatenate(parts, axis=-1)

    def gmean(t):
        hi, lo = _split2(t)
        return _dot(hi, bdavg_ref[...]) + _dot(lo, bdavg_ref[...])

    def chunk(c, carry):
        r0 = pl.multiple_of(c * ch, ch)
        a = i * n_chunk + c
        c_a = ca_ref[pl.ds(a, 1), :]
        s_a = sa_ref[pl.ds(a, 1), :]
        cos = c_a * cb_ref[...] - s_a * sb_ref[...]
        sin = s_a * cb_ref[...] + c_a * sb_ref[...]
        q = q_ref[pl.ds(r0, ch), :]
        k = k_ref[pl.ds(r0, ch), :]
        q = q * cos + swap_halves(q) * sin
        k = (k * cos + swap_halves(k) * sin) * FOX_SCALE
        qb = q.astype(BF16)
        kb = k.astype(BF16)
        vb = v_ref[pl.ds(r0, ch), :].astype(BF16)
        st = st_ref[...]
        o = _dot(qb, st.astype(BF16)) * xi_ref[...]
        for h in range(N_HEADS):
            mh = (lane // HEAD_DIM) == h
            qm = jnp.where(mh, q, 0.0).astype(BF16)
            att = _dot_nt(qm, kb) * dec_ref[h]
            o = o + jnp.where(mh, _dot(att.astype(BF16), vb), 0.0)
        kz = (k * zeta_ref[...]).astype(BF16)
        st_ref[...] = st * gmat_ref[...] + _dot_tn(kz, vb) * bd_ref[...]
        mu = gmean(o)
        d = o - mu
        var = gmean(d * d)
        on = d * lax.rsqrt(var + EPS) * gn_ref[...]
        gate = g_ref[pl.ds(r0, ch), :]
        y_ref[pl.ds(r0, ch), :] = (on * (gate * _sigmoid(gate))).astype(BF16)
        return carry

    lax.fori_loop(0, n_chunk, chunk, 0)

    @pl.when(i == pl.num_programs(1) - 1)
    def _():
        st = st_ref[...]
        acc = st[:, 0:HEAD_DIM]
        for h in range(1, N_HEADS):
            acc = acc + st[:, h * HEAD_DIM:(h + 1) * HEAD_DIM]
        sout_ref[...] = acc


def _retention(proj, gn, s0, nb, l, l_real, pos0, tm=512):
    m = proj.shape[0]
    tm = min(tm, l)
    ch = min(RET_CHUNK, tm)
    nl = l // tm
    t = _ret_tables(pos0, l, l_real, ch)
    rows = lambda b, i: b * nl + i
    const2 = lambda b, i: (0, 0)
    col = lambda cidx: pl.BlockSpec((tm, W_HEAD), lambda b, i: (rows(b, i), cidx))
    full = lambda a: pl.BlockSpec(a.shape, (lambda b, i: (0,) * a.ndim))
    in_specs = [col(COL_RQ), col(COL_RK), col(COL_RV), col(COL_RG),
                full(t["ca"]), full(t["sa"]), full(t["cb"]), full(t["sb"]), full(t["decay"]),
                full(t["xi"]), full(t["zeta"]), full(t["gmat"]), full(t["bd"]), full(t["bdavg"]),
                pl.BlockSpec((1, W_HEAD), const2),
                pl.BlockSpec((None, W_HEAD, HEAD_DIM), lambda b, i: (b, 0, 0))]
    return pl.pallas_call(
        functools.partial(_ret_body, tm, ch),
        grid=(nb, nl),
        in_specs=in_specs,
        out_specs=[pl.BlockSpec((tm, W_HEAD), lambda b, i: (rows(b, i), 0)),
                   pl.BlockSpec((None, W_HEAD, HEAD_DIM), lambda b, i: (b, 0, 0))],
        out_shape=[jax.ShapeDtypeStruct((m, W_HEAD), BF16),
                   jax.ShapeDtypeStruct((nb, W_HEAD, HEAD_DIM), F32)],
        scratch_shapes=[pltpu.VMEM((W_HEAD, W_HEAD), F32)],
        compiler_params=_params("arbitrary", "arbitrary"),
        name="retention",
    )(proj, proj, proj, proj, t["ca"], t["sa"], t["cb"], t["sb"], t["decay"], t["xi"], t["zeta"],
      t["gmat"], t["bd"], t["bdavg"], gn.reshape(1, W_HEAD), s0)


def _lru_body(tm, l_real, lx_ref, lg_ref, cw_ref, cb_ref, wr_ref, wi_ref, br_ref, bi_ref, lam_ref,
              conv0_ref, h0_ref, y_ref, hlast_ref, convnew_ref, xpad_ref, a_ref, b_ref, h_ref, hcar_ref):
    i = pl.program_id(1)
    w = lx_ref.shape[1]
    t_last, r_last = (l_real - 1) // tm, (l_real - 1) % tm

    @pl.when(i == 0)
    def _():
        xpad_ref[0:8, :] = conv0_ref[...]
        hcar_ref[...] = h0_ref[...]

    x = lx_ref[...]
    xpad_ref[8:8 + tm, :] = x
    xc = cb_ref[...] + cw_ref[CONV_W - 1:CONV_W, :] * x
    for j in range(1, CONV_W):
        xc = xc + cw_ref[CONV_W - 1 - j:CONV_W - j, :] * xpad_ref[pl.ds(8 - j, tm), :]

    xb = xc.astype(BF16)
    r = _sigmoid(_dot(xb, wr_ref[...]) + br_ref[...])
    ig = _sigmoid(_dot(xb, wi_ref[...]) + bi_ref[...])
    z = -lam_ref[...]
    softplus = jnp.maximum(z, 0.0) + jnp.log1p(jnp.exp(-jnp.abs(z)))
    log_a = (-LRU_C) * r * softplus
    a = jnp.exp(log_a)
    th = jnp.tanh(log_a)
    u = jnp.sqrt(-2.0 * th / (1.0 - th)) * (ig * xc)

    rowmod = lax.broadcasted_iota(jnp.int32, (tm, w), 0) % 8
    for sh in (1, 2, 4):
        a_sh = pltpu.roll(a, sh, 0)
        u_sh = pltpu.roll(u, sh, 0)
        valid = rowmod >= sh
        u = jnp.where(valid, a * u_sh + u, u)
        a = jnp.where(valid, a * a_sh, a)
    a_ref[...] = a
    b_ref[...] = u

    def group(gi, h):
        r0 = pl.multiple_of(gi * 8, 8)
        hg = a_ref[pl.ds(r0, 8), :] * h + b_ref[pl.ds(r0, 8), :]
        h_ref[pl.ds(r0, 8), :] = hg
        return hg[7:8, :]

    hcar_ref[...] = lax.fori_loop(0, tm // 8, group, hcar_ref[...])

    g = lg_ref[...]
    gelu = 0.5 * g * (1.0 + jnp.tanh(0.7978845608028654 * (g + 0.044715 * (g * g * g))))
    y_ref[...] = (h_ref[...] * gelu).astype(BF16)

    @pl.when(i == t_last)
    def _():
        hlast_ref[...] = h_ref[r_last:r_last + 1, :]
        convnew_ref[...] = xpad_ref[pl.ds(r_last + 1, 8), :]

    xpad_ref[0:8, :] = xpad_ref[tm:tm + 8, :]


def _lru(proj, cw, cb, wr, wi, br, bi, lam, conv0, h0, nb, l, l_real, tm=512):
    m = proj.shape[0]
    w = cw.shape[1]
    tm = min(tm, l)
    nl = l // tm
    rows = lambda b, i: b * nl + i
    const2 = lambda b, i: (0, 0)
    vec = pl.BlockSpec((1, w), const2)
    per_b = lambda r: pl.BlockSpec((None, r, w), lambda b, i: (b, 0, 0))
    return pl.pallas_call(
        functools.partial(_lru_body, tm, l_real),
        grid=(nb, nl),
        in_specs=[pl.BlockSpec((tm, w), lambda b, i: (rows(b, i), COL_LX)),
                  pl.BlockSpec((tm, w), lambda b, i: (rows(b, i), COL_LG)),
                  pl.BlockSpec((CONV_W, w), const2), vec,
                  pl.BlockSpec((w, w), const2), pl.BlockSpec((w, w), const2), vec, vec, vec,
                  per_b(8), per_b(1)],
        out_specs=[pl.BlockSpec((tm, w), lambda b, i: (rows(b, i), 0)), per_b(1), per_b(8)],
        out_shape=[jax.ShapeDtypeStruct((m, w), BF16), jax.ShapeDtypeStruct((nb, 1, w), F32),
                   jax.ShapeDtypeStruct((nb, 8, w), F32)],
        scratch_shapes=[pltpu.VMEM((tm + 8, w), F32), pltpu.VMEM((tm, w), F32), pltpu.VMEM((tm, w), F32),
                        pltpu.VMEM((tm, w), F32), pltpu.VMEM((1, w), F32)],
        compiler_params=_params("arbitrary", "arbitrary"),
        name="lru",
    )(proj, proj, cw, cb.reshape(1, w), wr, wi, br.reshape(1, w), bi.reshape(1, w), lam.reshape(1, w),
      conv0, h0)


def _pad16(x):
    return jnp.concatenate([x, jnp.zeros_like(x)], axis=0)


def _decode_body(n_pg, pt_ref, qn_ref, knew_ref, vnew_ref, lfnew_ref, *rest):
    del pt_ref
    k_refs, v_refs, lf_refs = rest[:n_pg], rest[n_pg:2 * n_pg], rest[2 * n_pg:3 * n_pg]
    o_ref, m_ref, l_ref, acc_ref, car_ref, qm_ref = rest[3 * n_pg:]
    p_i = pl.program_id(1)
    page = k_refs[0].shape[1]
    row = lax.broadcasted_iota(jnp.int32, (8, W_HEAD), 0)
    lane = lax.broadcasted_iota(jnp.int32, (8, W_HEAD), 1)
    own = (lane // HEAD_DIM) == row

    @pl.when(p_i == 0)
    def _():
        q = jnp.where(own, jnp.broadcast_to(qn_ref[0:1, :] * FOX_SCALE, (8, W_HEAD)), 0.0)
        qm_ref[...] = _pad16(q).astype(BF16)
        m_ref[...] = jnp.full_like(m_ref, NEG_INF)
        l_ref[...] = jnp.zeros_like(l_ref)
        acc_ref[...] = jnp.zeros_like(acc_ref)
        car_ref[...] = jnp.zeros_like(car_ref)

    qm = qm_ref[...]

    def update(s, pv_fn):
        m_old = m_ref[...]
        m_new = jnp.maximum(m_old, jnp.max(s, axis=-1, keepdims=True))
        alpha = jnp.exp(m_old - m_new)
        p = jnp.exp(s - m_new)
        l_ref[...] = alpha * l_ref[...] + jnp.sum(p, axis=-1, keepdims=True)
        acc_ref[...] = alpha * acc_ref[...] + pv_fn(p)
        m_ref[...] = m_new

    r_i = lax.broadcasted_iota(jnp.int32, (page, page), 0)
    c_i = lax.broadcasted_iota(jnp.int32, (page, page), 1)
    upper = jnp.where(r_i <= c_i, 1.0, 0.0).astype(BF16)
    hi, mid, lo = _split3(jnp.concatenate([lf_refs[g][...] for g in range(n_pg)], axis=0))
    c_all = _dot(hi, upper) + _dot(mid, upper) + _dot(lo, upper)
    nr = n_pg * 8
    rr = lax.broadcasted_iota(jnp.int32, (nr, nr), 0)
    cc = lax.broadcasted_iota(jnp.int32, (nr, nr), 1)
    before = jnp.where((rr % 8 == cc % 8) & (cc // 8 < rr // 8), 1.0, 0.0).astype(BF16)
    chi, cmid, clo = _split3(c_all)
    pref = _dot(before, chi) + _dot(before, cmid) + _dot(before, clo)
    c_all = c_all + pref[:, page - 1:page] + jnp.concatenate([car_ref[...]] * n_pg, axis=0)
    car_ref[...] = c_all[nr - 8:nr, page - 1:page]
    s_parts = [_dot(qm, k_refs[g][...].astype(BF16))[0:8, :] - c_all[g * 8:(g + 1) * 8, :]
               for g in range(n_pg)]

    def pv_pages(p):
        pb = _pad16(p).astype(BF16)
        pv = _dot_nt(pb[:, 0:page], v_refs[0][...].astype(BF16))
        for g in range(1, n_pg):
            pv = pv + _dot_nt(pb[:, g * page:(g + 1) * page], v_refs[g][...].astype(BF16))
        return pv[0:8, :]

    update(jnp.concatenate(s_parts, axis=-1), pv_pages)

    @pl.when(p_i == pl.num_programs(1) - 1)
    def _():
        lane128 = lax.broadcasted_iota(jnp.int32, (8, 128), 1)
        row128 = lax.broadcasted_iota(jnp.int32, (8, 128), 0)
        lf_col = jnp.sum(jnp.where(lane128 == row128, jnp.broadcast_to(lfnew_ref[0:1, :], (8, 128)), 0.0),
                         axis=-1, keepdims=True)
        k_new = knew_ref[0:1, :].astype(BF16).astype(F32)
        v_new = vnew_ref[0:1, :].astype(BF16).astype(F32)
        s_new = jnp.sum(qm[0:8, :].astype(F32) * k_new, axis=-1, keepdims=True) - (car_ref[...] + lf_col)
        update(s_new, lambda p: p.astype(BF16).astype(F32) * v_new)
        o_ref[...] = jnp.sum(jnp.where(own, acc_ref[...] / l_ref[...], 0.0), axis=0, keepdims=True)


def _fox_decode(layer, page_table, qn, kn, proj, logf, cache_kt, cache_vt, cache_lft, nb):
    n_pages = page_table.shape[1]
    n_pg = min(PAGES_PER_STEP, n_pages)
    sr = SAMPLE_ROWS
    tok = lambda cidx: (lambda b, p, pt: (b, cidx))
    in_specs = [pl.BlockSpec((sr, W_HEAD), tok(0)), pl.BlockSpec((sr, W_HEAD), tok(0)),
                pl.BlockSpec((sr, W_HEAD), tok(COL_FV)), pl.BlockSpec((sr, 128), tok(0))]
    args = [qn, kn, proj, logf]
    for arr in (cache_kt, cache_vt, cache_lft):
        for g in range(n_pg):
            in_specs.append(pl.BlockSpec(
                (None, None) + arr.shape[2:],
                lambda b, p, pt, g=g: (layer, pt[b, p * n_pg + g], 0, 0)))
            args.append(arr)
    grid_spec = pltpu.PrefetchScalarGridSpec(
        num_scalar_prefetch=1,
        grid=(nb, n_pages // n_pg),
        in_specs=in_specs,
        out_specs=pl.BlockSpec((None, 1, W_HEAD), lambda b, p, pt: (b, 0, 0)),
        scratch_shapes=[pltpu.VMEM((8, 1), F32), pltpu.VMEM((8, 1), F32), pltpu.VMEM((8, W_HEAD), F32),
                        pltpu.VMEM((8, 1), F32), pltpu.VMEM((16, W_HEAD), BF16)])
    return pl.pallas_call(
        functools.partial(_decode_body, n_pg),
        grid_spec=grid_spec,
        out_shape=jax.ShapeDtypeStruct((nb, 1, W_HEAD), F32),
        compiler_params=_params("arbitrary", "arbitrary"),
        name="fox_decode",
    )(page_table, *args)


def _block_diag(w):
    n, d, e = w.shape
    eye = jnp.eye(n, dtype=w.dtype)
    return (eye[:, None, :, None] * w[:, :, None, :]).reshape(n * d, n * e)


def _mixer_common(proj, nb, l, l_real, pos0, s0, conv0, h0, lw):
    y_ret, s_new = _retention(proj, lw["ret_gn"], s0, nb, l, l_real, pos0)
    y_lru, h_last, conv_new = _lru(proj, lw["conv_w"], lw["conv_b"], lw["wr"], lw["wi"], lw["br"], lw["bi"],
                                   lw["lam"], conv0, h0, nb, l, l_real)
    return y_ret, y_lru, s_new, h_last, conv_new


def kernel(x_prompt, x_sample, cache_k, cache_v, cache_logf, state_ret, state_lru, state_conv, page_table, norm_ffn1, ffn1_gate, ffn1_up, ffn1_down, norm_mix, w_in, ret_gn, conv_w, conv_b, lru_wr, lru_br, lru_wi, lru_bi, lru_lambda, fox_qn, fox_kn, fox_bf, w_out, norm_ffn2, ffn2_gate, ffn2_up, ffn2_down):
    bp, lp, d = x_prompt.shape
    bs, ls, _ = x_sample.shape
    assert ls == 1
    depth = w_in.shape[0]
    n_pool, page = cache_k.shape[1], cache_k.shape[2]
    past_len = page_table.shape[1] * page
    w_lru = conv_w.shape[2]
    sr = SAMPLE_ROWS
    in_width = w_in.shape[2]

    w_in_b = jnp.pad(w_in, ((0, 0), (0, 0), (0, PROJ_W - in_width))).astype(BF16)
    g1, u1, d1 = ffn1_gate.astype(BF16), ffn1_up.astype(BF16), ffn1_down.astype(BF16)
    g2, u2, d2 = ffn2_gate.astype(BF16), ffn2_up.astype(BF16), ffn2_down.astype(BF16)
    w_out_b = w_out.astype(BF16)
    cache_k4 = jnp.transpose(cache_k, (0, 1, 3, 4, 2)).reshape(depth, n_pool, W_HEAD, page)
    cache_v4 = jnp.transpose(cache_v, (0, 1, 3, 4, 2)).reshape(depth, n_pool, W_HEAD, page)
    cache_lft = jnp.pad(jnp.swapaxes(cache_logf, 2, 3), ((0, 0), (0, 0), (0, 8 - N_HEADS), (0, 0)))

    xp = x_prompt.reshape(bp * lp, d)
    xs = x_sample.reshape(bs, d)
    zero_s = jnp.zeros((bp, W_HEAD, HEAD_DIM), F32)
    zero_c = jnp.zeros((bp, 8, w_lru), F32)
    zero_h = jnp.zeros((bp, 1, w_lru), F32)

    outs_p = [[] for _ in range(6)]
    outs_s = [[] for _ in range(6)]
    for l in range(depth):
        lw = dict(ret_gn=ret_gn[l], conv_w=conv_w[l], conv_b=conv_b[l],
                  wr=_block_diag(lru_wr[l]).astype(BF16), wi=_block_diag(lru_wi[l]).astype(BF16),
                  br=lru_br[l], bi=lru_bi[l], lam=lru_lambda[l])

        xp = _ffn(xp, norm_ffn1[l], g1[l], u1[l], d1[l])
        proj = _inproj(xp, norm_mix[l], w_in_b[l])
        kn, logf, qa, ka, va = _fox_prep(proj, fox_qn[l], fox_kn[l], fox_bf[l], bp, lp, True)
        y_fox = _fox_attn(qa, ka, va)
        y_ret, y_lru, s_new, h_last, conv_new = _mixer_common(proj, bp, lp, lp, 0, zero_s, zero_c, zero_h, lw)
        xp = _ffn(xp, norm_ffn2[l], g2[l], u2[l], d2[l], mix=(y_ret, y_lru, y_fox, w_out_b[l]))
        outs_p[0].append(kn.reshape(bp, lp, N_HEADS, HEAD_DIM))
        outs_p[1].append(proj[:, COL_FV * W_HEAD:(COL_FV + 1) * W_HEAD].reshape(bp, lp, N_HEADS, HEAD_DIM))
        outs_p[2].append(logf[:, :N_HEADS].reshape(bp, lp, N_HEADS))
        outs_p[3].append(s_new.reshape(bp, N_HEADS, HEAD_DIM, HEAD_DIM))
        outs_p[4].append(h_last.reshape(bp, w_lru))
        outs_p[5].append(conv_new[:, 8 - (CONV_W - 1):, :])

        xs = _ffn(xs, norm_ffn1[l], g1[l], u1[l], d1[l])
        xs_pad = jnp.pad(xs[:, None, :], ((0, 0), (0, sr - 1), (0, 0))).reshape(bs * sr, d)
        proj_s = _inproj(xs_pad, norm_mix[l], w_in_b[l])
        kn_s, logf_s, qn_s = _fox_prep(proj_s, fox_qn[l], fox_kn[l], fox_bf[l], bs, sr, False)
        y_fox_s = _fox_decode(l, page_table, qn_s, kn_s, proj_s, logf_s, cache_k4, cache_v4, cache_lft, bs)
        s0 = state_ret[l].reshape(bs, W_HEAD, HEAD_DIM)
        conv0 = jnp.pad(state_conv[l], ((0, 0), (8 - (CONV_W - 1), 0), (0, 0)))
        h0 = state_lru[l].reshape(bs, 1, w_lru)
        y_ret_s, y_lru_s, s_new_s, h_last_s, conv_new_s = _mixer_common(
            proj_s, bs, sr, 1, past_len, s0, conv0, h0, lw)
        first = lambda a: a.reshape(bs, sr, a.shape[-1])[:, 0, :]
        xs = _ffn(xs, norm_ffn2[l], g2[l], u2[l], d2[l],
                  mix=(first(y_ret_s), first(y_lru_s), y_fox_s.reshape(bs, W_HEAD).astype(BF16), w_out_b[l]))
        outs_s[0].append(first(kn_s).reshape(bs, 1, N_HEADS, HEAD_DIM))
        outs_s[1].append(first(proj_s)[:, COL_FV * W_HEAD:(COL_FV + 1) * W_HEAD].reshape(bs, 1, N_HEADS, HEAD_DIM))
        outs_s[2].append(first(logf_s)[:, :N_HEADS].reshape(bs, 1, N_HEADS))
        outs_s[3].append(s_new_s.reshape(bs, N_HEADS, HEAD_DIM, HEAD_DIM))
        outs_s[4].append(h_last_s.reshape(bs, w_lru))
        outs_s[5].append(conv_new_s[:, 8 - (CONV_W - 1):, :])

    stk = lambda lst: jnp.stack(lst, axis=0)
    return (xp.reshape(bp, lp, d), xs.reshape(bs, 1, d),
            *[stk(o) for o in outs_p], *[stk(o) for o in outs_s])
```

```python
import functools

import numpy as np
import jax
import jax.numpy as jnp
from jax import lax
from jax.experimental import pallas as pl
from jax.experimental.pallas import tpu as pltpu

F32 = jnp.float32
BF16 = jnp.bfloat16

HEAD_DIM = 64
N_HEADS = 4
W_HEAD = N_HEADS * HEAD_DIM
CONV_W = 4
LRU_C = 8.0
RET_CHUNK = 128
ROPE_BASE = 10000.0
EPS = 1e-6
NEG_INF = -1e30
FOX_SCALE = HEAD_DIM ** -0.5
LOG2E = 1.4426950408889634
V_AUG_ROWS = 80
SAMPLE_ROWS = 16
PAGES_PER_STEP = 32
VMEM_LIMIT = 48 * 1024 * 1024

COL_RQ, COL_RK, COL_RV, COL_RG = 0, 1, 2, 3
COL_LX, COL_LG = 2, 3
COL_FQ, COL_FK, COL_FV = 8, 9, 10
COL_FF = 22
PROJ_W = 23 * 128


def _dot(a, b):
    return jnp.dot(a, b, preferred_element_type=F32)


def _dot_nt(a, b):
    return lax.dot_general(a, b, (((1,), (1,)), ((), ())), preferred_element_type=F32)


def _dot_tn(a, b):
    return lax.dot_general(a, b, (((0,), (0,)), ((), ())), preferred_element_type=F32)


def _sigmoid(x):
    return 1.0 / (1.0 + jnp.exp(-x))


def _split2(x):
    hi = x.astype(BF16)
    lo = (x - hi.astype(F32)).astype(BF16)
    return hi, lo


def _split3(x):
    hi = x.astype(BF16)
    r = x - hi.astype(F32)
    mid = r.astype(BF16)
    lo = (r - mid.astype(F32)).astype(BF16)
    return hi, mid, lo


def _params(*sem):
    return pltpu.CompilerParams(dimension_semantics=sem, vmem_limit_bytes=VMEM_LIMIT)


def _ffn_body(has_mix, tf, *refs):
    if has_mix:
        (x_ref, yr_ref, yl_ref, yf_ref, wo_ref, g_ref, wg_ref, wu_ref, wd_ref,
         o_ref, xres_ref, xn_ref, hid_ref) = refs
    else:
        x_ref, g_ref, wg_ref, wu_ref, wd_ref, o_ref, xres_ref, xn_ref, hid_ref = refs
    x = x_ref[...]
    if has_mix:
        w1 = yr_ref.shape[1]
        w2 = w1 + yl_ref.shape[1]
        x = (x + _dot(yr_ref[...], wo_ref[0:w1, :]) + _dot(yl_ref[...], wo_ref[w1:w2, :])
             + _dot(yf_ref[...], wo_ref[w2:, :]))
    xres_ref[...] = x
    ms = jnp.mean(x * x, axis=-1, keepdims=True)
    xn_ref[...] = (x * lax.rsqrt(ms + EPS) * g_ref[...]).astype(BF16)

    xn = xn_ref[...]
    for j in range(wg_ref.shape[1] // tf):
        gate = _dot(xn, wg_ref[:, j * tf:(j + 1) * tf])
        up = _dot(xn, wu_ref[:, j * tf:(j + 1) * tf])
        hid_ref[:, j * tf:(j + 1) * tf] = (gate * _sigmoid(gate) * up).astype(BF16)
    o_ref[...] = xres_ref[...] + 0.5 * _dot(hid_ref[...], wd_ref[...])


def _ffn(x, g, wg, wu, wd, mix=None, tm=512, tf=256):
    m, d = x.shape
    f = wg.shape[1]
    tm = min(tm, m)
    row = lambda i: (i, 0)
    resident = lambda a: pl.BlockSpec(a.shape, lambda i: (0, 0), pipeline_mode=pl.Buffered(1))
    in_specs = [pl.BlockSpec((tm, d), row)]
    args = [x]
    if mix is not None:
        yr, yl, yf, wo = mix
        in_specs += [pl.BlockSpec((tm, yr.shape[1]), row), pl.BlockSpec((tm, yl.shape[1]), row),
                     pl.BlockSpec((tm, yf.shape[1]), row), resident(wo)]
        args += [yr, yl, yf, wo]
    g2 = g.reshape(1, d)
    in_specs += [resident(g2), resident(wg), resident(wu), resident(wd)]
    args += [g2, wg, wu, wd]
    return pl.pallas_call(
        functools.partial(_ffn_body, mix is not None, tf),
        grid=(m // tm,),
        in_specs=in_specs,
        out_specs=pl.BlockSpec((tm, d), row),
        out_shape=jax.ShapeDtypeStruct((m, d), F32),
        scratch_shapes=[pltpu.VMEM((tm, d), F32), pltpu.VMEM((tm, d), BF16), pltpu.VMEM((tm, f), BF16)],
        compiler_params=_params("parallel"),
        name="ffn_mix" if mix is not None else "ffn",
    )(*args)


def _inproj_body(x_ref, g_ref, w_ref, o_ref):
    x = x_ref[...]
    ms = jnp.mean(x * x, axis=-1, keepdims=True)
    xn = (x * lax.rsqrt(ms + EPS) * g_ref[...]).astype(BF16)
    o_ref[...] = _dot(xn, w_ref[...])


def _inproj(x, g, w, tm=512):
    m, d = x.shape
    n = w.shape[1]
    tm = min(tm, m)
    return pl.pallas_call(
        _inproj_body,
        grid=(m // tm,),
        in_specs=[pl.BlockSpec((tm, d), lambda i: (i, 0)),
                  pl.BlockSpec((1, d), lambda i: (0, 0)),
                  pl.BlockSpec((d, n), lambda i: (0, 0))],
        out_specs=pl.BlockSpec((tm, n), lambda i: (i, 0)),
        out_shape=jax.ShapeDtypeStruct((m, n), F32),
        compiler_params=_params("parallel"),
        name="inproj",
    )(x, g.reshape(1, d), w)


def _head_norm(x, g):
    outs = []
    for h in range(N_HEADS):
        xh = x[:, h * HEAD_DIM:(h + 1) * HEAD_DIM]
        ms = jnp.mean(xh * xh, axis=-1, keepdims=True)
        outs.append(xh * lax.rsqrt(ms + EPS) * g)
    return outs


def _pad16(x):
    return jnp.concatenate([x, jnp.zeros_like(x)], axis=0)


def _fox_prep_sample_body(tm, fq_ref, fk_ref, fv_ref, ff_ref, gq_ref, gk_ref, bf_ref,
                          kn_ref, logf_ref, qn_ref):
    del fv_ref
    lane = lax.broadcasted_iota(jnp.int32, (tm, 128), 1)
    z = ff_ref[...] + bf_ref[...]
    logf = jnp.minimum(z, 0.0) - jnp.log1p(jnp.exp(-jnp.abs(z)))
    logf_ref[...] = jnp.where(lane < N_HEADS, logf, 0.0)
    kn_ref[...] = jnp.concatenate(_head_norm(fk_ref[...], gk_ref[...]), axis=-1)
    qn_ref[...] = jnp.concatenate(_head_norm(fq_ref[...], gq_ref[...]), axis=-1)


def _fox_prep_prompt_body(tm, fq_ref, fk_ref, fv_ref, ff_ref, gq_ref, gk_ref, bf_ref,
                          knt_ref, vt_ref, lft_ref, qa_ref, ka_ref, va_ref, carry_ref):
    @pl.when(pl.program_id(1) == 0)
    def _():
        carry_ref[...] = jnp.zeros_like(carry_ref)

    q_t = fq_ref[...].T
    k_t = fk_ref[...].T
    v_t = fv_ref[...].T
    z = ff_ref[...].T[0:8, :] + bf_ref[...]
    lf = jnp.minimum(z, 0.0) - jnp.log1p(jnp.exp(-jnp.abs(z)))
    lft_ref[...] = lf
    vt_ref[...] = v_t

    r_i = lax.broadcasted_iota(jnp.int32, (tm, tm), 0)
    c_i = lax.broadcasted_iota(jnp.int32, (tm, tm), 1)
    upper = jnp.where(r_i <= c_i, 1.0, 0.0).astype(BF16)
    hi, mid, lo = _split3(_pad16(lf))
    cs = (_dot(hi, upper) + _dot(mid, upper) + _dot(lo, upper))[0:8, :] + carry_ref[...]
    carry_ref[...] = cs[:, tm - 1:tm]
    cs = cs * LOG2E

    sub = lax.broadcasted_iota(jnp.int32, (HEAD_DIM, tm), 0)
    v_tail = jnp.where(lax.broadcasted_iota(jnp.int32, (V_AUG_ROWS - HEAD_DIM, tm), 0) == 0, 1.0, 0.0)
    kn_parts = []
    for h in range(N_HEADS):
        rows = slice(h * HEAD_DIM, (h + 1) * HEAD_DIM)
        qh, kh = q_t[rows], k_t[rows]
        qn = qh * lax.rsqrt(jnp.mean(qh * qh, axis=0, keepdims=True) + EPS) * gq_ref[...]
        kn = kh * lax.rsqrt(jnp.mean(kh * kh, axis=0, keepdims=True) + EPS) * gk_ref[...]
        kn_parts.append(kn)
        ch = cs[h:h + 1, :]
        chi = ch.astype(BF16).astype(F32)
        r1 = ch - chi
        cmid = r1.astype(BF16).astype(F32)
        clo = r1 - cmid
        q_tail = jnp.where(sub == 0, chi, jnp.where(sub == 1, cmid, jnp.where(
            sub == 2, clo, jnp.where(sub < 6, 1.0, 0.0))))
        k_tail = jnp.where(sub < 3, 1.0, jnp.where(sub == 3, -chi, jnp.where(
            sub == 4, -cmid, jnp.where(sub == 5, -clo, 0.0))))
        qa_ref[h] = jnp.concatenate([qn * (FOX_SCALE * LOG2E), q_tail], axis=0).astype(BF16)
        ka_ref[h] = jnp.concatenate([kn, k_tail], axis=0).T.astype(BF16)
        va_ref[h] = jnp.concatenate([v_t[rows], v_tail], axis=0).astype(BF16)
    knt_ref[...] = jnp.concatenate(kn_parts, axis=0)


def _fox_prep(proj, gq, gk, bf, nb, l, prompt, tm=256):
    m = proj.shape[0]
    tm = min(tm, l)
    nl = l // tm
    rows = lambda b, i: b * nl + i
    const2 = lambda b, i: (0, 0)
    in_specs = [pl.BlockSpec((tm, W_HEAD), lambda b, i: (rows(b, i), COL_FQ)),
                pl.BlockSpec((tm, W_HEAD), lambda b, i: (rows(b, i), COL_FK)),
                pl.BlockSpec((tm, W_HEAD), lambda b, i: (rows(b, i), COL_FV)),
                pl.BlockSpec((tm, 128), lambda b, i: (rows(b, i), COL_FF))]
    if prompt:
        in_specs += [pl.BlockSpec((HEAD_DIM, 1), const2), pl.BlockSpec((HEAD_DIM, 1), const2),
                     pl.BlockSpec((8, 1), const2)]
        params = (gq.reshape(HEAD_DIM, 1), gk.reshape(HEAD_DIM, 1),
                  jnp.zeros((8, 1), F32).at[:N_HEADS, 0].set(bf))
        pos_minor = lambda r: pl.BlockSpec((None, r, tm), lambda b, i: (b, 0, i))
        aug = pl.BlockSpec((None, N_HEADS, tm, 128), lambda b, i: (b, 0, i, 0))
        aug_t = pl.BlockSpec((None, N_HEADS, 128, tm), lambda b, i: (b, 0, 0, i))
        aug_v = pl.BlockSpec((None, N_HEADS, V_AUG_ROWS, tm), lambda b, i: (b, 0, 0, i))
        out_specs = [pos_minor(W_HEAD), pos_minor(W_HEAD), pos_minor(8), aug_t, aug, aug_v]
        out_shape = [jax.ShapeDtypeStruct((nb, W_HEAD, l), F32), jax.ShapeDtypeStruct((nb, W_HEAD, l), F32),
                     jax.ShapeDtypeStruct((nb, 8, l), F32),
                     jax.ShapeDtypeStruct((nb, N_HEADS, 128, l), BF16),
                     jax.ShapeDtypeStruct((nb, N_HEADS, l, 128), BF16),
                     jax.ShapeDtypeStruct((nb, N_HEADS, V_AUG_ROWS, l), BF16)]
        scratch = [pltpu.VMEM((8, 1), F32)]
        body = functools.partial(_fox_prep_prompt_body, tm)
    else:
        in_specs += [pl.BlockSpec((1, HEAD_DIM), const2), pl.BlockSpec((1, HEAD_DIM), const2),
                     pl.BlockSpec((1, 128), const2)]
        params = (gq.reshape(1, HEAD_DIM), gk.reshape(1, HEAD_DIM),
                  jnp.zeros((1, 128), F32).at[0, :N_HEADS].set(bf))
        row_spec = lambda w: pl.BlockSpec((tm, w), lambda b, i: (rows(b, i), 0))
        out_specs = [row_spec(W_HEAD), row_spec(128), row_spec(W_HEAD)]
        out_shape = [jax.ShapeDtypeStruct((m, W_HEAD), F32), jax.ShapeDtypeStruct((m, 128), F32),
                     jax.ShapeDtypeStruct((m, W_HEAD), F32)]
        scratch = []
        body = functools.partial(_fox_prep_sample_body, tm)
    return pl.pallas_call(
        body,
        grid=(nb, nl),
        in_specs=in_specs,
        out_specs=out_specs,
        out_shape=out_shape,
        scratch_shapes=scratch,
        compiler_params=_params("arbitrary", "arbitrary"),
        name="fox_prep_prompt" if prompt else "fox_prep_sample",
    )(proj, proj, proj, proj, *params)


def _fox_attn_body(tq, qt_ref, ka_ref, vt_ref, o_ref, s_ref, m_ref, acc_ref):
    qi = pl.program_id(2)
    tk = tq // 2
    heads = range(2)
    m_ref[...] = jnp.full_like(m_ref, NEG_INF)
    acc_ref[...] = jnp.zeros_like(acc_ref)

    def scores(slot, j):
        k0 = pl.multiple_of(j * tk, tk)
        for h in heads:
            s_ref[slot, h] = _dot(ka_ref[h, pl.ds(k0, tk), :], qt_ref[h])

    def absorb(h, s, v_t, lanes):
        m_old = m_ref[h, :, lanes]
        m_new = jnp.maximum(m_old, jnp.max(s, axis=0, keepdims=True))
        alpha = jnp.exp2(m_old - m_new)
        p = jnp.exp2(s - m_new)
        acc_ref[h, :, lanes] = alpha * acc_ref[h, :, lanes] + _dot(v_t, p.astype(BF16))
        m_ref[h, :, lanes] = m_new

    def consume(slot, j):
        k0 = pl.multiple_of(j * tk, tk)
        for h in heads:
            absorb(h, s_ref[slot, h], vt_ref[h, :, pl.ds(k0, tk)], slice(None))

    scores(0, 0)

    def pair(jj, carry):
        scores(1, 2 * jj + 1)
        consume(0, 2 * jj)
        scores(0, 2 * jj + 2)
        consume(1, 2 * jj + 1)
        return carry

    lax.fori_loop(0, qi, pair, 0)

    d0 = pl.multiple_of(2 * qi * tk, tk)
    d1 = pl.multiple_of((2 * qi + 1) * tk, tk)
    upper_half = slice(tk, tq)
    s_late = [_dot(ka_ref[h, pl.ds(d1, tk), :], qt_ref[h, :, upper_half]) for h in heads]
    def causal(n):
        return lax.broadcasted_iota(jnp.int32, (tk, n), 0) <= lax.broadcasted_iota(jnp.int32, (tk, n), 1)

    for h in heads:
        absorb(h, jnp.where(causal(tq), s_ref[0, h], NEG_INF), vt_ref[h, :, pl.ds(d0, tk)], slice(None))
    for h in heads:
        absorb(h, jnp.where(causal(tk), s_late[h], NEG_INF), vt_ref[h, :, pl.ds(d1, tk)], upper_half)

    outs = []
    for h in heads:
        acc = acc_ref[h]
        outs.append(acc[0:HEAD_DIM, :] / acc[HEAD_DIM:HEAD_DIM + 1, :])
    o_ref[...] = jnp.concatenate(outs, axis=0).T.astype(BF16)


def _fox_attn(qt, ka, vt, tq=512):
    nb, nh, l, _ = ka.shape
    tq = min(tq, l)
    nq = l // tq
    return pl.pallas_call(
        functools.partial(_fox_attn_body, tq),
        grid=(nb, nh // 2, nq),
        in_specs=[pl.BlockSpec((None, 2, 128, tq), lambda b, hp, i: (b, hp, 0, i)),
                  pl.BlockSpec((None, 2, l, 128), lambda b, hp, i: (b, hp, 0, 0)),
                  pl.BlockSpec((None, 2, V_AUG_ROWS, l), lambda b, hp, i: (b, hp, 0, 0))],
        out_specs=pl.BlockSpec((tq, 128), lambda b, hp, i: (b * nq + i, hp)),
        out_shape=jax.ShapeDtypeStruct((nb * l, W_HEAD), BF16),
        scratch_shapes=[pltpu.VMEM((2, 2, tq // 2, tq), F32), pltpu.VMEM((2, 1, tq), F32),
                        pltpu.VMEM((2, V_AUG_ROWS, tq), F32)],
        compiler_params=_params("parallel", "parallel", "arbitrary"),
        name="fox_attn",
    )(qt, ka, vt)


def _ret_tables(pos0, l_pad, l_real, chunk):
    c_real = chunk if l_real % chunk == 0 else l_real
    half = HEAD_DIM // 2
    inv = ROPE_BASE ** (-np.arange(half, dtype=np.float64) / half)
    inv_l = np.tile(inv, 2 * N_HEADS)
    sgn = np.tile(np.concatenate([-np.ones(half), np.ones(half)]), N_HEADS)
    n_a = l_pad // chunk
    ang_a = (pos0 + chunk * np.arange(n_a))[:, None] * inv_l[None, :]
    ang_b = np.arange(chunk)[:, None] * inv_l[None, :]
    log_g = np.log1p(-np.exp2(-5.0 - np.arange(N_HEADS, dtype=np.float64)))
    idx = np.arange(chunk, dtype=np.float64)
    diff = idx[:, None] - idx[None, :]
    decay = np.where(diff >= 0, np.exp(np.maximum(diff, 0.0)[None] * log_g[:, None, None]), 0.0)
    lg_l = np.repeat(log_g, HEAD_DIM)
    xi = np.exp((idx[:, None] + 1.0) * lg_l[None, :])
    zeta = np.exp((c_real - 1.0 - idx[:, None]) * lg_l[None, :])
    head = np.arange(W_HEAD) // HEAD_DIM
    bd = (head[:, None] == head[None, :]).astype(np.float64)
    gmat = bd * np.exp(c_real * lg_l)[:, None]
    f = lambda a: jnp.asarray(a, dtype=F32)
    return dict(ca=f(np.cos(ang_a)), sa=f(sgn * np.sin(ang_a)), cb=f(np.cos(ang_b)), sb=f(sgn * np.sin(ang_b)),
                decay=f(decay), xi=f(xi), zeta=f(zeta), gmat=f(gmat), bd=f(bd),
                bdavg=jnp.asarray(bd / HEAD_DIM, dtype=BF16))


def _ret_body(tm, ch, q_ref, k_ref, v_ref, g_ref, ca_ref, sa_ref, cb_ref, sb_ref, dec_ref, xi_ref,
              zeta_ref, gmat_ref, bd_ref, bdavg_ref, gn_ref, s0_ref, y_ref, sout_ref, st_ref):
    i = pl.program_id(1)
    n_chunk = tm // ch

    @pl.when(i == 0)
    def _():
        s0 = s0_ref[...]
        st_ref[...] = jnp.concatenate([s0] * N_HEADS, axis=-1) * bd_ref[...]

    lane = lax.broadcasted_iota(jnp.int32, (ch, W_HEAD), 1)
    lane128 = lax.broadcasted_iota(jnp.int32, (ch, 128), 1)
    first_half = (lane128 % HEAD_DIM) < (HEAD_DIM // 2)

    def swap_halves(x):
        parts = []
        for blk in range(W_HEAD // 128):
            xb = x[:, blk * 128:(blk + 1) * 128]
            parts.append(jnp.where(first_half, pltpu.roll(xb, 128 - HEAD_DIM // 2, 1),
                                   pltpu.roll(xb, HEAD_DIM // 2, 1)))
        return jnp.concatenate(parts, axis=-1)

    def gmean(t):
        hi, lo = _split2(t)
        return _dot(hi, bdavg_ref[...]) + _dot(lo, bdavg_ref[...])

    def chunk(c):
        r0 = c * ch
        a = i * n_chunk + c
        c_a = ca_ref[pl.ds(a, 1), :]
        s_a = sa_ref[pl.ds(a, 1), :]
        cos = c_a * cb_ref[...] - s_a * sb_ref[...]
        sin = s_a * cb_ref[...] + c_a * sb_ref[...]
        q = q_ref[pl.ds(r0, ch), :]
        k = k_ref[pl.ds(r0, ch), :]
        q = q * cos + swap_halves(q) * sin
        k = (k * cos + swap_halves(k) * sin) * FOX_SCALE
        qb = q.astype(BF16)
        kb = k.astype(BF16)
        vb = v_ref[pl.ds(r0, ch), :].astype(BF16)
        st = st_ref[...]
        o = _dot(qb, st.astype(BF16)) * xi_ref[...]
        for h in range(N_HEADS):
            mh = (lane // HEAD_DIM) == h
            qm = jnp.where(mh, q, 0.0).astype(BF16)
            att = _dot_nt(qm, kb) * dec_ref[h]
            o = o + jnp.where(mh, _dot(att.astype(BF16), vb), 0.0)
        kz = (k * zeta_ref[...]).astype(BF16)
        st_ref[...] = st * gmat_ref[...] + _dot_tn(kz, vb) * bd_ref[...]
        mu = gmean(o)
        d = o - mu
        var = gmean(d * d)
        on = d * lax.rsqrt(var + EPS) * gn_ref[...]
        gate = g_ref[pl.ds(r0, ch), :]
        y_ref[pl.ds(r0, ch), :] = (on * (gate * _sigmoid(gate))).astype(BF16)

    for c in range(n_chunk):
        chunk(c)

    @pl.when(i == pl.num_programs(1) - 1)
    def _():
        st = st_ref[...]
        acc = st[:, 0:HEAD_DIM]
        for h in range(1, N_HEADS):
            acc = acc + st[:, h * HEAD_DIM:(h + 1) * HEAD_DIM]
        sout_ref[...] = acc


def _retention(proj, gn, s0, nb, l, l_real, pos0, tm=512):
    m = proj.shape[0]
    tm = min(tm, l)
    ch = min(RET_CHUNK, tm)
    nl = l // tm
    t = _ret_tables(pos0, l, l_real, ch)
    rows = lambda b, i: b * nl + i
    const2 = lambda b, i: (0, 0)
    col = lambda cidx: pl.BlockSpec((tm, W_HEAD), lambda b, i: (rows(b, i), cidx))
    full = lambda a: pl.BlockSpec(a.shape, (lambda b, i: (0,) * a.ndim))
    in_specs = [col(COL_RQ), col(COL_RK), col(COL_RV), col(COL_RG),
                full(t["ca"]), full(t["sa"]), full(t["cb"]), full(t["sb"]), full(t["decay"]),
                full(t["xi"]), full(t["zeta"]), full(t["gmat"]), full(t["bd"]), full(t["bdavg"]),
                pl.BlockSpec((1, W_HEAD), const2),
                pl.BlockSpec((None, W_HEAD, HEAD_DIM), lambda b, i: (b, 0, 0))]
    return pl.pallas_call(
        functools.partial(_ret_body, tm, ch),
        grid=(nb, nl),
        in_specs=in_specs,
        out_specs=[pl.BlockSpec((tm, W_HEAD), lambda b, i: (rows(b, i), 0)),
                   pl.BlockSpec((None, W_HEAD, HEAD_DIM), lambda b, i: (b, 0, 0))],
        out_shape=[jax.ShapeDtypeStruct((m, W_HEAD), BF16),
                   jax.ShapeDtypeStruct((nb, W_HEAD, HEAD_DIM), F32)],
        scratch_shapes=[pltpu.VMEM((W_HEAD, W_HEAD), F32)],
        compiler_params=_params("arbitrary", "arbitrary"),
        name="retention",
    )(proj, proj, proj, proj, t["ca"], t["sa"], t["cb"], t["sb"], t["decay"], t["xi"], t["zeta"],
      t["gmat"], t["bd"], t["bdavg"], gn.reshape(1, W_HEAD), s0)


def _lru_body(tm, l_real, lx_ref, lg_ref, cw_ref, cb_ref, wr_ref, wi_ref, br_ref, bi_ref, lam_ref,
              conv0_ref, h0_ref, y_ref, hlast_ref, convnew_ref, xpad_ref, a_ref, b_ref, h_ref, hcar_ref):
    i = pl.program_id(1)
    w = lx_ref.shape[1]
    t_last, r_last = (l_real - 1) // tm, (l_real - 1) % tm

    @pl.when(i == 0)
    def _():
        xpad_ref[0:8, :] = conv0_ref[...]
        hcar_ref[...] = h0_ref[...]

    x = lx_ref[...]
    xpad_ref[8:8 + tm, :] = x
    xc = cb_ref[...] + cw_ref[CONV_W - 1:CONV_W, :] * x
    for j in range(1, CONV_W):
        xc = xc + cw_ref[CONV_W - 1 - j:CONV_W - j, :] * xpad_ref[pl.ds(8 - j, tm), :]

    xb = xc.astype(BF16)
    r = _sigmoid(_dot(xb, wr_ref[...]) + br_ref[...])
    ig = _sigmoid(_dot(xb, wi_ref[...]) + bi_ref[...])
    z = -lam_ref[...]
    softplus = jnp.maximum(z, 0.0) + jnp.log1p(jnp.exp(-jnp.abs(z)))
    log_a = (-LRU_C) * r * softplus
    a = jnp.exp(log_a)
    th = jnp.tanh(log_a)
    u = jnp.sqrt(-2.0 * th / (1.0 - th)) * (ig * xc)

    rowmod = lax.broadcasted_iota(jnp.int32, (tm, w), 0) % 8
    for sh in (1, 2, 4):
        a_sh = pltpu.roll(a, sh, 0)
        u_sh = pltpu.roll(u, sh, 0)
        valid = rowmod >= sh
        u = jnp.where(valid, a * u_sh + u, u)
        a = jnp.where(valid, a * a_sh, a)
    a_ref[...] = a
    b_ref[...] = u

    def group(gi, h):
        r0 = pl.multiple_of(gi * 8, 8)
        hg = a_ref[pl.ds(r0, 8), :] * h + b_ref[pl.ds(r0, 8), :]
        h_ref[pl.ds(r0, 8), :] = hg
        return hg[7:8, :]

    hcar_ref[...] = lax.fori_loop(0, tm // 8, group, hcar_ref[...])

    g = lg_ref[...]
    gelu = 0.5 * g * (1.0 + jnp.tanh(0.7978845608028654 * (g + 0.044715 * (g * g * g))))
    y_ref[...] = (h_ref[...] * gelu).astype(BF16)

    @pl.when(i == t_last)
    def _():
        hlast_ref[...] = h_ref[r_last:r_last + 1, :]
        convnew_ref[...] = xpad_ref[pl.ds(r_last + 1, 8), :]

    xpad_ref[0:8, :] = xpad_ref[tm:tm + 8, :]


def _lru(proj, cw, cb, wr, wi, br, bi, lam, conv0, h0, nb, l, l_real, tm=512):
    m = proj.shape[0]
    w = cw.shape[1]
    tm = min(tm, l)
    nl = l // tm
    rows = lambda b, i: b * nl + i
    const2 = lambda b, i: (0, 0)
    vec = pl.BlockSpec((1, w), const2)
    per_b = lambda r: pl.BlockSpec((None, r, w), lambda b, i: (b, 0, 0))
    return pl.pallas_call(
        functools.partial(_lru_body, tm, l_real),
        grid=(nb, nl),
        in_specs=[pl.BlockSpec((tm, w), lambda b, i: (rows(b, i), COL_LX)),
                  pl.BlockSpec((tm, w), lambda b, i: (rows(b, i), COL_LG)),
                  pl.BlockSpec((CONV_W, w), const2), vec,
                  pl.BlockSpec((w, w), const2), pl.BlockSpec((w, w), const2), vec, vec, vec,
                  per_b(8), per_b(1)],
        out_specs=[pl.BlockSpec((tm, w), lambda b, i: (rows(b, i), 0)), per_b(1), per_b(8)],
        out_shape=[jax.ShapeDtypeStruct((m, w), BF16), jax.ShapeDtypeStruct((nb, 1, w), F32),
                   jax.ShapeDtypeStruct((nb, 8, w), F32)],
        scratch_shapes=[pltpu.VMEM((tm + 8, w), F32), pltpu.VMEM((tm, w), F32), pltpu.VMEM((tm, w), F32),
                        pltpu.VMEM((tm, w), F32), pltpu.VMEM((1, w), F32)],
        compiler_params=_params("arbitrary", "arbitrary"),
        name="lru",
    )(proj, proj, cw, cb.reshape(1, w), wr, wi, br.reshape(1, w), bi.reshape(1, w), lam.reshape(1, w),
      conv0, h0)


def _decode_body(n_pg, pt_ref, qn_ref, knew_ref, vnew_ref, lfnew_ref, *rest):
    del pt_ref
    k_refs, v_refs, lf_refs = rest[:n_pg], rest[n_pg:2 * n_pg], rest[2 * n_pg:3 * n_pg]
    o_ref, m_ref, l_ref, acc_ref, car_ref, qm_ref = rest[3 * n_pg:]
    p_i = pl.program_id(1)
    page = k_refs[0].shape[1]
    row = lax.broadcasted_iota(jnp.int32, (8, W_HEAD), 0)
    lane = lax.broadcasted_iota(jnp.int32, (8, W_HEAD), 1)
    own = (lane // HEAD_DIM) == row

    @pl.when(p_i == 0)
    def _():
        q = jnp.where(own, jnp.broadcast_to(qn_ref[0:1, :] * FOX_SCALE, (8, W_HEAD)), 0.0)
        qm_ref[...] = _pad16(q).astype(BF16)
        m_ref[...] = jnp.full_like(m_ref, NEG_INF)
        l_ref[...] = jnp.zeros_like(l_ref)
        acc_ref[...] = jnp.zeros_like(acc_ref)
        car_ref[...] = jnp.zeros_like(car_ref)

    qm = qm_ref[...]

    def update(s, pv_fn):
        m_old = m_ref[...]
        m_new = jnp.maximum(m_old, jnp.max(s, axis=-1, keepdims=True))
        alpha = jnp.exp(m_old - m_new)
        p = jnp.exp(s - m_new)
        l_ref[...] = alpha * l_ref[...] + jnp.sum(p, axis=-1, keepdims=True)
        acc_ref[...] = alpha * acc_ref[...] + pv_fn(p)
        m_ref[...] = m_new

    r_i = lax.broadcasted_iota(jnp.int32, (page, page), 0)
    c_i = lax.broadcasted_iota(jnp.int32, (page, page), 1)
    upper = jnp.where(r_i <= c_i, 1.0, 0.0).astype(BF16)
    hi, mid, lo = _split3(jnp.concatenate([lf_refs[g][...] for g in range(n_pg)], axis=0))
    c_all = _dot(hi, upper) + _dot(mid, upper) + _dot(lo, upper)
    nr = n_pg * 8
    rr = lax.broadcasted_iota(jnp.int32, (nr, nr), 0)
    cc = lax.broadcasted_iota(jnp.int32, (nr, nr), 1)
    before = jnp.where((rr % 8 == cc % 8) & (cc // 8 < rr // 8), 1.0, 0.0).astype(BF16)
    chi, cmid, clo = _split3(c_all)
    pref = _dot(before, chi) + _dot(before, cmid) + _dot(before, clo)
    c_all = c_all + pref[:, page - 1:page] + jnp.concatenate([car_ref[...]] * n_pg, axis=0)
    car_ref[...] = c_all[nr - 8:nr, page - 1:page]
    s_parts = [_dot(qm, k_refs[g][...].astype(BF16))[0:8, :] - c_all[g * 8:(g + 1) * 8, :]
               for g in range(n_pg)]

    def pv_pages(p):
        pb = _pad16(p).astype(BF16)
        pv = _dot_nt(pb[:, 0:page], v_refs[0][...].astype(BF16))
        for g in range(1, n_pg):
            pv = pv + _dot_nt(pb[:, g * page:(g + 1) * page], v_refs[g][...].astype(BF16))
        return pv[0:8, :]

    update(jnp.concatenate(s_parts, axis=-1), pv_pages)

    @pl.when(p_i == pl.num_programs(1) - 1)
    def _():
        lane128 = lax.broadcasted_iota(jnp.int32, (8, 128), 1)
        row128 = lax.broadcasted_iota(jnp.int32, (8, 128), 0)
        lf_col = jnp.sum(jnp.where(lane128 == row128, jnp.broadcast_to(lfnew_ref[0:1, :], (8, 128)), 0.0),
                         axis=-1, keepdims=True)
        k_new = knew_ref[0:1, :].astype(BF16).astype(F32)
        v_new = vnew_ref[0:1, :].astype(BF16).astype(F32)
        s_new = jnp.sum(qm[0:8, :].astype(F32) * k_new, axis=-1, keepdims=True) - (car_ref[...] + lf_col)
        update(s_new, lambda p: p.astype(BF16).astype(F32) * v_new)
        o_ref[...] = jnp.sum(jnp.where(own, acc_ref[...] / l_ref[...], 0.0), axis=0, keepdims=True)


def _fox_decode(layer, page_table, qn, kn, proj, logf, cache_kt, cache_vt, cache_lft, nb):
    n_pages = page_table.shape[1]
    n_pg = min(PAGES_PER_STEP, n_pages)
    sr = SAMPLE_ROWS
    tok = lambda cidx: (lambda b, p, pt: (b, cidx))
    in_specs = [pl.BlockSpec((sr, W_HEAD), tok(0)), pl.BlockSpec((sr, W_HEAD), tok(0)),
                pl.BlockSpec((sr, W_HEAD), tok(COL_FV)), pl.BlockSpec((sr, 128), tok(0))]
    args = [qn, kn, proj, logf]
    for arr in (cache_kt, cache_vt, cache_lft):
        for g in range(n_pg):
            in_specs.append(pl.BlockSpec(
                (None, None) + arr.shape[2:],
                lambda b, p, pt, g=g: (layer, pt[b, p * n_pg + g], 0, 0)))
            args.append(arr)
    grid_spec = pltpu.PrefetchScalarGridSpec(
        num_scalar_prefetch=1,
        grid=(nb, n_pages // n_pg),
        in_specs=in_specs,
        out_specs=pl.BlockSpec((None, 1, W_HEAD), lambda b, p, pt: (b, 0, 0)),
        scratch_shapes=[pltpu.VMEM((8, 1), F32), pltpu.VMEM((8, 1), F32), pltpu.VMEM((8, W_HEAD), F32),
                        pltpu.VMEM((8, 1), F32), pltpu.VMEM((16, W_HEAD), BF16)])
    return pl.pallas_call(
        functools.partial(_decode_body, n_pg),
        grid_spec=grid_spec,
        out_shape=jax.ShapeDtypeStruct((nb, 1, W_HEAD), F32),
        compiler_params=_params("arbitrary", "arbitrary"),
        name="fox_decode",
    )(page_table, *args)


def _block_diag(w):
    n, d, e = w.shape
    eye = jnp.eye(n, dtype=w.dtype)
    return (eye[:, None, :, None] * w[:, :, None, :]).reshape(n * d, n * e)


def _mixer_common(proj, nb, l, l_real, pos0, s0, conv0, h0, lw):
    y_ret, s_new = _retention(proj, lw["ret_gn"], s0, nb, l, l_real, pos0)
    y_lru, h_last, conv_new = _lru(proj, lw["conv_w"], lw["conv_b"], lw["wr"], lw["wi"], lw["br"], lw["bi"],
                                   lw["lam"], conv0, h0, nb, l, l_real)
    return y_ret, y_lru, s_new, h_last, conv_new


def kernel(x_prompt, x_sample, cache_k, cache_v, cache_logf, state_ret, state_lru, state_conv, page_table, norm_ffn1, ffn1_gate, ffn1_up, ffn1_down, norm_mix, w_in, ret_gn, conv_w, conv_b, lru_wr, lru_br, lru_wi, lru_bi, lru_lambda, fox_qn, fox_kn, fox_bf, w_out, norm_ffn2, ffn2_gate, ffn2_up, ffn2_down):
    bp, lp, d = x_prompt.shape
    bs, ls, _ = x_sample.shape
    assert ls == 1
    depth = w_in.shape[0]
    n_pool, page = cache_k.shape[1], cache_k.shape[2]
    past_len = page_table.shape[1] * page
    w_lru = conv_w.shape[2]
    sr = SAMPLE_ROWS
    in_width = w_in.shape[2]

    w_in_b = jnp.pad(w_in, ((0, 0), (0, 0), (0, PROJ_W - in_width))).astype(BF16)
    g1, u1, d1 = ffn1_gate.astype(BF16), ffn1_up.astype(BF16), ffn1_down.astype(BF16)
    g2, u2, d2 = ffn2_gate.astype(BF16), ffn2_up.astype(BF16), ffn2_down.astype(BF16)
    w_out_b = w_out.astype(BF16)
    cache_k4 = jnp.transpose(cache_k, (0, 1, 3, 4, 2)).reshape(depth, n_pool, W_HEAD, page)
    cache_v4 = jnp.transpose(cache_v, (0, 1, 3, 4, 2)).reshape(depth, n_pool, W_HEAD, page)
    cache_lft = jnp.pad(jnp.swapaxes(cache_logf, 2, 3), ((0, 0), (0, 0), (0, 8 - N_HEADS), (0, 0)))

    xp = x_prompt.reshape(bp * lp, d)
    xs = x_sample.reshape(bs, d)
    zero_s = jnp.zeros((bp, W_HEAD, HEAD_DIM), F32)
    zero_c = jnp.zeros((bp, 8, w_lru), F32)
    zero_h = jnp.zeros((bp, 1, w_lru), F32)

    outs_p = [[] for _ in range(6)]
    outs_s = [[] for _ in range(6)]
    for l in range(depth):
        lw = dict(ret_gn=ret_gn[l], conv_w=conv_w[l], conv_b=conv_b[l],
                  wr=_block_diag(lru_wr[l]).astype(BF16), wi=_block_diag(lru_wi[l]).astype(BF16),
                  br=lru_br[l], bi=lru_bi[l], lam=lru_lambda[l])

        xp = _ffn(xp, norm_ffn1[l], g1[l], u1[l], d1[l])
        proj = _inproj(xp, norm_mix[l], w_in_b[l])
        kn_t, v_t, logf_t, qa, ka, va = _fox_prep(proj, fox_qn[l], fox_kn[l], fox_bf[l], bp, lp, True)
        y_fox = _fox_attn(qa, ka, va)
        y_ret, y_lru, s_new, h_last, conv_new = _mixer_common(proj, bp, lp, lp, 0, zero_s, zero_c, zero_h, lw)
        xp = _ffn(xp, norm_ffn2[l], g2[l], u2[l], d2[l], mix=(y_ret, y_lru, y_fox, w_out_b[l]))
        heads_last = lambda a: jnp.transpose(a.reshape(bp, N_HEADS, HEAD_DIM, lp), (0, 3, 1, 2))
        outs_p[0].append(heads_last(kn_t))
        outs_p[1].append(heads_last(v_t))
        outs_p[2].append(jnp.transpose(logf_t[:, :N_HEADS, :], (0, 2, 1)))
        outs_p[3].append(s_new.reshape(bp, N_HEADS, HEAD_DIM, HEAD_DIM))
        outs_p[4].append(h_last.reshape(bp, w_lru))
        outs_p[5].append(conv_new[:, 8 - (CONV_W - 1):, :])

        xs = _ffn(xs, norm_ffn1[l], g1[l], u1[l], d1[l])
        xs_pad = jnp.pad(xs[:, None, :], ((0, 0), (0, sr - 1), (0, 0))).reshape(bs * sr, d)
        proj_s = _inproj(xs_pad, norm_mix[l], w_in_b[l])
        kn_s, logf_s, qn_s = _fox_prep(proj_s, fox_qn[l], fox_kn[l], fox_bf[l], bs, sr, False)
        y_fox_s = _fox_decode(l, page_table, qn_s, kn_s, proj_s, logf_s, cache_k4, cache_v4, cache_lft, bs)
        s0 = state_ret[l].reshape(bs, W_HEAD, HEAD_DIM)
        conv0 = jnp.pad(state_conv[l], ((0, 0), (8 - (CONV_W - 1), 0), (0, 0)))
        h0 = state_lru[l].reshape(bs, 1, w_lru)
        y_ret_s, y_lru_s, s_new_s, h_last_s, conv_new_s = _mixer_common(
            proj_s, bs, sr, 1, past_len, s0, conv0, h0, lw)
        first = lambda a: a.reshape(bs, sr, a.shape[-1])[:, 0, :]
        xs = _ffn(xs, norm_ffn2[l], g2[l], u2[l], d2[l],
                  mix=(first(y_ret_s), first(y_lru_s), y_fox_s.reshape(bs, W_HEAD).astype(BF16), w_out_b[l]))
        outs_s[0].append(first(kn_s).reshape(bs, 1, N_HEADS, HEAD_DIM))
        outs_s[1].append(first(proj_s)[:, COL_FV * W_HEAD:(COL_FV + 1) * W_HEAD].reshape(bs, 1, N_HEADS, HEAD_DIM))
        outs_s[2].append(first(logf_s)[:, :N_HEADS].reshape(bs, 1, N_HEADS))
        outs_s[3].append(s_new_s.reshape(bs, N_HEADS, HEAD_DIM, HEAD_DIM))
        outs_s[4].append(h_last_s.reshape(bs, w_lru))
        outs_s[5].append(conv_new_s[:, 8 - (CONV_W - 1):, :])

    stk = lambda lst: jnp.stack(lst, axis=0)
    return (xp.reshape(bp, lp, d), xs.reshape(bs, 1, d),
            *[stk(o) for o in outs_p], *[stk(o) for o in outs_s])
```

```python
import functools

import numpy as np
import jax
import jax.numpy as jnp
from jax import lax
from jax.experimental import pallas as pl
from jax.experimental.pallas import tpu as pltpu

F32 = jnp.float32
BF16 = jnp.bfloat16

HEAD_DIM = 64
N_HEADS = 4
W_HEAD = N_HEADS * HEAD_DIM
CONV_W = 4
LRU_C = 8.0
RET_CHUNK = 128
ROPE_BASE = 10000.0
EPS = 1e-6
NEG_INF = -1e30
FOX_SCALE = HEAD_DIM ** -0.5
LOG2E = 1.4426950408889634
V_AUG_ROWS = 80
SAMPLE_ROWS = 16
PAGES_PER_STEP = 32
VMEM_LIMIT = 48 * 1024 * 1024

COL_RQ, COL_RK, COL_RV, COL_RG = 0, 1, 2, 3
COL_LX, COL_LG = 2, 3
COL_FQ, COL_FK, COL_FV = 8, 9, 10
COL_FF = 22
PROJ_W = 23 * 128


def _dot(a, b):
    return jnp.dot(a, b, preferred_element_type=F32)


def _dot_nt(a, b):
    return lax.dot_general(a, b, (((1,), (1,)), ((), ())), preferred_element_type=F32)


def _dot_tn(a, b):
    return lax.dot_general(a, b, (((0,), (0,)), ((), ())), preferred_element_type=F32)


def _sigmoid(x):
    return 1.0 / (1.0 + jnp.exp(-x))


def _split2(x):
    hi = x.astype(BF16)
    lo = (x - hi.astype(F32)).astype(BF16)
    return hi, lo


def _split3(x):
    hi = x.astype(BF16)
    r = x - hi.astype(F32)
    mid = r.astype(BF16)
    lo = (r - mid.astype(F32)).astype(BF16)
    return hi, mid, lo


def _params(*sem):
    return pltpu.CompilerParams(dimension_semantics=sem, vmem_limit_bytes=VMEM_LIMIT)


def _ffn_body(has_mix, tf, *refs):
    if has_mix:
        (x_ref, yr_ref, yl_ref, yf_ref, wo_ref, g_ref, wg_ref, wu_ref, wd_ref,
         o_ref, xres_ref, xn_ref, hid_ref) = refs
    else:
        x_ref, g_ref, wg_ref, wu_ref, wd_ref, o_ref, xres_ref, xn_ref, hid_ref = refs
    x = x_ref[...]
    if has_mix:
        w1 = yr_ref.shape[1]
        w2 = w1 + yl_ref.shape[1]
        x = (x + _dot(yr_ref[...], wo_ref[0:w1, :]) + _dot(yl_ref[...], wo_ref[w1:w2, :])
             + _dot(yf_ref[...], wo_ref[w2:, :]))
    xres_ref[...] = x
    ms = jnp.mean(x * x, axis=-1, keepdims=True)
    xn_ref[...] = (x * lax.rsqrt(ms + EPS) * g_ref[...]).astype(BF16)

    xn = xn_ref[...]
    for j in range(wg_ref.shape[1] // tf):
        gate = _dot(xn, wg_ref[:, j * tf:(j + 1) * tf])
        up = _dot(xn, wu_ref[:, j * tf:(j + 1) * tf])
        hid_ref[:, j * tf:(j + 1) * tf] = (gate * _sigmoid(gate) * up).astype(BF16)
    o_ref[...] = xres_ref[...] + 0.5 * _dot(hid_ref[...], wd_ref[...])


def _ffn(x, g, wg, wu, wd, mix=None, tm=512, tf=256):
    m, d = x.shape
    f = wg.shape[1]
    tm = min(tm, m)
    row = lambda i: (i, 0)
    resident = lambda a: pl.BlockSpec(a.shape, lambda i: (0, 0), pipeline_mode=pl.Buffered(1))
    in_specs = [pl.BlockSpec((tm, d), row)]
    args = [x]
    if mix is not None:
        yr, yl, yf, wo = mix
        in_specs += [pl.BlockSpec((tm, yr.shape[1]), row), pl.BlockSpec((tm, yl.shape[1]), row),
                     pl.BlockSpec((tm, yf.shape[1]), row), resident(wo)]
        args += [yr, yl, yf, wo]
    g2 = g.reshape(1, d)
    in_specs += [resident(g2), resident(wg), resident(wu), resident(wd)]
    args += [g2, wg, wu, wd]
    return pl.pallas_call(
        functools.partial(_ffn_body, mix is not None, tf),
        grid=(m // tm,),
        in_specs=in_specs,
        out_specs=pl.BlockSpec((tm, d), row),
        out_shape=jax.ShapeDtypeStruct((m, d), F32),
        scratch_shapes=[pltpu.VMEM((tm, d), F32), pltpu.VMEM((tm, d), BF16), pltpu.VMEM((tm, f), BF16)],
        compiler_params=_params("parallel"),
        name="ffn_mix" if mix is not None else "ffn",
    )(*args)


def _inproj_body(x_ref, g_ref, w_ref, o_ref):
    x = x_ref[...]
    ms = jnp.mean(x * x, axis=-1, keepdims=True)
    xn = (x * lax.rsqrt(ms + EPS) * g_ref[...]).astype(BF16)
    o_ref[...] = _dot(xn, w_ref[...])


def _inproj(x, g, w, tm=512):
    m, d = x.shape
    n = w.shape[1]
    tm = min(tm, m)
    return pl.pallas_call(
        _inproj_body,
        grid=(m // tm,),
        in_specs=[pl.BlockSpec((tm, d), lambda i: (i, 0)),
                  pl.BlockSpec((1, d), lambda i: (0, 0)),
                  pl.BlockSpec((d, n), lambda i: (0, 0))],
        out_specs=pl.BlockSpec((tm, n), lambda i: (i, 0)),
        out_shape=jax.ShapeDtypeStruct((m, n), F32),
        compiler_params=_params("parallel"),
        name="inproj",
    )(x, g.reshape(1, d), w)


def _head_norm(x, g):
    outs = []
    for h in range(N_HEADS):
        xh = x[:, h * HEAD_DIM:(h + 1) * HEAD_DIM]
        ms = jnp.mean(xh * xh, axis=-1, keepdims=True)
        outs.append(xh * lax.rsqrt(ms + EPS) * g)
    return outs


def _pad16(x):
    return jnp.concatenate([x, jnp.zeros_like(x)], axis=0)


def _fox_prep_sample_body(tm, fq_ref, fk_ref, fv_ref, ff_ref, gq_ref, gk_ref, bf_ref,
                          kn_ref, logf_ref, qn_ref):
    del fv_ref
    lane = lax.broadcasted_iota(jnp.int32, (tm, 128), 1)
    z = ff_ref[...] + bf_ref[...]
    logf = jnp.minimum(z, 0.0) - jnp.log1p(jnp.exp(-jnp.abs(z)))
    logf_ref[...] = jnp.where(lane < N_HEADS, logf, 0.0)
    kn_ref[...] = jnp.concatenate(_head_norm(fk_ref[...], gk_ref[...]), axis=-1)
    qn_ref[...] = jnp.concatenate(_head_norm(fq_ref[...], gq_ref[...]), axis=-1)


def _fox_prep_prompt_body(tm, fq_ref, fk_ref, fv_ref, ff_ref, gq_ref, gk_ref, bf_ref,
                          knt_ref, vt_ref, lft_ref, qa_ref, ka_ref, va_ref, carry_ref):
    @pl.when(pl.program_id(1) == 0)
    def _():
        carry_ref[...] = jnp.zeros_like(carry_ref)

    q_t = fq_ref[...].T
    k_t = fk_ref[...].T
    v_t = fv_ref[...].T
    z = ff_ref[...].T[0:8, :] + bf_ref[...]
    lf = jnp.minimum(z, 0.0) - jnp.log1p(jnp.exp(-jnp.abs(z)))
    lft_ref[...] = lf
    vt_ref[...] = v_t

    r_i = lax.broadcasted_iota(jnp.int32, (tm, tm), 0)
    c_i = lax.broadcasted_iota(jnp.int32, (tm, tm), 1)
    upper = jnp.where(r_i <= c_i, 1.0, 0.0).astype(BF16)
    hi, mid, lo = _split3(_pad16(lf))
    cs = (_dot(hi, upper) + _dot(mid, upper) + _dot(lo, upper))[0:8, :] + carry_ref[...]
    carry_ref[...] = cs[:, tm - 1:tm]
    cs = cs * LOG2E

    sub = lax.broadcasted_iota(jnp.int32, (HEAD_DIM, tm), 0)
    v_tail = jnp.where(lax.broadcasted_iota(jnp.int32, (V_AUG_ROWS - HEAD_DIM, tm), 0) == 0, 1.0, 0.0)
    kn_parts = []
    for h in range(N_HEADS):
        rows = slice(h * HEAD_DIM, (h + 1) * HEAD_DIM)
        qh, kh = q_t[rows], k_t[rows]
        qn = qh * lax.rsqrt(jnp.mean(qh * qh, axis=0, keepdims=True) + EPS) * gq_ref[...]
        kn = kh * lax.rsqrt(jnp.mean(kh * kh, axis=0, keepdims=True) + EPS) * gk_ref[...]
        kn_parts.append(kn)
        ch = cs[h:h + 1, :]
        chi = ch.astype(BF16).astype(F32)
        r1 = ch - chi
        cmid = r1.astype(BF16).astype(F32)
        clo = r1 - cmid
        q_tail = jnp.where(sub == 0, chi, jnp.where(sub == 1, cmid, jnp.where(
            sub == 2, clo, jnp.where(sub < 6, 1.0, 0.0))))
        k_tail = jnp.where(sub < 3, 1.0, jnp.where(sub == 3, -chi, jnp.where(
            sub == 4, -cmid, jnp.where(sub == 5, -clo, 0.0))))
        qa_ref[h] = jnp.concatenate([qn * (FOX_SCALE * LOG2E), q_tail], axis=0).astype(BF16)
        ka_ref[h] = jnp.concatenate([kn, k_tail], axis=0).T.astype(BF16)
        va_ref[h] = jnp.concatenate([v_t[rows], v_tail], axis=0).astype(BF16)
    knt_ref[...] = jnp.concatenate(kn_parts, axis=0)


def _fox_prep(proj, gq, gk, bf, nb, l, prompt, tm=256):
    m = proj.shape[0]
    tm = min(tm, l)
    nl = l // tm
    rows = lambda b, i: b * nl + i
    const2 = lambda b, i: (0, 0)
    in_specs = [pl.BlockSpec((tm, W_HEAD), lambda b, i: (rows(b, i), COL_FQ)),
                pl.BlockSpec((tm, W_HEAD), lambda b, i: (rows(b, i), COL_FK)),
                pl.BlockSpec((tm, W_HEAD), lambda b, i: (rows(b, i), COL_FV)),
                pl.BlockSpec((tm, 128), lambda b, i: (rows(b, i), COL_FF))]
    if prompt:
        in_specs += [pl.BlockSpec((HEAD_DIM, 1), const2), pl.BlockSpec((HEAD_DIM, 1), const2),
                     pl.BlockSpec((8, 1), const2)]
        params = (gq.reshape(HEAD_DIM, 1), gk.reshape(HEAD_DIM, 1),
                  jnp.zeros((8, 1), F32).at[:N_HEADS, 0].set(bf))
        pos_minor = lambda r: pl.BlockSpec((None, r, tm), lambda b, i: (b, 0, i))
        aug = pl.BlockSpec((None, N_HEADS, tm, 128), lambda b, i: (b, 0, i, 0))
        aug_t = pl.BlockSpec((None, N_HEADS, 128, tm), lambda b, i: (b, 0, 0, i))
        aug_v = pl.BlockSpec((None, N_HEADS, V_AUG_ROWS, tm), lambda b, i: (b, 0, 0, i))
        out_specs = [pos_minor(W_HEAD), pos_minor(W_HEAD), pos_minor(8), aug_t, aug, aug_v]
        out_shape = [jax.ShapeDtypeStruct((nb, W_HEAD, l), F32), jax.ShapeDtypeStruct((nb, W_HEAD, l), F32),
                     jax.ShapeDtypeStruct((nb, 8, l), F32),
                     jax.ShapeDtypeStruct((nb, N_HEADS, 128, l), BF16),
                     jax.ShapeDtypeStruct((nb, N_HEADS, l, 128), BF16),
                     jax.ShapeDtypeStruct((nb, N_HEADS, V_AUG_ROWS, l), BF16)]
        scratch = [pltpu.VMEM((8, 1), F32)]
        body = functools.partial(_fox_prep_prompt_body, tm)
    else:
        in_specs += [pl.BlockSpec((1, HEAD_DIM), const2), pl.BlockSpec((1, HEAD_DIM), const2),
                     pl.BlockSpec((1, 128), const2)]
        params = (gq.reshape(1, HEAD_DIM), gk.reshape(1, HEAD_DIM),
                  jnp.zeros((1, 128), F32).at[0, :N_HEADS].set(bf))
        row_spec = lambda w: pl.BlockSpec((tm, w), lambda b, i: (rows(b, i), 0))
        out_specs = [row_spec(W_HEAD), row_spec(128), row_spec(W_HEAD)]
        out_shape = [jax.ShapeDtypeStruct((m, W_HEAD), F32), jax.ShapeDtypeStruct((m, 128), F32),
                     jax.ShapeDtypeStruct((m, W_HEAD), F32)]
        scratch = []
        body = functools.partial(_fox_prep_sample_body, tm)
    return pl.pallas_call(
        body,
        grid=(nb, nl),
        in_specs=in_specs,
        out_specs=out_specs,
        out_shape=out_shape,
        scratch_shapes=scratch,
        compiler_params=_params("arbitrary", "arbitrary"),
        name="fox_prep_prompt" if prompt else "fox_prep_sample",
    )(proj, proj, proj, proj, *params)


def _fox_attn_body(tq, qt_ref, ka_ref, vt_ref, o_ref, s_ref, m_ref, acc_ref):
    qi = pl.program_id(2)
    tk = tq // 2
    heads = range(2)
    m_ref[...] = jnp.full_like(m_ref, NEG_INF)
    acc_ref[...] = jnp.zeros_like(acc_ref)

    def scores(slot, j):
        k0 = pl.multiple_of(j * tk, tk)
        for h in heads:
            s_ref[slot, h] = _dot(ka_ref[h, pl.ds(k0, tk), :], qt_ref[h])

    def absorb(h, s, v_t, lanes):
        m_old = m_ref[h, :, lanes]
        m_new = jnp.maximum(m_old, jnp.max(s, axis=0, keepdims=True))
        alpha = jnp.exp2(m_old - m_new)
        p = jnp.exp2(s - m_new)
        acc_ref[h, :, lanes] = alpha * acc_ref[h, :, lanes] + _dot(v_t, p.astype(BF16))
        m_ref[h, :, lanes] = m_new

    def consume(slot, j):
        k0 = pl.multiple_of(j * tk, tk)
        for h in heads:
            absorb(h, s_ref[slot, h], vt_ref[h, :, pl.ds(k0, tk)], slice(None))

    scores(0, 0)

    def pair(jj, carry):
        scores(1, 2 * jj + 1)
        consume(0, 2 * jj)
        scores(0, 2 * jj + 2)
        consume(1, 2 * jj + 1)
        return carry

    lax.fori_loop(0, qi, pair, 0)

    d0 = pl.multiple_of(2 * qi * tk, tk)
    d1 = pl.multiple_of((2 * qi + 1) * tk, tk)
    upper_half = slice(tk, tq)
    s_late = [_dot(ka_ref[h, pl.ds(d1, tk), :], qt_ref[h, :, upper_half]) for h in heads]
    def causal(n):
        return lax.broadcasted_iota(jnp.int32, (tk, n), 0) <= lax.broadcasted_iota(jnp.int32, (tk, n), 1)

    for h in heads:
        absorb(h, jnp.where(causal(tq), s_ref[0, h], NEG_INF), vt_ref[h, :, pl.ds(d0, tk)], slice(None))
    for h in heads:
        absorb(h, jnp.where(causal(tk), s_late[h], NEG_INF), vt_ref[h, :, pl.ds(d1, tk)], upper_half)

    outs = []
    for h in heads:
        acc = acc_ref[h]
        outs.append(acc[0:HEAD_DIM, :] / acc[HEAD_DIM:HEAD_DIM + 1, :])
    o_ref[...] = jnp.concatenate(outs, axis=0).T.astype(BF16)


def _fox_attn(qt, ka, vt, tq=512):
    nb, nh, l, _ = ka.shape
    tq = min(tq, l)
    nq = l // tq
    return pl.pallas_call(
        functools.partial(_fox_attn_body, tq),
        grid=(nb, nh // 2, nq),
        in_specs=[pl.BlockSpec((None, 2, 128, tq), lambda b, hp, i: (b, hp, 0, i)),
                  pl.BlockSpec((None, 2, l, 128), lambda b, hp, i: (b, hp, 0, 0)),
                  pl.BlockSpec((None, 2, V_AUG_ROWS, l), lambda b, hp, i: (b, hp, 0, 0))],
        out_specs=pl.BlockSpec((tq, 128), lambda b, hp, i: (b * nq + i, hp)),
        out_shape=jax.ShapeDtypeStruct((nb * l, W_HEAD), BF16),
        scratch_shapes=[pltpu.VMEM((2, 2, tq // 2, tq), F32), pltpu.VMEM((2, 1, tq), F32),
                        pltpu.VMEM((2, V_AUG_ROWS, tq), F32)],
        compiler_params=_params("parallel", "parallel", "arbitrary"),
        name="fox_attn",
    )(qt, ka, vt)


def _ret_tables(pos0, l_pad, l_real, chunk):
    c_real = chunk if l_real % chunk == 0 else l_real
    half = HEAD_DIM // 2
    inv = ROPE_BASE ** (-np.arange(half, dtype=np.float64) / half)
    inv_l = np.tile(inv, 2 * N_HEADS)
    sgn = np.tile(np.concatenate([-np.ones(half), np.ones(half)]), N_HEADS)
    n_a = l_pad // chunk
    ang_a = (pos0 + chunk * np.arange(n_a))[:, None] * inv_l[None, :]
    ang_b = np.arange(chunk)[:, None] * inv_l[None, :]
    log_g = np.log1p(-np.exp2(-5.0 - np.arange(N_HEADS, dtype=np.float64)))
    idx = np.arange(chunk, dtype=np.float64)
    diff = idx[:, None] - idx[None, :]
    decay = np.where(diff >= 0, np.exp(np.maximum(diff, 0.0)[None] * log_g[:, None, None]), 0.0)
    lg_l = np.repeat(log_g, HEAD_DIM)
    xi = np.exp((idx[:, None] + 1.0) * lg_l[None, :])
    zeta = np.exp((c_real - 1.0 - idx[:, None]) * lg_l[None, :])
    head = np.arange(W_HEAD) // HEAD_DIM
    bd = (head[:, None] == head[None, :]).astype(np.float64)
    gmat = bd * np.exp(c_real * lg_l)[:, None]
    f = lambda a: jnp.asarray(a, dtype=F32)
    return dict(ca=f(np.cos(ang_a)), sa=f(sgn * np.sin(ang_a)), cb=f(np.cos(ang_b)), sb=f(sgn * np.sin(ang_b)),
                decay=f(decay), xi=f(xi), zeta=f(zeta), gmat=f(gmat), bd=f(bd),
                bdavg=jnp.asarray(bd / HEAD_DIM, dtype=BF16))


def _ret_body(tm, ch, n_seq, q_ref, k_ref, v_ref, g_ref, ca_ref, sa_ref, cb_ref, sb_ref, dec_ref, xi_ref,
              zeta_ref, gmat_ref, bd_ref, bdavg_ref, gn_ref, s0_ref, y_ref, sout_ref, st_ref):
    i = pl.program_id(1)
    n_chunk = tm // ch

    @pl.when(i == 0)
    def _():
        for s in range(n_seq):
            st_ref[s] = jnp.concatenate([s0_ref[s]] * N_HEADS, axis=-1) * bd_ref[...]

    lane = lax.broadcasted_iota(jnp.int32, (ch, W_HEAD), 1)
    lane128 = lax.broadcasted_iota(jnp.int32, (ch, 128), 1)
    first_half = (lane128 % HEAD_DIM) < (HEAD_DIM // 2)

    def swap_halves(x):
        parts = []
        for blk in range(W_HEAD // 128):
            xb = x[:, blk * 128:(blk + 1) * 128]
            parts.append(jnp.where(first_half, pltpu.roll(xb, 128 - HEAD_DIM // 2, 1),
                                   pltpu.roll(xb, HEAD_DIM // 2, 1)))
        return jnp.concatenate(parts, axis=-1)

    def gmean(t):
        hi, lo = _split2(t)
        return _dot(hi, bdavg_ref[...]) + _dot(lo, bdavg_ref[...])

    def chunk(s, c):
        r0 = s * tm + c * ch
        a = i * n_chunk + c
        c_a = ca_ref[pl.ds(a, 1), :]
        s_a = sa_ref[pl.ds(a, 1), :]
        cos = c_a * cb_ref[...] - s_a * sb_ref[...]
        sin = s_a * cb_ref[...] + c_a * sb_ref[...]
        q = q_ref[pl.ds(r0, ch), :]
        k = k_ref[pl.ds(r0, ch), :]
        q = q * cos + swap_halves(q) * sin
        k = (k * cos + swap_halves(k) * sin) * FOX_SCALE
        qb = q.astype(BF16)
        kb = k.astype(BF16)
        vb = v_ref[pl.ds(r0, ch), :].astype(BF16)
        st = st_ref[s]
        o = _dot(qb, st.astype(BF16)) * xi_ref[...]
        for h in range(N_HEADS):
            mh = (lane // HEAD_DIM) == h
            qm = jnp.where(mh, q, 0.0).astype(BF16)
            att = _dot_nt(qm, kb) * dec_ref[h]
            o = o + jnp.where(mh, _dot(att.astype(BF16), vb), 0.0)
        kz = (k * zeta_ref[...]).astype(BF16)
        st_ref[s] = st * gmat_ref[...] + _dot_tn(kz, vb) * bd_ref[...]
        mu = gmean(o)
        d = o - mu
        var = gmean(d * d)
        on = d * lax.rsqrt(var + EPS) * gn_ref[...]
        gate = g_ref[pl.ds(r0, ch), :]
        y_ref[pl.ds(r0, ch), :] = (on * (gate * _sigmoid(gate))).astype(BF16)

    for c in range(n_chunk):
        for s in range(n_seq):
            chunk(s, c)

    @pl.when(i == pl.num_programs(1) - 1)
    def _():
        for s in range(n_seq):
            st = st_ref[s]
            acc = st[:, 0:HEAD_DIM]
            for h in range(1, N_HEADS):
                acc = acc + st[:, h * HEAD_DIM:(h + 1) * HEAD_DIM]
            sout_ref[s] = acc


def _retention(proj, gn, s0, nb, l, l_real, pos0, tm=512, n_seq=1):
    m = proj.shape[0]
    tm = min(tm, l)
    ch = min(RET_CHUNK, tm)
    nl = l // tm
    assert n_seq == 1 or (nl == 1 and nb % n_seq == 0)
    t = _ret_tables(pos0, l, l_real, ch)
    rows = lambda b, i: b * nl + i
    const2 = lambda b, i: (0, 0)
    col = lambda cidx: pl.BlockSpec((n_seq * tm, W_HEAD), lambda b, i: (rows(b, i), cidx))
    full = lambda a: pl.BlockSpec(a.shape, (lambda b, i: (0,) * a.ndim))
    in_specs = [col(COL_RQ), col(COL_RK), col(COL_RV), col(COL_RG),
                full(t["ca"]), full(t["sa"]), full(t["cb"]), full(t["sb"]), full(t["decay"]),
                full(t["xi"]), full(t["zeta"]), full(t["gmat"]), full(t["bd"]), full(t["bdavg"]),
                pl.BlockSpec((1, W_HEAD), const2),
                pl.BlockSpec((n_seq, W_HEAD, HEAD_DIM), lambda b, i: (b, 0, 0))]
    return pl.pallas_call(
        functools.partial(_ret_body, tm, ch, n_seq),
        grid=(nb // n_seq, nl),
        in_specs=in_specs,
        out_specs=[pl.BlockSpec((n_seq * tm, W_HEAD), lambda b, i: (rows(b, i), 0)),
                   pl.BlockSpec((n_seq, W_HEAD, HEAD_DIM), lambda b, i: (b, 0, 0))],
        out_shape=[jax.ShapeDtypeStruct((m, W_HEAD), BF16),
                   jax.ShapeDtypeStruct((nb, W_HEAD, HEAD_DIM), F32)],
        scratch_shapes=[pltpu.VMEM((n_seq, W_HEAD, W_HEAD), F32)],
        compiler_params=_params("arbitrary", "arbitrary"),
        name="retention",
    )(proj, proj, proj, proj, t["ca"], t["sa"], t["cb"], t["sb"], t["decay"], t["xi"], t["zeta"],
      t["gmat"], t["bd"], t["bdavg"], gn.reshape(1, W_HEAD), s0)


def _lru_body(tm, l_real, lx_ref, lg_ref, cw_ref, cb_ref, wr_ref, wi_ref, br_ref, bi_ref, lam_ref,
              conv0_ref, h0_ref, y_ref, hlast_ref, convnew_ref, xpad_ref, a_ref, b_ref, h_ref, hcar_ref):
    i = pl.program_id(1)
    w = lx_ref.shape[1]
    t_last, r_last = (l_real - 1) // tm, (l_real - 1) % tm

    @pl.when(i == 0)
    def _():
        xpad_ref[0:8, :] = conv0_ref[...]
        hcar_ref[...] = h0_ref[...]

    x = lx_ref[...]
    xpad_ref[8:8 + tm, :] = x
    xc = cb_ref[...] + cw_ref[CONV_W - 1:CONV_W, :] * x
    for j in range(1, CONV_W):
        xc = xc + cw_ref[CONV_W - 1 - j:CONV_W - j, :] * xpad_ref[pl.ds(8 - j, tm), :]

    xb = xc.astype(BF16)
    r = _sigmoid(_dot(xb, wr_ref[...]) + br_ref[...])
    ig = _sigmoid(_dot(xb, wi_ref[...]) + bi_ref[...])
    z = -lam_ref[...]
    softplus = jnp.maximum(z, 0.0) + jnp.log1p(jnp.exp(-jnp.abs(z)))
    log_a = (-LRU_C) * r * softplus
    a = jnp.exp(log_a)
    th = jnp.tanh(log_a)
    u = jnp.sqrt(-2.0 * th / (1.0 - th)) * (ig * xc)

    rowmod = lax.broadcasted_iota(jnp.int32, (tm, w), 0) % 8
    for sh in (1, 2, 4):
        a_sh = pltpu.roll(a, sh, 0)
        u_sh = pltpu.roll(u, sh, 0)
        valid = rowmod >= sh
        u = jnp.where(valid, a * u_sh + u, u)
        a = jnp.where(valid, a * a_sh, a)
    a_ref[...] = a
    b_ref[...] = u

    def group(gi, h):
        r0 = pl.multiple_of(gi * 8, 8)
        hg = a_ref[pl.ds(r0, 8), :] * h + b_ref[pl.ds(r0, 8), :]
        h_ref[pl.ds(r0, 8), :] = hg
        return hg[7:8, :]

    hcar_ref[...] = lax.fori_loop(0, tm // 8, group, hcar_ref[...])

    g = lg_ref[...]
    gelu = 0.5 * g * (1.0 + jnp.tanh(0.7978845608028654 * (g + 0.044715 * (g * g * g))))
    y_ref[...] = (h_ref[...] * gelu).astype(BF16)

    @pl.when(i == t_last)
    def _():
        hlast_ref[...] = h_ref[r_last:r_last + 1, :]
        convnew_ref[...] = xpad_ref[pl.ds(r_last + 1, 8), :]

    xpad_ref[0:8, :] = xpad_ref[tm:tm + 8, :]


def _lru(proj, cw, cb, wr, wi, br, bi, lam, conv0, h0, nb, l, l_real, tm=512):
    m = proj.shape[0]
    w = cw.shape[1]
    tm = min(tm, l)
    nl = l // tm
    rows = lambda b, i: b * nl + i
    const2 = lambda b, i: (0, 0)
    vec = pl.BlockSpec((1, w), const2)
    per_b = lambda r: pl.BlockSpec((None, r, w), lambda b, i: (b, 0, 0))
    return pl.pallas_call(
        functools.partial(_lru_body, tm, l_real),
        grid=(nb, nl),
        in_specs=[pl.BlockSpec((tm, w), lambda b, i: (rows(b, i), COL_LX)),
                  pl.BlockSpec((tm, w), lambda b, i: (rows(b, i), COL_LG)),
                  pl.BlockSpec((CONV_W, w), const2), vec,
                  pl.BlockSpec((w, w), const2), pl.BlockSpec((w, w), const2), vec, vec, vec,
                  per_b(8), per_b(1)],
        out_specs=[pl.BlockSpec((tm, w), lambda b, i: (rows(b, i), 0)), per_b(1), per_b(8)],
        out_shape=[jax.ShapeDtypeStruct((m, w), BF16), jax.ShapeDtypeStruct((nb, 1, w), F32),
                   jax.ShapeDtypeStruct((nb, 8, w), F32)],
        scratch_shapes=[pltpu.VMEM((tm + 8, w), F32), pltpu.VMEM((tm, w), F32), pltpu.VMEM((tm, w), F32),
                        pltpu.VMEM((tm, w), F32), pltpu.VMEM((1, w), F32)],
        compiler_params=_params("arbitrary", "arbitrary"),
        name="lru",
    )(proj, proj, cw, cb.reshape(1, w), wr, wi, br.reshape(1, w), bi.reshape(1, w), lam.reshape(1, w),
      conv0, h0)


def _decode_body(n_pg, layer, pt_ref, qn_ref, knew_ref, vnew_ref, lfnew_ref, kt_hbm, vt_hbm, lf_hbm,
                 o_ref, kbuf, vbuf, lfbuf, sem, m_ref, l_ref, acc_ref, car_ref, qm_ref):
    b_i = pl.program_id(0)
    p_i = pl.program_id(1)
    n_steps = pl.num_programs(1)
    t = b_i * n_steps + p_i
    slot = lax.rem(t, 2)
    page = kbuf.shape[3]

    def page_copies(pages, sl):
        cps = []
        for g in range(n_pg):
            pg = pages(g)
            cps.append(pltpu.make_async_copy(kt_hbm.at[layer, pg], kbuf.at[sl, g], sem.at[0, sl]))
            cps.append(pltpu.make_async_copy(vt_hbm.at[layer, pg], vbuf.at[sl, g], sem.at[1, sl]))
            cps.append(pltpu.make_async_copy(lf_hbm.at[layer, pg], lfbuf.at[sl, g], sem.at[2, sl]))
        return cps

    @pl.when(t == 0)
    def _():
        for cp in page_copies(lambda g: pt_ref[0, g], 0):
            cp.start()

    @pl.when(t + 1 < pl.num_programs(0) * n_steps)
    def _():
        wrap = p_i + 1 == n_steps
        b_n = jnp.where(wrap, b_i + 1, b_i)
        p_n = jnp.where(wrap, 0, p_i + 1)
        for cp in page_copies(lambda g: pt_ref[b_n, p_n * n_pg + g], 1 - slot):
            cp.start()

    for cp in page_copies(lambda g: 0, slot):
        cp.wait()
    k_refs = [kbuf.at[slot, g] for g in range(n_pg)]
    v_refs = [vbuf.at[slot, g] for g in range(n_pg)]
    lf_refs = [lfbuf.at[slot, g] for g in range(n_pg)]

    row = lax.broadcasted_iota(jnp.int32, (8, W_HEAD), 0)
    lane = lax.broadcasted_iota(jnp.int32, (8, W_HEAD), 1)
    own = (lane // HEAD_DIM) == row

    @pl.when(p_i == 0)
    def _():
        q = jnp.where(own, jnp.broadcast_to(qn_ref[0:1, :] * FOX_SCALE, (8, W_HEAD)), 0.0)
        qm_ref[...] = _pad16(q).astype(BF16)
        m_ref[...] = jnp.full_like(m_ref, NEG_INF)
        l_ref[...] = jnp.zeros_like(l_ref)
        acc_ref[...] = jnp.zeros_like(acc_ref)
        car_ref[...] = jnp.zeros_like(car_ref)

    qm = qm_ref[...]

    def update(s, pv_fn):
        m_old = m_ref[...]
        m_new = jnp.maximum(m_old, jnp.max(s, axis=-1, keepdims=True))
        alpha = jnp.exp(m_old - m_new)
        p = jnp.exp(s - m_new)
        l_ref[...] = alpha * l_ref[...] + jnp.sum(p, axis=-1, keepdims=True)
        acc_ref[...] = alpha * acc_ref[...] + pv_fn(p)
        m_ref[...] = m_new

    r_i = lax.broadcasted_iota(jnp.int32, (page, page), 0)
    c_i = lax.broadcasted_iota(jnp.int32, (page, page), 1)
    upper = jnp.where(r_i <= c_i, 1.0, 0.0).astype(BF16)
    hi, mid, lo = _split3(jnp.concatenate([lf_refs[g][...] for g in range(n_pg)], axis=0))
    c_all = _dot(hi, upper) + _dot(mid, upper) + _dot(lo, upper)
    nr = n_pg * 8
    rr = lax.broadcasted_iota(jnp.int32, (nr, nr), 0)
    cc = lax.broadcasted_iota(jnp.int32, (nr, nr), 1)
    before = jnp.where((rr % 8 == cc % 8) & (cc // 8 < rr // 8), 1.0, 0.0).astype(BF16)
    chi, cmid, clo = _split3(c_all)
    pref = _dot(before, chi) + _dot(before, cmid) + _dot(before, clo)
    c_all = c_all + pref[:, page - 1:page] + jnp.concatenate([car_ref[...]] * n_pg, axis=0)
    car_ref[...] = c_all[nr - 8:nr, page - 1:page]
    s_parts = [_dot(qm, k_refs[g][...].astype(BF16))[0:8, :] - c_all[g * 8:(g + 1) * 8, :]
               for g in range(n_pg)]

    def pv_pages(p):
        pb = _pad16(p).astype(BF16)
        pv = _dot_nt(pb[:, 0:page], v_refs[0][...].astype(BF16))
        for g in range(1, n_pg):
            pv = pv + _dot_nt(pb[:, g * page:(g + 1) * page], v_refs[g][...].astype(BF16))
        return pv[0:8, :]

    update(jnp.concatenate(s_parts, axis=-1), pv_pages)

    @pl.when(p_i == pl.num_programs(1) - 1)
    def _():
        lane128 = lax.broadcasted_iota(jnp.int32, (8, 128), 1)
        row128 = lax.broadcasted_iota(jnp.int32, (8, 128), 0)
        lf_col = jnp.sum(jnp.where(lane128 == row128, jnp.broadcast_to(lfnew_ref[0:1, :], (8, 128)), 0.0),
                         axis=-1, keepdims=True)
        k_new = knew_ref[0:1, :].astype(BF16).astype(F32)
        v_new = vnew_ref[0:1, :].astype(BF16).astype(F32)
        s_new = jnp.sum(qm[0:8, :].astype(F32) * k_new, axis=-1, keepdims=True) - (car_ref[...] + lf_col)
        update(s_new, lambda p: p.astype(BF16).astype(F32) * v_new)
        o_ref[...] = jnp.sum(jnp.where(own, acc_ref[...] / l_ref[...], 0.0), axis=0, keepdims=True)


def _fox_decode(layer, page_table, qn, kn, proj, logf, cache_kt, cache_vt, cache_lft, nb):
    n_pages = page_table.shape[1]
    n_pg = min(PAGES_PER_STEP, n_pages)
    sr = SAMPLE_ROWS
    page = cache_kt.shape[3]
    tok = lambda cidx: (lambda b, p, pt: (b, cidx))
    hbm = pl.BlockSpec(memory_space=pl.ANY)
    in_specs = [pl.BlockSpec((sr, W_HEAD), tok(0)), pl.BlockSpec((sr, W_HEAD), tok(0)),
                pl.BlockSpec((sr, W_HEAD), tok(COL_FV)), pl.BlockSpec((sr, 128), tok(0)),
                hbm, hbm, hbm]
    grid_spec = pltpu.PrefetchScalarGridSpec(
        num_scalar_prefetch=1,
        grid=(nb, n_pages // n_pg),
        in_specs=in_specs,
        out_specs=pl.BlockSpec((None, 1, W_HEAD), lambda b, p, pt: (b, 0, 0)),
        scratch_shapes=[pltpu.VMEM((2, n_pg, W_HEAD, page), F32), pltpu.VMEM((2, n_pg, W_HEAD, page), F32),
                        pltpu.VMEM((2, n_pg, 8, page), F32), pltpu.SemaphoreType.DMA((3, 2)),
                        pltpu.VMEM((8, 1), F32), pltpu.VMEM((8, 1), F32), pltpu.VMEM((8, W_HEAD), F32),
                        pltpu.VMEM((8, 1), F32), pltpu.VMEM((16, W_HEAD), BF16)])
    return pl.pallas_call(
        functools.partial(_decode_body, n_pg, layer),
        grid_spec=grid_spec,
        out_shape=jax.ShapeDtypeStruct((nb, 1, W_HEAD), F32),
        compiler_params=_params("arbitrary", "arbitrary"),
        name="fox_decode",
    )(page_table, qn, kn, proj, logf, cache_kt, cache_vt, cache_lft)


def _block_diag(w):
    n, d, e = w.shape
    eye = jnp.eye(n, dtype=w.dtype)
    return (eye[:, None, :, None] * w[:, :, None, :]).reshape(n * d, n * e)


def _mixer_common(proj, nb, l, l_real, pos0, s0, conv0, h0, lw):
    n_seq = 4 if (l <= RET_CHUNK and nb % 4 == 0) else 1
    y_ret, s_new = _retention(proj, lw["ret_gn"], s0, nb, l, l_real, pos0, n_seq=n_seq)
    y_lru, h_last, conv_new = _lru(proj, lw["conv_w"], lw["conv_b"], lw["wr"], lw["wi"], lw["br"], lw["bi"],
                                   lw["lam"], conv0, h0, nb, l, l_real)
    return y_ret, y_lru, s_new, h_last, conv_new


def kernel(x_prompt, x_sample, cache_k, cache_v, cache_logf, state_ret, state_lru, state_conv, page_table, norm_ffn1, ffn1_gate, ffn1_up, ffn1_down, norm_mix, w_in, ret_gn, conv_w, conv_b, lru_wr, lru_br, lru_wi, lru_bi, lru_lambda, fox_qn, fox_kn, fox_bf, w_out, norm_ffn2, ffn2_gate, ffn2_up, ffn2_down):
    bp, lp, d = x_prompt.shape
    bs, ls, _ = x_sample.shape
    assert ls == 1
    depth = w_in.shape[0]
    n_pool, page = cache_k.shape[1], cache_k.shape[2]
    past_len = page_table.shape[1] * page
    w_lru = conv_w.shape[2]
    sr = SAMPLE_ROWS
    in_width = w_in.shape[2]

    cast = lambda w: [w[l].astype(BF16) for l in range(depth)]
    g1, u1, d1 = cast(ffn1_gate), cast(ffn1_up), cast(ffn1_down)
    g2, u2, d2 = cast(ffn2_gate), cast(ffn2_up), cast(ffn2_down)
    w_out_b = cast(w_out)
    w_in_b = [jnp.pad(w_in[l], ((0, 0), (0, PROJ_W - in_width))).astype(BF16) for l in range(depth)]
    cache_k4 = jnp.transpose(cache_k, (0, 1, 3, 4, 2)).reshape(depth, n_pool, W_HEAD, page)
    cache_v4 = jnp.transpose(cache_v, (0, 1, 3, 4, 2)).reshape(depth, n_pool, W_HEAD, page)
    cache_lft = jnp.pad(jnp.swapaxes(cache_logf, 2, 3), ((0, 0), (0, 0), (0, 8 - N_HEADS), (0, 0)))

    xp = x_prompt.reshape(bp * lp, d)
    xs = x_sample.reshape(bs, d)
    zero_s = jnp.zeros((bp, W_HEAD, HEAD_DIM), F32)
    zero_c = jnp.zeros((bp, 8, w_lru), F32)
    zero_h = jnp.zeros((bp, 1, w_lru), F32)

    outs_p = [[] for _ in range(6)]
    outs_s = [[] for _ in range(6)]
    for l in range(depth):
        lw = dict(ret_gn=ret_gn[l], conv_w=conv_w[l], conv_b=conv_b[l],
                  wr=_block_diag(lru_wr[l]).astype(BF16), wi=_block_diag(lru_wi[l]).astype(BF16),
                  br=lru_br[l], bi=lru_bi[l], lam=lru_lambda[l])

        xp = _ffn(xp, norm_ffn1[l], g1[l], u1[l], d1[l])
        proj = _inproj(xp, norm_mix[l], w_in_b[l])
        kn_t, v_t, logf_t, qa, ka, va = _fox_prep(proj, fox_qn[l], fox_kn[l], fox_bf[l], bp, lp, True)
        y_fox = _fox_attn(qa, ka, va)
        y_ret, y_lru, s_new, h_last, conv_new = _mixer_common(proj, bp, lp, lp, 0, zero_s, zero_c, zero_h, lw)
        xp = _ffn(xp, norm_ffn2[l], g2[l], u2[l], d2[l], mix=(y_ret, y_lru, y_fox, w_out_b[l]))
        heads_last = lambda a: jnp.transpose(a.reshape(bp, N_HEADS, HEAD_DIM, lp), (0, 3, 1, 2))
        outs_p[0].append(heads_last(kn_t))
        outs_p[1].append(heads_last(v_t))
        outs_p[2].append(jnp.transpose(logf_t[:, :N_HEADS, :], (0, 2, 1)))
        outs_p[3].append(s_new.reshape(bp, N_HEADS, HEAD_DIM, HEAD_DIM))
        outs_p[4].append(h_last.reshape(bp, w_lru))
        outs_p[5].append(conv_new[:, 8 - (CONV_W - 1):, :])

        xs = _ffn(xs, norm_ffn1[l], g1[l], u1[l], d1[l])
        xs_pad = jnp.pad(xs[:, None, :], ((0, 0), (0, sr - 1), (0, 0))).reshape(bs * sr, d)
        proj_s = _inproj(xs_pad, norm_mix[l], w_in_b[l])
        kn_s, logf_s, qn_s = _fox_prep(proj_s, fox_qn[l], fox_kn[l], fox_bf[l], 1, bs * sr, False)
        y_fox_s = _fox_decode(l, page_table, qn_s, kn_s, proj_s, logf_s, cache_k4, cache_v4, cache_lft, bs)
        s0 = state_ret[l].reshape(bs, W_HEAD, HEAD_DIM)
        conv0 = jnp.pad(state_conv[l], ((0, 0), (8 - (CONV_W - 1), 0), (0, 0)))
        h0 = state_lru[l].reshape(bs, 1, w_lru)
        y_ret_s, y_lru_s, s_new_s, h_last_s, conv_new_s = _mixer_common(
            proj_s, bs, sr, 1, past_len, s0, conv0, h0, lw)
        first = lambda a: a.reshape(bs, sr, a.shape[-1])[:, 0, :]
        xs = _ffn(xs, norm_ffn2[l], g2[l], u2[l], d2[l],
                  mix=(first(y_ret_s), first(y_lru_s), y_fox_s.reshape(bs, W_HEAD).astype(BF16), w_out_b[l]))
        outs_s[0].append(first(kn_s).reshape(bs, 1, N_HEADS, HEAD_DIM))
        outs_s[1].append(first(proj_s)[:, COL_FV * W_HEAD:(COL_FV + 1) * W_HEAD].reshape(bs, 1, N_HEADS, HEAD_DIM))
        outs_s[2].append(first(logf_s)[:, :N_HEADS].reshape(bs, 1, N_HEADS))
        outs_s[3].append(s_new_s.reshape(bs, N_HEADS, HEAD_DIM, HEAD_DIM))
        outs_s[4].append(h_last_s.reshape(bs, w_lru))
        outs_s[5].append(conv_new_s[:, 8 - (CONV_W - 1):, :])

    stk = lambda lst: jnp.stack(lst, axis=0)
    return (xp.reshape(bp, lp, d), xs.reshape(bs, 1, d),
            *[stk(o) for o in outs_p], *[stk(o) for o in outs_s])
```

```python
import functools

import numpy as np
import jax
import jax.numpy as jnp
from jax import lax
from jax.experimental import pallas as pl
from jax.experimental.pallas import tpu as pltpu

F32 = jnp.float32
BF16 = jnp.bfloat16

HEAD_DIM = 64
N_HEADS = 4
W_HEAD = N_HEADS * HEAD_DIM
CONV_W = 4
LRU_C = 8.0
RET_CHUNK = 128
ROPE_BASE = 10000.0
EPS = 1e-6
NEG_INF = -1e30
FOX_SCALE = HEAD_DIM ** -0.5
LOG2E = 1.4426950408889634
V_AUG_ROWS = 80
SAMPLE_ROWS = 16
PAGES_PER_STEP = 32
VMEM_LIMIT = 48 * 1024 * 1024

COL_RQ, COL_RK, COL_RV, COL_RG = 0, 1, 2, 3
COL_LX, COL_LG = 2, 3
COL_FQ, COL_FK, COL_FV = 8, 9, 10
COL_FF = 22
PROJ_W = 23 * 128


def _dot(a, b):
    return jnp.dot(a, b, preferred_element_type=F32)


def _dot_nt(a, b):
    return lax.dot_general(a, b, (((1,), (1,)), ((), ())), preferred_element_type=F32)


def _dot_tn(a, b):
    return lax.dot_general(a, b, (((0,), (0,)), ((), ())), preferred_element_type=F32)


def _sigmoid(x):
    return 1.0 / (1.0 + jnp.exp(-x))


def _split2(x):
    hi = x.astype(BF16)
    lo = (x - hi.astype(F32)).astype(BF16)
    return hi, lo


def _split3(x):
    hi = x.astype(BF16)
    r = x - hi.astype(F32)
    mid = r.astype(BF16)
    lo = (r - mid.astype(F32)).astype(BF16)
    return hi, mid, lo


def _params(*sem):
    return pltpu.CompilerParams(dimension_semantics=sem, vmem_limit_bytes=VMEM_LIMIT)


def _ffn_body(has_mix, tf, *refs):
    if has_mix:
        (x_ref, yr_ref, yl_ref, yf_ref, wo_ref, g_ref, wg_ref, wu_ref, wd_ref,
         o_ref, xres_ref, xn_ref, hid_ref) = refs
    else:
        x_ref, g_ref, wg_ref, wu_ref, wd_ref, o_ref, xres_ref, xn_ref, hid_ref = refs
    x = x_ref[...]
    if has_mix:
        w1 = yr_ref.shape[1]
        w2 = w1 + yl_ref.shape[1]
        x = (x + _dot(yr_ref[...], wo_ref[0:w1, :]) + _dot(yl_ref[...], wo_ref[w1:w2, :])
             + _dot(yf_ref[...], wo_ref[w2:, :]))
    xres_ref[...] = x
    ms = jnp.mean(x * x, axis=-1, keepdims=True)
    xn_ref[...] = (x * lax.rsqrt(ms + EPS) * g_ref[...]).astype(BF16)

    xn = xn_ref[...]
    for j in range(wg_ref.shape[1] // tf):
        gate = _dot(xn, wg_ref[:, j * tf:(j + 1) * tf])
        up = _dot(xn, wu_ref[:, j * tf:(j + 1) * tf])
        hid_ref[:, j * tf:(j + 1) * tf] = (gate * _sigmoid(gate) * up).astype(BF16)
    o_ref[...] = xres_ref[...] + 0.5 * _dot(hid_ref[...], wd_ref[...])


def _ffn(x, g, wg, wu, wd, mix=None, tm=512, tf=256):
    m, d = x.shape
    f = wg.shape[1]
    tm = min(tm, m)
    row = lambda i: (i, 0)
    resident = lambda a: pl.BlockSpec(a.shape, lambda i: (0, 0), pipeline_mode=pl.Buffered(1))
    in_specs = [pl.BlockSpec((tm, d), row)]
    args = [x]
    if mix is not None:
        yr, yl, yf, wo = mix
        in_specs += [pl.BlockSpec((tm, yr.shape[1]), row), pl.BlockSpec((tm, yl.shape[1]), row),
                     pl.BlockSpec((tm, yf.shape[1]), row), resident(wo)]
        args += [yr, yl, yf, wo]
    g2 = g.reshape(1, d)
    in_specs += [resident(g2), resident(wg), resident(wu), resident(wd)]
    args += [g2, wg, wu, wd]
    return pl.pallas_call(
        functools.partial(_ffn_body, mix is not None, tf),
        grid=(m // tm,),
        in_specs=in_specs,
        out_specs=pl.BlockSpec((tm, d), row),
        out_shape=jax.ShapeDtypeStruct((m, d), F32),
        scratch_shapes=[pltpu.VMEM((tm, d), F32), pltpu.VMEM((tm, d), BF16), pltpu.VMEM((tm, f), BF16)],
        compiler_params=_params("parallel"),
        name="ffn_mix" if mix is not None else "ffn",
    )(*args)


def _inproj_body(x_ref, g_ref, w_ref, o_ref):
    x = x_ref[...]
    ms = jnp.mean(x * x, axis=-1, keepdims=True)
    xn = (x * lax.rsqrt(ms + EPS) * g_ref[...]).astype(BF16)
    o_ref[...] = _dot(xn, w_ref[...])


def _inproj(x, g, w, tm=512):
    m, d = x.shape
    n = w.shape[1]
    tm = min(tm, m)
    return pl.pallas_call(
        _inproj_body,
        grid=(m // tm,),
        in_specs=[pl.BlockSpec((tm, d), lambda i: (i, 0)),
                  pl.BlockSpec((1, d), lambda i: (0, 0)),
                  pl.BlockSpec((d, n), lambda i: (0, 0))],
        out_specs=pl.BlockSpec((tm, n), lambda i: (i, 0)),
        out_shape=jax.ShapeDtypeStruct((m, n), F32),
        compiler_params=_params("parallel"),
        name="inproj",
    )(x, g.reshape(1, d), w)


def _head_norm(x, g):
    outs = []
    for h in range(N_HEADS):
        xh = x[:, h * HEAD_DIM:(h + 1) * HEAD_DIM]
        ms = jnp.mean(xh * xh, axis=-1, keepdims=True)
        outs.append(xh * lax.rsqrt(ms + EPS) * g)
    return outs


def _pad16(x):
    return jnp.concatenate([x, jnp.zeros_like(x)], axis=0)


def _fox_prep_sample_body(tm, fq_ref, fk_ref, fv_ref, ff_ref, gq_ref, gk_ref, bf_ref,
                          kn_ref, logf_ref, qn_ref):
    del fv_ref
    lane = lax.broadcasted_iota(jnp.int32, (tm, 128), 1)
    z = ff_ref[...] + bf_ref[...]
    logf = jnp.minimum(z, 0.0) - jnp.log1p(jnp.exp(-jnp.abs(z)))
    logf_ref[...] = jnp.where(lane < N_HEADS, logf, 0.0)
    kn_ref[...] = jnp.concatenate(_head_norm(fk_ref[...], gk_ref[...]), axis=-1)
    qn_ref[...] = jnp.concatenate(_head_norm(fq_ref[...], gq_ref[...]), axis=-1)


def _fox_prep_prompt_body(tm, fq_ref, fk_ref, fv_ref, ff_ref, gq_ref, gk_ref, bf_ref,
                          knt_ref, vt_ref, lft_ref, qa_ref, ka_ref, va_ref, carry_ref):
    @pl.when(pl.program_id(1) == 0)
    def _():
        carry_ref[...] = jnp.zeros_like(carry_ref)

    q_t = fq_ref[...].T
    k_t = fk_ref[...].T
    v_t = fv_ref[...].T
    z = ff_ref[...].T[0:8, :] + bf_ref[...]
    lf = jnp.minimum(z, 0.0) - jnp.log1p(jnp.exp(-jnp.abs(z)))
    lft_ref[...] = lf
    vt_ref[...] = v_t

    r_i = lax.broadcasted_iota(jnp.int32, (tm, tm), 0)
    c_i = lax.broadcasted_iota(jnp.int32, (tm, tm), 1)
    upper = jnp.where(r_i <= c_i, 1.0, 0.0).astype(BF16)
    hi, mid, lo = _split3(_pad16(lf))
    cs = (_dot(hi, upper) + _dot(mid, upper) + _dot(lo, upper))[0:8, :] + carry_ref[...]
    carry_ref[...] = cs[:, tm - 1:tm]
    cs = cs * LOG2E

    sub = lax.broadcasted_iota(jnp.int32, (HEAD_DIM, tm), 0)
    v_tail = jnp.where(lax.broadcasted_iota(jnp.int32, (V_AUG_ROWS - HEAD_DIM, tm), 0) == 0, 1.0, 0.0)
    kn_parts = []
    for h in range(N_HEADS):
        rows = slice(h * HEAD_DIM, (h + 1) * HEAD_DIM)
        qh, kh = q_t[rows], k_t[rows]
        qn = qh * lax.rsqrt(jnp.mean(qh * qh, axis=0, keepdims=True) + EPS) * gq_ref[...]
        kn = kh * lax.rsqrt(jnp.mean(kh * kh, axis=0, keepdims=True) + EPS) * gk_ref[...]
        kn_parts.append(kn)
        ch = cs[h:h + 1, :]
        chi = ch.astype(BF16).astype(F32)
        r1 = ch - chi
        cmid = r1.astype(BF16).astype(F32)
        clo = r1 - cmid
        q_tail = jnp.where(sub == 0, chi, jnp.where(sub == 1, cmid, jnp.where(
            sub == 2, clo, jnp.where(sub < 6, 1.0, 0.0))))
        k_tail = jnp.where(sub < 3, 1.0, jnp.where(sub == 3, -chi, jnp.where(
            sub == 4, -cmid, jnp.where(sub == 5, -clo, 0.0))))
        qa_ref[h] = jnp.concatenate([qn * (FOX_SCALE * LOG2E), q_tail], axis=0).astype(BF16)
        ka_ref[h] = jnp.concatenate([kn, k_tail], axis=0).T.astype(BF16)
        va_ref[h] = jnp.concatenate([v_t[rows], v_tail], axis=0).astype(BF16)
    knt_ref[...] = jnp.concatenate(kn_parts, axis=0)


def _fox_prep(proj, gq, gk, bf, nb, l, prompt, tm=256):
    m = proj.shape[0]
    tm = min(tm, l)
    nl = l // tm
    rows = lambda b, i: b * nl + i
    const2 = lambda b, i: (0, 0)
    in_specs = [pl.BlockSpec((tm, W_HEAD), lambda b, i: (rows(b, i), COL_FQ)),
                pl.BlockSpec((tm, W_HEAD), lambda b, i: (rows(b, i), COL_FK)),
                pl.BlockSpec((tm, W_HEAD), lambda b, i: (rows(b, i), COL_FV)),
                pl.BlockSpec((tm, 128), lambda b, i: (rows(b, i), COL_FF))]
    if prompt:
        in_specs += [pl.BlockSpec((HEAD_DIM, 1), const2), pl.BlockSpec((HEAD_DIM, 1), const2),
                     pl.BlockSpec((8, 1), const2)]
        params = (gq.reshape(HEAD_DIM, 1), gk.reshape(HEAD_DIM, 1),
                  jnp.zeros((8, 1), F32).at[:N_HEADS, 0].set(bf))
        pos_minor = lambda r: pl.BlockSpec((None, r, tm), lambda b, i: (b, 0, i))
        aug = pl.BlockSpec((None, N_HEADS, tm, 128), lambda b, i: (b, 0, i, 0))
        aug_t = pl.BlockSpec((None, N_HEADS, 128, tm), lambda b, i: (b, 0, 0, i))
        aug_v = pl.BlockSpec((None, N_HEADS, V_AUG_ROWS, tm), lambda b, i: (b, 0, 0, i))
        out_specs = [pos_minor(W_HEAD), pos_minor(W_HEAD), pos_minor(8), aug_t, aug, aug_v]
        out_shape = [jax.ShapeDtypeStruct((nb, W_HEAD, l), F32), jax.ShapeDtypeStruct((nb, W_HEAD, l), F32),
                     jax.ShapeDtypeStruct((nb, 8, l), F32),
                     jax.ShapeDtypeStruct((nb, N_HEADS, 128, l), BF16),
                     jax.ShapeDtypeStruct((nb, N_HEADS, l, 128), BF16),
                     jax.ShapeDtypeStruct((nb, N_HEADS, V_AUG_ROWS, l), BF16)]
        scratch = [pltpu.VMEM((8, 1), F32)]
        body = functools.partial(_fox_prep_prompt_body, tm)
    else:
        in_specs += [pl.BlockSpec((1, HEAD_DIM), const2), pl.BlockSpec((1, HEAD_DIM), const2),
                     pl.BlockSpec((1, 128), const2)]
        params = (gq.reshape(1, HEAD_DIM), gk.reshape(1, HEAD_DIM),
                  jnp.zeros((1, 128), F32).at[0, :N_HEADS].set(bf))
        row_spec = lambda w: pl.BlockSpec((tm, w), lambda b, i: (rows(b, i), 0))
        out_specs = [row_spec(W_HEAD), row_spec(128), row_spec(W_HEAD)]
        out_shape = [jax.ShapeDtypeStruct((m, W_HEAD), F32), jax.ShapeDtypeStruct((m, 128), F32),
                     jax.ShapeDtypeStruct((m, W_HEAD), F32)]
        scratch = []
        body = functools.partial(_fox_prep_sample_body, tm)
    return pl.pallas_call(
        body,
        grid=(nb, nl),
        in_specs=in_specs,
        out_specs=out_specs,
        out_shape=out_shape,
        scratch_shapes=scratch,
        compiler_params=_params("arbitrary", "arbitrary"),
        name="fox_prep_prompt" if prompt else "fox_prep_sample",
    )(proj, proj, proj, proj, *params)


def _fox_attn_body(tq, tk, qt_ref, ka_ref, vt_ref, o_ref, s_ref, m_ref, acc_ref):
    qi = pl.program_id(2)
    n_diag = tq // tk
    heads = range(2)
    m_ref[...] = jnp.full_like(m_ref, NEG_INF)
    acc_ref[...] = jnp.zeros_like(acc_ref)

    def scores(slot, j):
        k0 = pl.multiple_of(j * tk, tk)
        for h in heads:
            s_ref[slot, h] = _dot(ka_ref[h, pl.ds(k0, tk), :], qt_ref[h])

    def absorb(h, s, v_t, lanes):
        m_old = m_ref[h, :, lanes]
        m_new = jnp.maximum(m_old, jnp.max(s, axis=0, keepdims=True))
        alpha = jnp.exp2(m_old - m_new)
        p = jnp.exp2(s - m_new)
        acc_ref[h, :, lanes] = alpha * acc_ref[h, :, lanes] + _dot(v_t, p.astype(BF16))
        m_ref[h, :, lanes] = m_new

    def consume(slot, j):
        k0 = pl.multiple_of(j * tk, tk)
        for h in heads:
            absorb(h, s_ref[slot, h], vt_ref[h, :, pl.ds(k0, tk)], slice(None))

    scores(0, 0)

    def pair(jj):
        scores(1, 2 * jj + 1)
        consume(0, 2 * jj)
        scores(0, 2 * jj + 2)
        consume(1, 2 * jj + 1)

    def four_pairs(t, carry):
        for u in range(4):
            pair(4 * t + u)
        return carry

    n_pairs = qi * (n_diag // 2)
    lax.fori_loop(0, n_pairs // 4, four_pairs, 0)
    done4 = (n_pairs // 4) * 4

    @pl.when(n_pairs % 4 >= 2)
    def _():
        pair(done4)
        pair(done4 + 1)

    @pl.when(n_pairs % 2 == 1)
    def _():
        pair(n_pairs - 1)

    def key_start(d):
        return pl.multiple_of((qi * n_diag + d) * tk, tk)

    def causal(n):
        return lax.broadcasted_iota(jnp.int32, (tk, n), 0) <= lax.broadcasted_iota(jnp.int32, (tk, n), 1)

    late = {d: [_dot(ka_ref[h, pl.ds(key_start(d), tk), :], qt_ref[h, :, d * tk:tq]) for h in heads]
            for d in range(1, n_diag)}
    for d in range(n_diag):
        for h in heads:
            s = s_ref[0, h] if d == 0 else late[d][h]
            absorb(h, jnp.where(causal(tq - d * tk), s, NEG_INF), vt_ref[h, :, pl.ds(key_start(d), tk)],
                   slice(d * tk, tq))

    outs = []
    for h in heads:
        acc = acc_ref[h]
        outs.append(acc[0:HEAD_DIM, :] / acc[HEAD_DIM:HEAD_DIM + 1, :])
    o_ref[...] = jnp.concatenate(outs, axis=0).T.astype(BF16)


def _fox_attn(qt, ka, vt, tq=512):
    nb, nh, l, _ = ka.shape
    tq = min(tq, l)
    tk = tq // 2
    nq = l // tq
    return pl.pallas_call(
        functools.partial(_fox_attn_body, tq, tk),
        grid=(nb, nh // 2, nq),
        in_specs=[pl.BlockSpec((None, 2, 128, tq), lambda b, hp, i: (b, hp, 0, i)),
                  pl.BlockSpec((None, 2, l, 128), lambda b, hp, i: (b, hp, 0, 0)),
                  pl.BlockSpec((None, 2, V_AUG_ROWS, l), lambda b, hp, i: (b, hp, 0, 0))],
        out_specs=pl.BlockSpec((tq, 128), lambda b, hp, i: (b * nq + i, hp)),
        out_shape=jax.ShapeDtypeStruct((nb * l, W_HEAD), BF16),
        scratch_shapes=[pltpu.VMEM((2, 2, tk, tq), F32), pltpu.VMEM((2, 1, tq), F32),
                        pltpu.VMEM((2, V_AUG_ROWS, tq), F32)],
        compiler_params=_params("parallel", "parallel", "arbitrary"),
        name="fox_attn",
    )(qt, ka, vt)


def _ret_tables(pos0, l_pad, l_real, chunk):
    c_real = chunk if l_real % chunk == 0 else l_real
    half = HEAD_DIM // 2
    inv = ROPE_BASE ** (-np.arange(half, dtype=np.float64) / half)
    inv_l = np.tile(inv, 2 * N_HEADS)
    sgn = np.tile(np.concatenate([-np.ones(half), np.ones(half)]), N_HEADS)
    n_a = l_pad // chunk
    ang_a = (pos0 + chunk * np.arange(n_a))[:, None] * inv_l[None, :]
    ang_b = np.arange(chunk)[:, None] * inv_l[None, :]
    log_g = np.log1p(-np.exp2(-5.0 - np.arange(N_HEADS, dtype=np.float64)))
    idx = np.arange(chunk, dtype=np.float64)
    diff = idx[:, None] - idx[None, :]
    decay = np.where(diff >= 0, np.exp(np.maximum(diff, 0.0)[None] * log_g[:, None, None]), 0.0)
    lg_l = np.repeat(log_g, HEAD_DIM)
    xi = np.exp((idx[:, None] + 1.0) * lg_l[None, :])
    zeta = np.exp((c_real - 1.0 - idx[:, None]) * lg_l[None, :])
    head = np.arange(W_HEAD) // HEAD_DIM
    bd = (head[:, None] == head[None, :]).astype(np.float64)
    gmat = bd * np.exp(c_real * lg_l)[:, None]
    f = lambda a: jnp.asarray(a, dtype=F32)
    return dict(ca=f(np.cos(ang_a)), sa=f(sgn * np.sin(ang_a)), cb=f(np.cos(ang_b)), sb=f(sgn * np.sin(ang_b)),
                decay=f(decay), xi=f(xi), zeta=f(zeta), gmat=f(gmat), bd=f(bd),
                bdavg=jnp.asarray(bd / HEAD_DIM, dtype=BF16))


def _ret_body(tm, ch, n_seq, q_ref, k_ref, v_ref, g_ref, ca_ref, sa_ref, cb_ref, sb_ref, dec_ref, xi_ref,
              zeta_ref, gmat_ref, bd_ref, bdavg_ref, gn_ref, s0_ref, y_ref, sout_ref, st_ref):
    i = pl.program_id(1)
    n_chunk = tm // ch

    @pl.when(i == 0)
    def _():
        for s in range(n_seq):
            st_ref[s] = jnp.concatenate([s0_ref[s]] * N_HEADS, axis=-1) * bd_ref[...]

    lane = lax.broadcasted_iota(jnp.int32, (ch, W_HEAD), 1)
    lane128 = lax.broadcasted_iota(jnp.int32, (ch, 128), 1)
    first_half = (lane128 % HEAD_DIM) < (HEAD_DIM // 2)

    def swap_halves(x):
        parts = []
        for blk in range(W_HEAD // 128):
            xb = x[:, blk * 128:(blk + 1) * 128]
            parts.append(jnp.where(first_half, pltpu.roll(xb, 128 - HEAD_DIM // 2, 1),
                                   pltpu.roll(xb, HEAD_DIM // 2, 1)))
        return jnp.concatenate(parts, axis=-1)

    def gmean(t):
        hi, lo = _split2(t)
        return _dot(hi, bdavg_ref[...]) + _dot(lo, bdavg_ref[...])

    def chunk(s, c):
        r0 = s * tm + c * ch
        a = i * n_chunk + c
        c_a = ca_ref[pl.ds(a, 1), :]
        s_a = sa_ref[pl.ds(a, 1), :]
        cos = c_a * cb_ref[...] - s_a * sb_ref[...]
        sin = s_a * cb_ref[...] + c_a * sb_ref[...]
        q = q_ref[pl.ds(r0, ch), :]
        k = k_ref[pl.ds(r0, ch), :]
        q = q * cos + swap_halves(q) * sin
        k = (k * cos + swap_halves(k) * sin) * FOX_SCALE
        qb = q.astype(BF16)
        kb = k.astype(BF16)
        vb = v_ref[pl.ds(r0, ch), :].astype(BF16)
        st = st_ref[s]
        o = _dot(qb, st.astype(BF16)) * xi_ref[...]
        for h in range(N_HEADS):
            mh = (lane // HEAD_DIM) == h
            qm = jnp.where(mh, q, 0.0).astype(BF16)
            att = _dot_nt(qm, kb) * dec_ref[h]
            o = o + jnp.where(mh, _dot(att.astype(BF16), vb), 0.0)
        kz = (k * zeta_ref[...]).astype(BF16)
        st_ref[s] = st * gmat_ref[...] + _dot_tn(kz, vb) * bd_ref[...]
        mu = gmean(o)
        d = o - mu
        var = gmean(d * d)
        on = d * lax.rsqrt(var + EPS) * gn_ref[...]
        gate = g_ref[pl.ds(r0, ch), :]
        y_ref[pl.ds(r0, ch), :] = (on * (gate * _sigmoid(gate))).astype(BF16)

    for c in range(n_chunk):
        for s in range(n_seq):
            chunk(s, c)

    @pl.when(i == pl.num_programs(1) - 1)
    def _():
        for s in range(n_seq):
            st = st_ref[s]
            acc = st[:, 0:HEAD_DIM]
            for h in range(1, N_HEADS):
                acc = acc + st[:, h * HEAD_DIM:(h + 1) * HEAD_DIM]
            sout_ref[s] = acc


def _retention(proj, gn, s0, nb, l, l_real, pos0, tm=512, n_seq=1):
    m = proj.shape[0]
    tm = min(tm, l)
    ch = min(RET_CHUNK, tm)
    nl = l // tm
    assert n_seq == 1 or (nl == 1 and nb % n_seq == 0)
    t = _ret_tables(pos0, l, l_real, ch)
    rows = lambda b, i: b * nl + i
    const2 = lambda b, i: (0, 0)
    col = lambda cidx: pl.BlockSpec((n_seq * tm, W_HEAD), lambda b, i: (rows(b, i), cidx))
    full = lambda a: pl.BlockSpec(a.shape, (lambda b, i: (0,) * a.ndim))
    in_specs = [col(COL_RQ), col(COL_RK), col(COL_RV), col(COL_RG),
                full(t["ca"]), full(t["sa"]), full(t["cb"]), full(t["sb"]), full(t["decay"]),
                full(t["xi"]), full(t["zeta"]), full(t["gmat"]), full(t["bd"]), full(t["bdavg"]),
                pl.BlockSpec((1, W_HEAD), const2),
                pl.BlockSpec((n_seq, W_HEAD, HEAD_DIM), lambda b, i: (b, 0, 0))]
    return pl.pallas_call(
        functools.partial(_ret_body, tm, ch, n_seq),
        grid=(nb // n_seq, nl),
        in_specs=in_specs,
        out_specs=[pl.BlockSpec((n_seq * tm, W_HEAD), lambda b, i: (rows(b, i), 0)),
                   pl.BlockSpec((n_seq, W_HEAD, HEAD_DIM), lambda b, i: (b, 0, 0))],
        out_shape=[jax.ShapeDtypeStruct((m, W_HEAD), BF16),
                   jax.ShapeDtypeStruct((nb, W_HEAD, HEAD_DIM), F32)],
        scratch_shapes=[pltpu.VMEM((n_seq, W_HEAD, W_HEAD), F32)],
        compiler_params=_params("arbitrary", "arbitrary"),
        name="retention",
    )(proj, proj, proj, proj, t["ca"], t["sa"], t["cb"], t["sb"], t["decay"], t["xi"], t["zeta"],
      t["gmat"], t["bd"], t["bdavg"], gn.reshape(1, W_HEAD), s0)


def _lru_body(tm, l_real, lx_ref, lg_ref, cw_ref, cb_ref, wr_ref, wi_ref, br_ref, bi_ref, lam_ref,
              conv0_ref, h0_ref, y_ref, hlast_ref, convnew_ref, xpad_ref, a_ref, b_ref, h_ref, hcar_ref):
    i = pl.program_id(1)
    w = lx_ref.shape[1]
    t_last, r_last = (l_real - 1) // tm, (l_real - 1) % tm

    @pl.when(i == 0)
    def _():
        xpad_ref[0:8, :] = conv0_ref[...]
        hcar_ref[...] = h0_ref[...]

    x = lx_ref[...]
    xpad_ref[8:8 + tm, :] = x
    xc = cb_ref[...] + cw_ref[CONV_W - 1:CONV_W, :] * x
    for j in range(1, CONV_W):
        xc = xc + cw_ref[CONV_W - 1 - j:CONV_W - j, :] * xpad_ref[pl.ds(8 - j, tm), :]

    xb = xc.astype(BF16)
    r = _sigmoid(_dot(xb, wr_ref[...]) + br_ref[...])
    ig = _sigmoid(_dot(xb, wi_ref[...]) + bi_ref[...])
    z = -lam_ref[...]
    softplus = jnp.maximum(z, 0.0) + jnp.log1p(jnp.exp(-jnp.abs(z)))
    log_a = (-LRU_C) * r * softplus
    a = jnp.exp(log_a)
    th = jnp.tanh(log_a)
    u = jnp.sqrt(-2.0 * th / (1.0 - th)) * (ig * xc)

    a = a.reshape(tm // 8, 8, w)
    u = u.reshape(tm // 8, 8, w)
    rowmod = lax.broadcasted_iota(jnp.int32, (tm // 8, 8, w), 1)
    for sh in (1, 2, 4):
        a_sh = pltpu.roll(a, sh, 1)
        u_sh = pltpu.roll(u, sh, 1)
        valid = rowmod >= sh
        u = jnp.where(valid, a * u_sh + u, u)
        a = jnp.where(valid, a * a_sh, a)
    a_ref[...] = a.reshape(tm, w)
    b_ref[...] = u.reshape(tm, w)

    def group(gi, h):
        r0 = pl.multiple_of(gi * 8, 8)
        hg = a_ref[pl.ds(r0, 8), :] * h + b_ref[pl.ds(r0, 8), :]
        h_ref[pl.ds(r0, 8), :] = hg
        return hg[7:8, :]

    hcar_ref[...] = lax.fori_loop(0, tm // 8, group, hcar_ref[...])

    g = lg_ref[...]
    gelu = 0.5 * g * (1.0 + jnp.tanh(0.7978845608028654 * (g + 0.044715 * (g * g * g))))
    y_ref[...] = (h_ref[...] * gelu).astype(BF16)

    @pl.when(i == t_last)
    def _():
        hlast_ref[...] = h_ref[r_last:r_last + 1, :]
        convnew_ref[...] = xpad_ref[pl.ds(r_last + 1, 8), :]

    xpad_ref[0:8, :] = xpad_ref[tm:tm + 8, :]


def _lru(proj, cw, cb, wr, wi, br, bi, lam, conv0, h0, nb, l, l_real, tm=512):
    m = proj.shape[0]
    w = cw.shape[1]
    tm = min(tm, l)
    nl = l // tm
    rows = lambda b, i: b * nl + i
    const2 = lambda b, i: (0, 0)
    vec = pl.BlockSpec((1, w), const2)
    per_b = lambda r: pl.BlockSpec((None, r, w), lambda b, i: (b, 0, 0))
    return pl.pallas_call(
        functools.partial(_lru_body, tm, l_real),
        grid=(nb, nl),
        in_specs=[pl.BlockSpec((tm, w), lambda b, i: (rows(b, i), COL_LX)),
                  pl.BlockSpec((tm, w), lambda b, i: (rows(b, i), COL_LG)),
                  pl.BlockSpec((CONV_W, w), const2), vec,
                  pl.BlockSpec((w, w), const2), pl.BlockSpec((w, w), const2), vec, vec, vec,
                  per_b(8), per_b(1)],
        out_specs=[pl.BlockSpec((tm, w), lambda b, i: (rows(b, i), 0)), per_b(1), per_b(8)],
        out_shape=[jax.ShapeDtypeStruct((m, w), BF16), jax.ShapeDtypeStruct((nb, 1, w), F32),
                   jax.ShapeDtypeStruct((nb, 8, w), F32)],
        scratch_shapes=[pltpu.VMEM((tm + 8, w), F32), pltpu.VMEM((tm, w), F32), pltpu.VMEM((tm, w), F32),
                        pltpu.VMEM((tm, w), F32), pltpu.VMEM((1, w), F32)],
        compiler_params=_params("arbitrary", "arbitrary"),
        name="lru",
    )(proj, proj, cw, cb.reshape(1, w), wr, wi, br.reshape(1, w), bi.reshape(1, w), lam.reshape(1, w),
      conv0, h0)


def _decode_body(n_pg, layer, pt_ref, qn_ref, knew_ref, vnew_ref, lfnew_ref, kt_hbm, vt_hbm, lf_hbm,
                 o_ref, kbuf, vbuf, lfbuf, sem, m_ref, l_ref, acc_ref, car_ref, qm_ref):
    b_i = pl.program_id(0)
    p_i = pl.program_id(1)
    n_steps = pl.num_programs(1)
    t = b_i * n_steps + p_i
    slot = lax.rem(t, 2)
    page = kbuf.shape[3]

    def page_copies(pages, sl):
        cps = []
        for g in range(n_pg):
            pg = pages(g)
            cps.append(pltpu.make_async_copy(kt_hbm.at[layer, pg], kbuf.at[sl, g], sem.at[0, sl]))
            cps.append(pltpu.make_async_copy(vt_hbm.at[layer, pg], vbuf.at[sl, g], sem.at[1, sl]))
            cps.append(pltpu.make_async_copy(lf_hbm.at[layer, pg], lfbuf.at[sl, g], sem.at[2, sl]))
        return cps

    @pl.when(t == 0)
    def _():
        for cp in page_copies(lambda g: pt_ref[0, g], 0):
            cp.start()

    @pl.when(t + 1 < pl.num_programs(0) * n_steps)
    def _():
        wrap = p_i + 1 == n_steps
        b_n = jnp.where(wrap, b_i + 1, b_i)
        p_n = jnp.where(wrap, 0, p_i + 1)
        for cp in page_copies(lambda g: pt_ref[b_n, p_n * n_pg + g], 1 - slot):
            cp.start()

    for cp in page_copies(lambda g: 0, slot):
        cp.wait()
    k_refs = [kbuf.at[slot, g] for g in range(n_pg)]
    v_refs = [vbuf.at[slot, g] for g in range(n_pg)]
    lf_refs = [lfbuf.at[slot, g] for g in range(n_pg)]

    row = lax.broadcasted_iota(jnp.int32, (8, W_HEAD), 0)
    lane = lax.broadcasted_iota(jnp.int32, (8, W_HEAD), 1)
    own = (lane // HEAD_DIM) == row

    @pl.when(p_i == 0)
    def _():
        q = jnp.where(own, jnp.broadcast_to(qn_ref[0:1, :] * FOX_SCALE, (8, W_HEAD)), 0.0)
        qm_ref[...] = _pad16(q).astype(BF16)
        m_ref[...] = jnp.full_like(m_ref, NEG_INF)
        l_ref[...] = jnp.zeros_like(l_ref)
        acc_ref[...] = jnp.zeros_like(acc_ref)
        car_ref[...] = jnp.zeros_like(car_ref)

    qm = qm_ref[...]

    def update(s, pv_fn):
        m_old = m_ref[...]
        m_new = jnp.maximum(m_old, jnp.max(s, axis=-1, keepdims=True))
        alpha = jnp.exp(m_old - m_new)
        p = jnp.exp(s - m_new)
        l_ref[...] = alpha * l_ref[...] + jnp.sum(p, axis=-1, keepdims=True)
        acc_ref[...] = alpha * acc_ref[...] + pv_fn(p)
        m_ref[...] = m_new

    r_i = lax.broadcasted_iota(jnp.int32, (page, page), 0)
    c_i = lax.broadcasted_iota(jnp.int32, (page, page), 1)
    upper = jnp.where(r_i <= c_i, 1.0, 0.0).astype(BF16)
    hi, mid, lo = _split3(jnp.concatenate([lf_refs[g][...] for g in range(n_pg)], axis=0))
    c_all = _dot(hi, upper) + _dot(mid, upper) + _dot(lo, upper)
    nr = n_pg * 8
    rr = lax.broadcasted_iota(jnp.int32, (nr, nr), 0)
    cc = lax.broadcasted_iota(jnp.int32, (nr, nr), 1)
    before = jnp.where((rr % 8 == cc % 8) & (cc // 8 < rr // 8), 1.0, 0.0).astype(BF16)
    chi, cmid, clo = _split3(c_all)
    pref = _dot(before, chi) + _dot(before, cmid) + _dot(before, clo)
    c_all = c_all + pref[:, page - 1:page] + jnp.concatenate([car_ref[...]] * n_pg, axis=0)
    car_ref[...] = c_all[nr - 8:nr, page - 1:page]
    s_parts = [_dot(qm, k_refs[g][...].astype(BF16))[0:8, :] - c_all[g * 8:(g + 1) * 8, :]
               for g in range(n_pg)]

    def pv_pages(p):
        pb = _pad16(p).astype(BF16)
        pv = _dot_nt(pb[:, 0:page], v_refs[0][...].astype(BF16))
        for g in range(1, n_pg):
            pv = pv + _dot_nt(pb[:, g * page:(g + 1) * page], v_refs[g][...].astype(BF16))
        return pv[0:8, :]

    update(jnp.concatenate(s_parts, axis=-1), pv_pages)

    @pl.when(p_i == pl.num_programs(1) - 1)
    def _():
        lane128 = lax.broadcasted_iota(jnp.int32, (8, 128), 1)
        row128 = lax.broadcasted_iota(jnp.int32, (8, 128), 0)
        lf_col = jnp.sum(jnp.where(lane128 == row128, jnp.broadcast_to(lfnew_ref[0:1, :], (8, 128)), 0.0),
                         axis=-1, keepdims=True)
        k_new = knew_ref[0:1, :].astype(BF16).astype(F32)
        v_new = vnew_ref[0:1, :].astype(BF16).astype(F32)
        s_new = jnp.sum(qm[0:8, :].astype(F32) * k_new, axis=-1, keepdims=True) - (car_ref[...] + lf_col)
        update(s_new, lambda p: p.astype(BF16).astype(F32) * v_new)
        o_ref[...] = jnp.sum(jnp.where(own, acc_ref[...] / l_ref[...], 0.0), axis=0, keepdims=True)


def _fox_decode(layer, page_table, qn, kn, proj, logf, cache_kt, cache_vt, cache_lft, nb):
    n_pages = page_table.shape[1]
    n_pg = min(PAGES_PER_STEP, n_pages)
    sr = SAMPLE_ROWS
    page = cache_kt.shape[3]
    tok = lambda cidx: (lambda b, p, pt: (b, cidx))
    hbm = pl.BlockSpec(memory_space=pl.ANY)
    in_specs = [pl.BlockSpec((sr, W_HEAD), tok(0)), pl.BlockSpec((sr, W_HEAD), tok(0)),
                pl.BlockSpec((sr, W_HEAD), tok(COL_FV)), pl.BlockSpec((sr, 128), tok(0)),
                hbm, hbm, hbm]
    grid_spec = pltpu.PrefetchScalarGridSpec(
        num_scalar_prefetch=1,
        grid=(nb, n_pages // n_pg),
        in_specs=in_specs,
        out_specs=pl.BlockSpec((None, 1, W_HEAD), lambda b, p, pt: (b, 0, 0)),
        scratch_shapes=[pltpu.VMEM((2, n_pg, W_HEAD, page), F32), pltpu.VMEM((2, n_pg, W_HEAD, page), F32),
                        pltpu.VMEM((2, n_pg, 8, page), F32), pltpu.SemaphoreType.DMA((3, 2)),
                        pltpu.VMEM((8, 1), F32), pltpu.VMEM((8, 1), F32), pltpu.VMEM((8, W_HEAD), F32),
                        pltpu.VMEM((8, 1), F32), pltpu.VMEM((16, W_HEAD), BF16)])
    return pl.pallas_call(
        functools.partial(_decode_body, n_pg, layer),
        grid_spec=grid_spec,
        out_shape=jax.ShapeDtypeStruct((nb, 1, W_HEAD), F32),
        compiler_params=_params("arbitrary", "arbitrary"),
        name="fox_decode",
    )(page_table, qn, kn, proj, logf, cache_kt, cache_vt, cache_lft)


def _block_diag(w):
    n, d, e = w.shape
    eye = jnp.eye(n, dtype=w.dtype)
    return (eye[:, None, :, None] * w[:, :, None, :]).reshape(n * d, n * e)


def _mixer_common(proj, nb, l, l_real, pos0, s0, conv0, h0, lw):
    n_seq = 4 if (l <= RET_CHUNK and nb % 4 == 0) else 1
    y_ret, s_new = _retention(proj, lw["ret_gn"], s0, nb, l, l_real, pos0, n_seq=n_seq)
    y_lru, h_last, conv_new = _lru(proj, lw["conv_w"], lw["conv_b"], lw["wr"], lw["wi"], lw["br"], lw["bi"],
                                   lw["lam"], conv0, h0, nb, l, l_real)
    return y_ret, y_lru, s_new, h_last, conv_new


def kernel(x_prompt, x_sample, cache_k, cache_v, cache_logf, state_ret, state_lru, state_conv, page_table, norm_ffn1, ffn1_gate, ffn1_up, ffn1_down, norm_mix, w_in, ret_gn, conv_w, conv_b, lru_wr, lru_br, lru_wi, lru_bi, lru_lambda, fox_qn, fox_kn, fox_bf, w_out, norm_ffn2, ffn2_gate, ffn2_up, ffn2_down):
    bp, lp, d = x_prompt.shape
    bs, ls, _ = x_sample.shape
    assert ls == 1
    depth = w_in.shape[0]
    n_pool, page = cache_k.shape[1], cache_k.shape[2]
    past_len = page_table.shape[1] * page
    w_lru = conv_w.shape[2]
    sr = SAMPLE_ROWS
    in_width = w_in.shape[2]

    cast = lambda w: [w[l].astype(BF16) for l in range(depth)]
    g1, u1, d1 = cast(ffn1_gate), cast(ffn1_up), cast(ffn1_down)
    g2, u2, d2 = cast(ffn2_gate), cast(ffn2_up), cast(ffn2_down)
    w_out_b = cast(w_out)
    w_in_b = [jnp.pad(w_in[l], ((0, 0), (0, PROJ_W - in_width))).astype(BF16) for l in range(depth)]
    cache_k4 = jnp.transpose(cache_k, (0, 1, 3, 4, 2)).reshape(depth, n_pool, W_HEAD, page)
    cache_v4 = jnp.transpose(cache_v, (0, 1, 3, 4, 2)).reshape(depth, n_pool, W_HEAD, page)
    cache_lft = jnp.pad(jnp.swapaxes(cache_logf, 2, 3), ((0, 0), (0, 0), (0, 8 - N_HEADS), (0, 0)))

    xp = x_prompt.reshape(bp * lp, d)
    xs = x_sample.reshape(bs, d)
    zero_s = jnp.zeros((bp, W_HEAD, HEAD_DIM), F32)
    zero_c = jnp.zeros((bp, 8, w_lru), F32)
    zero_h = jnp.zeros((bp, 1, w_lru), F32)

    outs_p = [[] for _ in range(6)]
    outs_s = [[] for _ in range(6)]
    for l in range(depth):
        lw = dict(ret_gn=ret_gn[l], conv_w=conv_w[l], conv_b=conv_b[l],
                  wr=_block_diag(lru_wr[l]).astype(BF16), wi=_block_diag(lru_wi[l]).astype(BF16),
                  br=lru_br[l], bi=lru_bi[l], lam=lru_lambda[l])

        xp = _ffn(xp, norm_ffn1[l], g1[l], u1[l], d1[l])
        proj = _inproj(xp, norm_mix[l], w_in_b[l])
        kn_t, v_t, logf_t, qa, ka, va = _fox_prep(proj, fox_qn[l], fox_kn[l], fox_bf[l], bp, lp, True)
        y_fox = _fox_attn(qa, ka, va)
        y_ret, y_lru, s_new, h_last, conv_new = _mixer_common(proj, bp, lp, lp, 0, zero_s, zero_c, zero_h, lw)
        xp = _ffn(xp, norm_ffn2[l], g2[l], u2[l], d2[l], mix=(y_ret, y_lru, y_fox, w_out_b[l]))
        heads_last = lambda a: jnp.transpose(a.reshape(bp, N_HEADS, HEAD_DIM, lp), (0, 3, 1, 2))
        outs_p[0].append(heads_last(kn_t))
        outs_p[1].append(heads_last(v_t))
        outs_p[2].append(jnp.transpose(logf_t[:, :N_HEADS, :], (0, 2, 1)))
        outs_p[3].append(s_new.reshape(bp, N_HEADS, HEAD_DIM, HEAD_DIM))
        outs_p[4].append(h_last.reshape(bp, w_lru))
        outs_p[5].append(conv_new[:, 8 - (CONV_W - 1):, :])

        xs = _ffn(xs, norm_ffn1[l], g1[l], u1[l], d1[l])
        xs_pad = jnp.pad(xs[:, None, :], ((0, 0), (0, sr - 1), (0, 0))).reshape(bs * sr, d)
        proj_s = _inproj(xs_pad, norm_mix[l], w_in_b[l])
        kn_s, logf_s, qn_s = _fox_prep(proj_s, fox_qn[l], fox_kn[l], fox_bf[l], 1, bs * sr, False)
        y_fox_s = _fox_decode(l, page_table, qn_s, kn_s, proj_s, logf_s, cache_k4, cache_v4, cache_lft, bs)
        s0 = state_ret[l].reshape(bs, W_HEAD, HEAD_DIM)
        conv0 = jnp.pad(state_conv[l], ((0, 0), (8 - (CONV_W - 1), 0), (0, 0)))
        h0 = state_lru[l].reshape(bs, 1, w_lru)
        y_ret_s, y_lru_s, s_new_s, h_last_s, conv_new_s = _mixer_common(
            proj_s, bs, sr, 1, past_len, s0, conv0, h0, lw)
        first = lambda a: a.reshape(bs, sr, a.shape[-1])[:, 0, :]
        xs = _ffn(xs, norm_ffn2[l], g2[l], u2[l], d2[l],
                  mix=(first(y_ret_s), first(y_lru_s), y_fox_s.reshape(bs, W_HEAD).astype(BF16), w_out_b[l]))
        outs_s[0].append(first(kn_s).reshape(bs, 1, N_HEADS, HEAD_DIM))
        outs_s[1].append(first(proj_s)[:, COL_FV * W_HEAD:(COL_FV + 1) * W_HEAD].reshape(bs, 1, N_HEADS, HEAD_DIM))
        outs_s[2].append(first(logf_s)[:, :N_HEADS].reshape(bs, 1, N_HEADS))
        outs_s[3].append(s_new_s.reshape(bs, N_HEADS, HEAD_DIM, HEAD_DIM))
        outs_s[4].append(h_last_s.reshape(bs, w_lru))
        outs_s[5].append(conv_new_s[:, 8 - (CONV_W - 1):, :])

    stk = lambda lst: jnp.stack(lst, axis=0)
    return (xp.reshape(bp, lp, d), xs.reshape(bs, 1, d),
            *[stk(o) for o in outs_p], *[stk(o) for o in outs_s])
```

```python
import functools

import numpy as np
import jax
import jax.numpy as jnp
from jax import lax
from jax.experimental import pallas as pl
from jax.experimental.pallas import tpu as pltpu

F32 = jnp.float32
BF16 = jnp.bfloat16

HEAD_DIM = 64
N_HEADS = 4
W_HEAD = N_HEADS * HEAD_DIM
CONV_W = 4
LRU_C = 8.0
RET_CHUNK = 256
ROPE_BASE = 10000.0
EPS = 1e-6
NEG_INF = -1e30
FOX_SCALE = HEAD_DIM ** -0.5
LOG2E = 1.4426950408889634
V_AUG_ROWS = 80
SAMPLE_ROWS = 16
PAGES_PER_STEP = 32
VMEM_LIMIT = 48 * 1024 * 1024

COL_RQ, COL_RK, COL_RV, COL_RG = 0, 1, 2, 3
COL_LX, COL_LG = 2, 3
COL_FQ, COL_FK, COL_FV = 8, 9, 10
COL_FF = 22
PROJ_W = 23 * 128


def _dot(a, b):
    return jnp.dot(a, b, preferred_element_type=F32)


def _dot_nt(a, b):
    return lax.dot_general(a, b, (((1,), (1,)), ((), ())), preferred_element_type=F32)


def _dot_tn(a, b):
    return lax.dot_general(a, b, (((0,), (0,)), ((), ())), preferred_element_type=F32)


def _sigmoid(x):
    return 0.5 * jnp.tanh(0.5 * x) + 0.5


def _split2(x):
    hi = x.astype(BF16)
    lo = (x - hi.astype(F32)).astype(BF16)
    return hi, lo


def _split3(x):
    hi = x.astype(BF16)
    r = x - hi.astype(F32)
    mid = r.astype(BF16)
    lo = (r - mid.astype(F32)).astype(BF16)
    return hi, mid, lo


def _params(*sem):
    return pltpu.CompilerParams(dimension_semantics=sem, vmem_limit_bytes=VMEM_LIMIT)


def _ffn_body(has_mix, tf, *refs):
    if has_mix:
        (x_ref, yr_ref, yl_ref, yf_ref, wo_ref, g_ref, wg_ref, wu_ref, wd_ref,
         o_ref, xres_ref, xn_ref, hid_ref) = refs
    else:
        x_ref, g_ref, wg_ref, wu_ref, wd_ref, o_ref, xres_ref, xn_ref, hid_ref = refs
    x = x_ref[...]
    if has_mix:
        w1 = yr_ref.shape[1]
        w2 = w1 + yl_ref.shape[1]
        x = (x + _dot(yr_ref[...], wo_ref[0:w1, :]) + _dot(yl_ref[...], wo_ref[w1:w2, :])
             + _dot(yf_ref[...], wo_ref[w2:, :]))
    xres_ref[...] = x
    ms = jnp.mean(x * x, axis=-1, keepdims=True)
    xn_ref[...] = (x * lax.rsqrt(ms + EPS) * g_ref[...]).astype(BF16)

    xn = xn_ref[...]
    for j in range(wg_ref.shape[1] // tf):
        gate = _dot(xn, wg_ref[:, j * tf:(j + 1) * tf])
        up = _dot(xn, wu_ref[:, j * tf:(j + 1) * tf])
        hid_ref[:, j * tf:(j + 1) * tf] = (gate * _sigmoid(gate) * up).astype(BF16)
    o_ref[...] = xres_ref[...] + 0.5 * _dot(hid_ref[...], wd_ref[...])


def _ffn(x, g, wg, wu, wd, mix=None, tm=512, tf=256):
    m, d = x.shape
    f = wg.shape[1]
    tm = min(tm, m)
    row = lambda i: (i, 0)
    resident = lambda a: pl.BlockSpec(a.shape, lambda i: (0, 0), pipeline_mode=pl.Buffered(1))
    in_specs = [pl.BlockSpec((tm, d), row)]
    args = [x]
    if mix is not None:
        yr, yl, yf, wo = mix
        in_specs += [pl.BlockSpec((tm, yr.shape[1]), row), pl.BlockSpec((tm, yl.shape[1]), row),
                     pl.BlockSpec((tm, yf.shape[1]), row), resident(wo)]
        args += [yr, yl, yf, wo]
    g2 = g.reshape(1, d)
    in_specs += [resident(g2), resident(wg), resident(wu), resident(wd)]
    args += [g2, wg, wu, wd]
    return pl.pallas_call(
        functools.partial(_ffn_body, mix is not None, tf),
        grid=(m // tm,),
        in_specs=in_specs,
        out_specs=pl.BlockSpec((tm, d), row),
        out_shape=jax.ShapeDtypeStruct((m, d), F32),
        scratch_shapes=[pltpu.VMEM((tm, d), F32), pltpu.VMEM((tm, d), BF16), pltpu.VMEM((tm, f), BF16)],
        compiler_params=_params("parallel"),
        name="ffn_mix" if mix is not None else "ffn",
    )(*args)


def _inproj_body(x_ref, g_ref, w_ref, o_ref):
    x = x_ref[...]
    ms = jnp.mean(x * x, axis=-1, keepdims=True)
    xn = (x * lax.rsqrt(ms + EPS) * g_ref[...]).astype(BF16)
    o_ref[...] = _dot(xn, w_ref[...])


def _inproj(x, g, w, tm=512):
    m, d = x.shape
    n = w.shape[1]
    tm = min(tm, m)
    return pl.pallas_call(
        _inproj_body,
        grid=(m // tm,),
        in_specs=[pl.BlockSpec((tm, d), lambda i: (i, 0)),
                  pl.BlockSpec((1, d), lambda i: (0, 0)),
                  pl.BlockSpec((d, n), lambda i: (0, 0))],
        out_specs=pl.BlockSpec((tm, n), lambda i: (i, 0)),
        out_shape=jax.ShapeDtypeStruct((m, n), F32),
        compiler_params=_params("parallel"),
        name="inproj",
    )(x, g.reshape(1, d), w)


def _head_norm(x, g):
    outs = []
    for h in range(N_HEADS):
        xh = x[:, h * HEAD_DIM:(h + 1) * HEAD_DIM]
        ms = jnp.mean(xh * xh, axis=-1, keepdims=True)
        outs.append(xh * lax.rsqrt(ms + EPS) * g)
    return outs


def _pad16(x):
    return jnp.concatenate([x, jnp.zeros_like(x)], axis=0)


def _fox_prep_sample_body(tm, fq_ref, fk_ref, fv_ref, ff_ref, gq_ref, gk_ref, bf_ref,
                          kn_ref, logf_ref, qn_ref):
    del fv_ref
    lane = lax.broadcasted_iota(jnp.int32, (tm, 128), 1)
    z = ff_ref[...] + bf_ref[...]
    logf = jnp.minimum(z, 0.0) - jnp.log1p(jnp.exp(-jnp.abs(z)))
    logf_ref[...] = jnp.where(lane < N_HEADS, logf, 0.0)
    kn_ref[...] = jnp.concatenate(_head_norm(fk_ref[...], gk_ref[...]), axis=-1)
    qn_ref[...] = jnp.concatenate(_head_norm(fq_ref[...], gq_ref[...]), axis=-1)


def _fox_prep_prompt_body(tm, fq_ref, fk_ref, fv_ref, ff_ref, gq_ref, gk_ref, bf_ref,
                          knt_ref, vt_ref, lft_ref, qa_ref, ka_ref, va_ref, carry_ref):
    @pl.when(pl.program_id(1) == 0)
    def _():
        carry_ref[...] = jnp.zeros_like(carry_ref)

    q_t = fq_ref[...].T
    k_t = fk_ref[...].T
    v_t = fv_ref[...].T
    z = ff_ref[...].T[0:8, :] + bf_ref[...]
    lf = jnp.minimum(z, 0.0) - jnp.log1p(jnp.exp(-jnp.abs(z)))
    lft_ref[...] = lf
    vt_ref[...] = v_t

    r_i = lax.broadcasted_iota(jnp.int32, (tm, tm), 0)
    c_i = lax.broadcasted_iota(jnp.int32, (tm, tm), 1)
    upper = jnp.where(r_i <= c_i, 1.0, 0.0).astype(BF16)
    hi, mid, lo = _split3(_pad16(lf))
    cs = (_dot(hi, upper) + _dot(mid, upper) + _dot(lo, upper))[0:8, :] + carry_ref[...]
    carry_ref[...] = cs[:, tm - 1:tm]
    cs = cs * LOG2E

    sub = lax.broadcasted_iota(jnp.int32, (HEAD_DIM, tm), 0)
    v_tail = jnp.where(lax.broadcasted_iota(jnp.int32, (V_AUG_ROWS - HEAD_DIM, tm), 0) == 0, 1.0, 0.0)
    kn_parts = []
    for h in range(N_HEADS):
        rows = slice(h * HEAD_DIM, (h + 1) * HEAD_DIM)
        qh, kh = q_t[rows], k_t[rows]
        qn = qh * lax.rsqrt(jnp.mean(qh * qh, axis=0, keepdims=True) + EPS) * gq_ref[...]
        kn = kh * lax.rsqrt(jnp.mean(kh * kh, axis=0, keepdims=True) + EPS) * gk_ref[...]
        kn_parts.append(kn)
        ch = cs[h:h + 1, :]
        chi = ch.astype(BF16).astype(F32)
        r1 = ch - chi
        cmid = r1.astype(BF16).astype(F32)
        clo = r1 - cmid
        q_tail = jnp.where(sub == 0, chi, jnp.where(sub == 1, cmid, jnp.where(
            sub == 2, clo, jnp.where(sub < 6, 1.0, 0.0))))
        k_tail = jnp.where(sub < 3, 1.0, jnp.where(sub == 3, -chi, jnp.where(
            sub == 4, -cmid, jnp.where(sub == 5, -clo, 0.0))))
        qa_ref[h] = jnp.concatenate([qn * (FOX_SCALE * LOG2E), q_tail], axis=0).astype(BF16)
        ka_ref[h] = jnp.concatenate([kn, k_tail], axis=0).T.astype(BF16)
        va_ref[h] = jnp.concatenate([v_t[rows], v_tail], axis=0).astype(BF16)
    knt_ref[...] = jnp.concatenate(kn_parts, axis=0)


def _fox_prep(proj, gq, gk, bf, nb, l, prompt, tm=512):
    m = proj.shape[0]
    tm = min(tm, l)
    nl = l // tm
    rows = lambda b, i: b * nl + i
    const2 = lambda b, i: (0, 0)
    in_specs = [pl.BlockSpec((tm, W_HEAD), lambda b, i: (rows(b, i), COL_FQ)),
                pl.BlockSpec((tm, W_HEAD), lambda b, i: (rows(b, i), COL_FK)),
                pl.BlockSpec((tm, W_HEAD), lambda b, i: (rows(b, i), COL_FV)),
                pl.BlockSpec((tm, 128), lambda b, i: (rows(b, i), COL_FF))]
    if prompt:
        in_specs += [pl.BlockSpec((HEAD_DIM, 1), const2), pl.BlockSpec((HEAD_DIM, 1), const2),
                     pl.BlockSpec((8, 1), const2)]
        params = (gq.reshape(HEAD_DIM, 1), gk.reshape(HEAD_DIM, 1),
                  jnp.zeros((8, 1), F32).at[:N_HEADS, 0].set(bf))
        pos_minor = lambda r: pl.BlockSpec((None, r, tm), lambda b, i: (b, 0, i))
        aug = pl.BlockSpec((None, N_HEADS, tm, 128), lambda b, i: (b, 0, i, 0))
        aug_t = pl.BlockSpec((None, N_HEADS, 128, tm), lambda b, i: (b, 0, 0, i))
        aug_v = pl.BlockSpec((None, N_HEADS, V_AUG_ROWS, tm), lambda b, i: (b, 0, 0, i))
        out_specs = [pos_minor(W_HEAD), pos_minor(W_HEAD), pos_minor(8), aug_t, aug, aug_v]
        out_shape = [jax.ShapeDtypeStruct((nb, W_HEAD, l), F32), jax.ShapeDtypeStruct((nb, W_HEAD, l), F32),
                     jax.ShapeDtypeStruct((nb, 8, l), F32),
                     jax.ShapeDtypeStruct((nb, N_HEADS, 128, l), BF16),
                     jax.ShapeDtypeStruct((nb, N_HEADS, l, 128), BF16),
                     jax.ShapeDtypeStruct((nb, N_HEADS, V_AUG_ROWS, l), BF16)]
        scratch = [pltpu.VMEM((8, 1), F32)]
        body = functools.partial(_fox_prep_prompt_body, tm)
    else:
        in_specs += [pl.BlockSpec((1, HEAD_DIM), const2), pl.BlockSpec((1, HEAD_DIM), const2),
                     pl.BlockSpec((1, 128), const2)]
        params = (gq.reshape(1, HEAD_DIM), gk.reshape(1, HEAD_DIM),
                  jnp.zeros((1, 128), F32).at[0, :N_HEADS].set(bf))
        row_spec = lambda w: pl.BlockSpec((tm, w), lambda b, i: (rows(b, i), 0))
        out_specs = [row_spec(W_HEAD), row_spec(128), row_spec(W_HEAD)]
        out_shape = [jax.ShapeDtypeStruct((m, W_HEAD), F32), jax.ShapeDtypeStruct((m, 128), F32),
                     jax.ShapeDtypeStruct((m, W_HEAD), F32)]
        scratch = []
        body = functools.partial(_fox_prep_sample_body, tm)
    return pl.pallas_call(
        body,
        grid=(nb, nl),
        in_specs=in_specs,
        out_specs=out_specs,
        out_shape=out_shape,
        scratch_shapes=scratch,
        compiler_params=_params("arbitrary", "arbitrary"),
        name="fox_prep_prompt" if prompt else "fox_prep_sample",
    )(proj, proj, proj, proj, *params)


def _fox_attn_body(tq, tk, qt_ref, ka_ref, vt_ref, o_ref, s_ref, m_ref, acc_ref):
    qi = pl.program_id(2)
    n_diag = tq // tk
    heads = range(2)
    m_ref[...] = jnp.full_like(m_ref, NEG_INF)
    acc_ref[...] = jnp.zeros_like(acc_ref)

    def scores(slot, j):
        k0 = pl.multiple_of(j * tk, tk)
        for h in heads:
            s_ref[slot, h] = _dot(ka_ref[h, pl.ds(k0, tk), :], qt_ref[h])

    def absorb(h, s, v_t, lanes):
        m_old = m_ref[h, :, lanes]
        m_new = jnp.maximum(m_old, jnp.max(s, axis=0, keepdims=True))
        alpha = jnp.exp2(m_old - m_new)
        p = jnp.exp2(s - m_new)
        acc_ref[h, :, lanes] = alpha * acc_ref[h, :, lanes] + _dot(v_t, p.astype(BF16))
        m_ref[h, :, lanes] = m_new

    def consume(slot, j):
        k0 = pl.multiple_of(j * tk, tk)
        for h in heads:
            absorb(h, s_ref[slot, h], vt_ref[h, :, pl.ds(k0, tk)], slice(None))

    scores(0, 0)

    def pair(jj):
        scores(1, 2 * jj + 1)
        consume(0, 2 * jj)
        scores(0, 2 * jj + 2)
        consume(1, 2 * jj + 1)

    def four_pairs(t, carry):
        for u in range(4):
            pair(4 * t + u)
        return carry

    n_pairs = qi * (n_diag // 2)
    lax.fori_loop(0, n_pairs // 4, four_pairs, 0)
    done4 = (n_pairs // 4) * 4

    @pl.when(n_pairs % 4 >= 2)
    def _():
        pair(done4)
        pair(done4 + 1)

    @pl.when(n_pairs % 2 == 1)
    def _():
        pair(n_pairs - 1)

    def key_start(d):
        return pl.multiple_of((qi * n_diag + d) * tk, tk)

    def causal(n):
        return lax.broadcasted_iota(jnp.int32, (tk, n), 0) <= lax.broadcasted_iota(jnp.int32, (tk, n), 1)

    late = {d: [_dot(ka_ref[h, pl.ds(key_start(d), tk), :], qt_ref[h, :, d * tk:tq]) for h in heads]
            for d in range(1, n_diag)}
    for d in range(n_diag):
        for h in heads:
            s = s_ref[0, h] if d == 0 else late[d][h]
            absorb(h, jnp.where(causal(tq - d * tk), s, NEG_INF), vt_ref[h, :, pl.ds(key_start(d), tk)],
                   slice(d * tk, tq))

    outs = []
    for h in heads:
        acc = acc_ref[h]
        outs.append(acc[0:HEAD_DIM, :] / acc[HEAD_DIM:HEAD_DIM + 1, :])
    o_ref[...] = jnp.concatenate(outs, axis=0).T.astype(BF16)


def _fox_attn(qt, ka, vt, tq=512):
    nb, nh, l, _ = ka.shape
    tq = min(tq, l)
    tk = tq // 2
    nq = l // tq
    return pl.pallas_call(
        functools.partial(_fox_attn_body, tq, tk),
        grid=(nb, nh // 2, nq),
        in_specs=[pl.BlockSpec((None, 2, 128, tq), lambda b, hp, i: (b, hp, 0, i)),
                  pl.BlockSpec((None, 2, l, 128), lambda b, hp, i: (b, hp, 0, 0)),
                  pl.BlockSpec((None, 2, V_AUG_ROWS, l), lambda b, hp, i: (b, hp, 0, 0))],
        out_specs=pl.BlockSpec((tq, 128), lambda b, hp, i: (b * nq + i, hp)),
        out_shape=jax.ShapeDtypeStruct((nb * l, W_HEAD), BF16),
        scratch_shapes=[pltpu.VMEM((2, 2, tk, tq), F32), pltpu.VMEM((2, 1, tq), F32),
                        pltpu.VMEM((2, V_AUG_ROWS, tq), F32)],
        compiler_params=_params("parallel", "parallel", "arbitrary"),
        name="fox_attn",
    )(qt, ka, vt)


def _ret_tables(pos0, l_pad, l_real, chunk):
    c_real = chunk if l_real % chunk == 0 else l_real
    half = HEAD_DIM // 2
    inv = ROPE_BASE ** (-np.arange(half, dtype=np.float64) / half)
    inv_l = np.tile(inv, 2 * N_HEADS)
    sgn = np.tile(np.concatenate([-np.ones(half), np.ones(half)]), N_HEADS)
    n_a = l_pad // chunk
    ang_a = (pos0 + chunk * np.arange(n_a))[:, None] * inv_l[None, :]
    ang_b = np.arange(chunk)[:, None] * inv_l[None, :]
    log_g = np.log1p(-np.exp2(-5.0 - np.arange(N_HEADS, dtype=np.float64)))
    idx = np.arange(chunk, dtype=np.float64)
    diff = idx[:, None] - idx[None, :]
    decay = np.where(diff >= 0, np.exp(np.maximum(diff, 0.0)[None] * log_g[:, None, None]), 0.0)
    lg_l = np.repeat(log_g, HEAD_DIM)
    xi = np.exp((idx[:, None] + 1.0) * lg_l[None, :])
    zeta = np.exp((c_real - 1.0 - idx[:, None]) * lg_l[None, :])
    head = np.arange(W_HEAD) // HEAD_DIM
    bd = (head[:, None] == head[None, :]).astype(np.float64)
    gmat = bd * np.exp(c_real * lg_l)[:, None]
    f = lambda a: jnp.asarray(a, dtype=F32)
    return dict(ca=f(np.cos(ang_a)), sa=f(sgn * np.sin(ang_a)), cb=f(np.cos(ang_b)), sb=f(sgn * np.sin(ang_b)),
                decay=f(decay), xi=f(xi), zeta=f(zeta), gmat=f(gmat), bd=f(bd),
                bdavg=jnp.asarray(bd / HEAD_DIM, dtype=BF16))


def _ret_body(tm, ch, n_seq, q_ref, k_ref, v_ref, g_ref, ca_ref, sa_ref, cb_ref, sb_ref, dec_ref, xi_ref,
              zeta_ref, gmat_ref, bd_ref, bdavg_ref, gn_ref, s0_ref, y_ref, sout_ref, st_ref):
    i = pl.program_id(1)
    n_chunk = tm // ch

    @pl.when(i == 0)
    def _():
        for s in range(n_seq):
            st_ref[s] = jnp.concatenate([s0_ref[s]] * N_HEADS, axis=-1) * bd_ref[...]

    lane = lax.broadcasted_iota(jnp.int32, (ch, W_HEAD), 1)
    lane128 = lax.broadcasted_iota(jnp.int32, (ch, 128), 1)
    first_half = (lane128 % HEAD_DIM) < (HEAD_DIM // 2)

    def swap_halves(x):
        parts = []
        for blk in range(W_HEAD // 128):
            xb = x[:, blk * 128:(blk + 1) * 128]
            parts.append(jnp.where(first_half, pltpu.roll(xb, 128 - HEAD_DIM // 2, 1),
                                   pltpu.roll(xb, HEAD_DIM // 2, 1)))
        return jnp.concatenate(parts, axis=-1)

    def gmean(t):
        hi, lo = _split2(t)
        return _dot(hi, bdavg_ref[...]) + _dot(lo, bdavg_ref[...])

    def chunk(s, c):
        r0 = s * tm + c * ch
        a = i * n_chunk + c
        c_a = ca_ref[pl.ds(a, 1), :]
        s_a = sa_ref[pl.ds(a, 1), :]
        cos = c_a * cb_ref[...] - s_a * sb_ref[...]
        sin = s_a * cb_ref[...] + c_a * sb_ref[...]
        q = q_ref[pl.ds(r0, ch), :]
        k = k_ref[pl.ds(r0, ch), :]
        q = q * cos + swap_halves(q) * sin
        k = (k * cos + swap_halves(k) * sin) * FOX_SCALE
        qb = q.astype(BF16)
        kb = k.astype(BF16)
        vb = v_ref[pl.ds(r0, ch), :].astype(BF16)
        st = st_ref[s]
        o = _dot(qb, st.astype(BF16)) * xi_ref[...]
        for h in range(N_HEADS):
            mh = (lane // HEAD_DIM) == h
            qm = jnp.where(mh, q, 0.0).astype(BF16)
            att = _dot_nt(qm, kb) * dec_ref[h]
            o = o + jnp.where(mh, _dot(att.astype(BF16), vb), 0.0)
        kz = (k * zeta_ref[...]).astype(BF16)
        st_ref[s] = st * gmat_ref[...] + _dot_tn(kz, vb) * bd_ref[...]
        mu = gmean(o)
        d = o - mu
        var = gmean(d * d)
        on = d * lax.rsqrt(var + EPS) * gn_ref[...]
        gate = g_ref[pl.ds(r0, ch), :]
        y_ref[pl.ds(r0, ch), :] = (on * (gate * _sigmoid(gate))).astype(BF16)

    for c in range(n_chunk):
        for s in range(n_seq):
            chunk(s, c)

    @pl.when(i == pl.num_programs(1) - 1)
    def _():
        for s in range(n_seq):
            st = st_ref[s]
            acc = st[:, 0:HEAD_DIM]
            for h in range(1, N_HEADS):
                acc = acc + st[:, h * HEAD_DIM:(h + 1) * HEAD_DIM]
            sout_ref[s] = acc


def _retention(proj, gn, s0, nb, l, l_real, pos0, tm=512, n_seq=1):
    m = proj.shape[0]
    tm = min(tm, l)
    ch = min(RET_CHUNK, tm)
    nl = l // tm
    assert n_seq == 1 or (nl == 1 and nb % n_seq == 0)
    t = _ret_tables(pos0, l, l_real, ch)
    rows = lambda b, i: b * nl + i
    const2 = lambda b, i: (0, 0)
    col = lambda cidx: pl.BlockSpec((n_seq * tm, W_HEAD), lambda b, i: (rows(b, i), cidx))
    full = lambda a: pl.BlockSpec(a.shape, (lambda b, i: (0,) * a.ndim))
    in_specs = [col(COL_RQ), col(COL_RK), col(COL_RV), col(COL_RG),
                full(t["ca"]), full(t["sa"]), full(t["cb"]), full(t["sb"]), full(t["decay"]),
                full(t["xi"]), full(t["zeta"]), full(t["gmat"]), full(t["bd"]), full(t["bdavg"]),
                pl.BlockSpec((1, W_HEAD), const2),
                pl.BlockSpec((n_seq, W_HEAD, HEAD_DIM), lambda b, i: (b, 0, 0))]
    return pl.pallas_call(
        functools.partial(_ret_body, tm, ch, n_seq),
        grid=(nb // n_seq, nl),
        in_specs=in_specs,
        out_specs=[pl.BlockSpec((n_seq * tm, W_HEAD), lambda b, i: (rows(b, i), 0)),
                   pl.BlockSpec((n_seq, W_HEAD, HEAD_DIM), lambda b, i: (b, 0, 0))],
        out_shape=[jax.ShapeDtypeStruct((m, W_HEAD), BF16),
                   jax.ShapeDtypeStruct((nb, W_HEAD, HEAD_DIM), F32)],
        scratch_shapes=[pltpu.VMEM((n_seq, W_HEAD, W_HEAD), F32)],
        compiler_params=_params("arbitrary", "arbitrary"),
        name="retention",
    )(proj, proj, proj, proj, t["ca"], t["sa"], t["cb"], t["sb"], t["decay"], t["xi"], t["zeta"],
      t["gmat"], t["bd"], t["bdavg"], gn.reshape(1, W_HEAD), s0)


def _lru_body(tm, l_real, lx_ref, lg_ref, cw_ref, cb_ref, wr_ref, wi_ref, br_ref, bi_ref, lam_ref,
              conv0_ref, h0_ref, y_ref, hlast_ref, convnew_ref, xpad_ref, a_ref, b_ref, h_ref, hcar_ref):
    i = pl.program_id(1)
    w = lx_ref.shape[1]
    t_last, r_last = (l_real - 1) // tm, (l_real - 1) % tm

    @pl.when(i == 0)
    def _():
        xpad_ref[0:8, :] = conv0_ref[...]
        hcar_ref[...] = h0_ref[...]

    x = lx_ref[...]
    xpad_ref[8:8 + tm, :] = x
    xc = cb_ref[...] + cw_ref[CONV_W - 1:CONV_W, :] * x
    for j in range(1, CONV_W):
        xc = xc + cw_ref[CONV_W - 1 - j:CONV_W - j, :] * xpad_ref[pl.ds(8 - j, tm), :]

    xb = xc.astype(BF16)
    r = _sigmoid(_dot(xb, wr_ref[...]) + br_ref[...])
    ig = _sigmoid(_dot(xb, wi_ref[...]) + bi_ref[...])
    z = -lam_ref[...]
    softplus = jnp.maximum(z, 0.0) + jnp.log1p(jnp.exp(-jnp.abs(z)))
    log_a = (-LRU_C) * r * softplus
    a = jnp.exp(log_a)
    th = jnp.tanh(log_a)
    em = -2.0 * th / (1.0 - th)
    u = jnp.where(em > 0.0, em * lax.rsqrt(em), 0.0) * (ig * xc)

    a = a.reshape(tm // 8, 8, w)
    u = u.reshape(tm // 8, 8, w)
    rowmod = lax.broadcasted_iota(jnp.int32, (tm // 8, 8, w), 1)
    for sh in (1, 2, 4):
        a_sh = pltpu.roll(a, sh, 1)
        u_sh = pltpu.roll(u, sh, 1)
        valid = rowmod >= sh
        u = jnp.where(valid, a * u_sh + u, u)
        a = jnp.where(valid, a * a_sh, a)
    a_ref[...] = a.reshape(tm, w)
    b_ref[...] = u.reshape(tm, w)

    def group(gi, h):
        r0 = pl.multiple_of(gi * 8, 8)
        hg = a_ref[pl.ds(r0, 8), :] * h + b_ref[pl.ds(r0, 8), :]
        h_ref[pl.ds(r0, 8), :] = hg
        return hg[7:8, :]

    hcar_ref[...] = lax.fori_loop(0, tm // 8, group, hcar_ref[...])

    g = lg_ref[...]
    gelu = 0.5 * g * (1.0 + jnp.tanh(0.7978845608028654 * (g + 0.044715 * (g * g * g))))
    y_ref[...] = (h_ref[...] * gelu).astype(BF16)

    @pl.when(i == t_last)
    def _():
        hlast_ref[...] = h_ref[r_last:r_last + 1, :]
        convnew_ref[...] = xpad_ref[pl.ds(r_last + 1, 8), :]

    xpad_ref[0:8, :] = xpad_ref[tm:tm + 8, :]


def _lru(proj, cw, cb, wr, wi, br, bi, lam, conv0, h0, nb, l, l_real, tm=512):
    m = proj.shape[0]
    w = cw.shape[1]
    tm = min(tm, l)
    nl = l // tm
    rows = lambda b, i: b * nl + i
    const2 = lambda b, i: (0, 0)
    vec = pl.BlockSpec((1, w), const2)
    per_b = lambda r: pl.BlockSpec((None, r, w), lambda b, i: (b, 0, 0))
    return pl.pallas_call(
        functools.partial(_lru_body, tm, l_real),
        grid=(nb, nl),
        in_specs=[pl.BlockSpec((tm, w), lambda b, i: (rows(b, i), COL_LX)),
                  pl.BlockSpec((tm, w), lambda b, i: (rows(b, i), COL_LG)),
                  pl.BlockSpec((CONV_W, w), const2), vec,
                  pl.BlockSpec((w, w), const2), pl.BlockSpec((w, w), const2), vec, vec, vec,
                  per_b(8), per_b(1)],
        out_specs=[pl.BlockSpec((tm, w), lambda b, i: (rows(b, i), 0)), per_b(1), per_b(8)],
        out_shape=[jax.ShapeDtypeStruct((m, w), BF16), jax.ShapeDtypeStruct((nb, 1, w), F32),
                   jax.ShapeDtypeStruct((nb, 8, w), F32)],
        scratch_shapes=[pltpu.VMEM((tm + 8, w), F32), pltpu.VMEM((tm, w), F32), pltpu.VMEM((tm, w), F32),
                        pltpu.VMEM((tm, w), F32), pltpu.VMEM((1, w), F32)],
        compiler_params=_params("arbitrary", "arbitrary"),
        name="lru",
    )(proj, proj, cw, cb.reshape(1, w), wr, wi, br.reshape(1, w), bi.reshape(1, w), lam.reshape(1, w),
      conv0, h0)


def _decode_body(n_pg, layer, pt_ref, qn_ref, knew_ref, vnew_ref, lfnew_ref, kt_hbm, vt_hbm, lf_hbm,
                 o_ref, kbuf, vbuf, lfbuf, sem, m_ref, l_ref, acc_ref, car_ref, qm_ref):
    b_i = pl.program_id(0)
    p_i = pl.program_id(1)
    n_steps = pl.num_programs(1)
    t = b_i * n_steps + p_i
    slot = lax.rem(t, 2)
    page = kbuf.shape[3]

    def page_copies(pages, sl):
        cps = []
        for g in range(n_pg):
            pg = pages(g)
            cps.append(pltpu.make_async_copy(kt_hbm.at[layer, pg], kbuf.at[sl, g], sem.at[0, sl]))
            cps.append(pltpu.make_async_copy(vt_hbm.at[layer, pg], vbuf.at[sl, g], sem.at[1, sl]))
            cps.append(pltpu.make_async_copy(lf_hbm.at[layer, pg], lfbuf.at[sl, g], sem.at[2, sl]))
        return cps

    @pl.when(t == 0)
    def _():
        for cp in page_copies(lambda g: pt_ref[0, g], 0):
            cp.start()

    @pl.when(t + 1 < pl.num_programs(0) * n_steps)
    def _():
        wrap = p_i + 1 == n_steps
        b_n = jnp.where(wrap, b_i + 1, b_i)
        p_n = jnp.where(wrap, 0, p_i + 1)
        for cp in page_copies(lambda g: pt_ref[b_n, p_n * n_pg + g], 1 - slot):
            cp.start()

    for cp in page_copies(lambda g: 0, slot):
        cp.wait()
    k_refs = [kbuf.at[slot, g] for g in range(n_pg)]
    v_refs = [vbuf.at[slot, g] for g in range(n_pg)]
    lf_refs = [lfbuf.at[slot, g] for g in range(n_pg)]

    row = lax.broadcasted_iota(jnp.int32, (8, W_HEAD), 0)
    lane = lax.broadcasted_iota(jnp.int32, (8, W_HEAD), 1)
    own = (lane // HEAD_DIM) == row

    @pl.when(p_i == 0)
    def _():
        q = jnp.where(own, jnp.broadcast_to(qn_ref[0:1, :] * FOX_SCALE, (8, W_HEAD)), 0.0)
        qm_ref[...] = _pad16(q).astype(BF16)
        m_ref[...] = jnp.full_like(m_ref, NEG_INF)
        l_ref[...] = jnp.zeros_like(l_ref)
        acc_ref[...] = jnp.zeros_like(acc_ref)
        car_ref[...] = jnp.zeros_like(car_ref)

    qm = qm_ref[...]

    def update(s, pv_fn):
        m_old = m_ref[...]
        m_new = jnp.maximum(m_old, jnp.max(s, axis=-1, keepdims=True))
        alpha = jnp.exp(m_old - m_new)
        p = jnp.exp(s - m_new)
        l_ref[...] = alpha * l_ref[...] + jnp.sum(p, axis=-1, keepdims=True)
        acc_ref[...] = alpha * acc_ref[...] + pv_fn(p)
        m_ref[...] = m_new

    r_i = lax.broadcasted_iota(jnp.int32, (page, page), 0)
    c_i = lax.broadcasted_iota(jnp.int32, (page, page), 1)
    upper = jnp.where(r_i <= c_i, 1.0, 0.0).astype(BF16)
    hi, mid, lo = _split3(jnp.concatenate([lf_refs[g][...] for g in range(n_pg)], axis=0))
    c_all = _dot(hi, upper) + _dot(mid, upper) + _dot(lo, upper)
    nr = n_pg * 8
    rr = lax.broadcasted_iota(jnp.int32, (nr, nr), 0)
    cc = lax.broadcasted_iota(jnp.int32, (nr, nr), 1)
    before = jnp.where((rr % 8 == cc % 8) & (cc // 8 < rr // 8), 1.0, 0.0).astype(BF16)
    chi, cmid, clo = _split3(c_all)
    pref = _dot(before, chi) + _dot(before, cmid) + _dot(before, clo)
    c_all = c_all + pref[:, page - 1:page] + jnp.concatenate([car_ref[...]] * n_pg, axis=0)
    car_ref[...] = c_all[nr - 8:nr, page - 1:page]
    s_parts = [_dot(qm, k_refs[g][...].astype(BF16))[0:8, :] - c_all[g * 8:(g + 1) * 8, :]
               for g in range(n_pg)]

    def pv_pages(p):
        pb = _pad16(p).astype(BF16)
        pv = _dot_nt(pb[:, 0:page], v_refs[0][...].astype(BF16))
        for g in range(1, n_pg):
            pv = pv + _dot_nt(pb[:, g * page:(g + 1) * page], v_refs[g][...].astype(BF16))
        return pv[0:8, :]

    update(jnp.concatenate(s_parts, axis=-1), pv_pages)

    @pl.when(p_i == pl.num_programs(1) - 1)
    def _():
        lane128 = lax.broadcasted_iota(jnp.int32, (8, 128), 1)
        row128 = lax.broadcasted_iota(jnp.int32, (8, 128), 0)
        lf_col = jnp.sum(jnp.where(lane128 == row128, jnp.broadcast_to(lfnew_ref[0:1, :], (8, 128)), 0.0),
                         axis=-1, keepdims=True)
        k_new = knew_ref[0:1, :].astype(BF16).astype(F32)
        v_new = vnew_ref[0:1, :].astype(BF16).astype(F32)
        s_new = jnp.sum(qm[0:8, :].astype(F32) * k_new, axis=-1, keepdims=True) - (car_ref[...] + lf_col)
        update(s_new, lambda p: p.astype(BF16).astype(F32) * v_new)
        o_ref[...] = jnp.sum(jnp.where(own, acc_ref[...] / l_ref[...], 0.0), axis=0, keepdims=True)


def _fox_decode(layer, page_table, qn, kn, proj, logf, cache_kt, cache_vt, cache_lft, nb):
    n_pages = page_table.shape[1]
    n_pg = min(PAGES_PER_STEP, n_pages)
    sr = SAMPLE_ROWS
    page = cache_kt.shape[3]
    tok = lambda cidx: (lambda b, p, pt: (b, cidx))
    hbm = pl.BlockSpec(memory_space=pl.ANY)
    in_specs = [pl.BlockSpec((sr, W_HEAD), tok(0)), pl.BlockSpec((sr, W_HEAD), tok(0)),
                pl.BlockSpec((sr, W_HEAD), tok(COL_FV)), pl.BlockSpec((sr, 128), tok(0)),
                hbm, hbm, hbm]
    grid_spec = pltpu.PrefetchScalarGridSpec(
        num_scalar_prefetch=1,
        grid=(nb, n_pages // n_pg),
        in_specs=in_specs,
        out_specs=pl.BlockSpec((None, 1, W_HEAD), lambda b, p, pt: (b, 0, 0)),
        scratch_shapes=[pltpu.VMEM((2, n_pg, W_HEAD, page), F32), pltpu.VMEM((2, n_pg, W_HEAD, page), F32),
                        pltpu.VMEM((2, n_pg, 8, page), F32), pltpu.SemaphoreType.DMA((3, 2)),
                        pltpu.VMEM((8, 1), F32), pltpu.VMEM((8, 1), F32), pltpu.VMEM((8, W_HEAD), F32),
                        pltpu.VMEM((8, 1), F32), pltpu.VMEM((16, W_HEAD), BF16)])
    return pl.pallas_call(
        functools.partial(_decode_body, n_pg, layer),
        grid_spec=grid_spec,
        out_shape=jax.ShapeDtypeStruct((nb, 1, W_HEAD), F32),
        compiler_params=_params("arbitrary", "arbitrary"),
        name="fox_decode",
    )(page_table, qn, kn, proj, logf, cache_kt, cache_vt, cache_lft)


def _block_diag(w):
    n, d, e = w.shape
    eye = jnp.eye(n, dtype=w.dtype)
    return (eye[:, None, :, None] * w[:, :, None, :]).reshape(n * d, n * e)


def _mixer_common(proj, nb, l, l_real, pos0, s0, conv0, h0, lw):
    n_seq = 4 if (l <= RET_CHUNK and nb % 4 == 0) else 1
    y_ret, s_new = _retention(proj, lw["ret_gn"], s0, nb, l, l_real, pos0, n_seq=n_seq)
    y_lru, h_last, conv_new = _lru(proj, lw["conv_w"], lw["conv_b"], lw["wr"], lw["wi"], lw["br"], lw["bi"],
                                   lw["lam"], conv0, h0, nb, l, l_real)
    return y_ret, y_lru, s_new, h_last, conv_new


def kernel(x_prompt, x_sample, cache_k, cache_v, cache_logf, state_ret, state_lru, state_conv, page_table, norm_ffn1, ffn1_gate, ffn1_up, ffn1_down, norm_mix, w_in, ret_gn, conv_w, conv_b, lru_wr, lru_br, lru_wi, lru_bi, lru_lambda, fox_qn, fox_kn, fox_bf, w_out, norm_ffn2, ffn2_gate, ffn2_up, ffn2_down):
    bp, lp, d = x_prompt.shape
    bs, ls, _ = x_sample.shape
    assert ls == 1
    depth = w_in.shape[0]
    n_pool, page = cache_k.shape[1], cache_k.shape[2]
    past_len = page_table.shape[1] * page
    w_lru = conv_w.shape[2]
    sr = SAMPLE_ROWS
    in_width = w_in.shape[2]

    cast = lambda w: [w[l].astype(BF16) for l in range(depth)]
    g1, u1, d1 = cast(ffn1_gate), cast(ffn1_up), cast(ffn1_down)
    g2, u2, d2 = cast(ffn2_gate), cast(ffn2_up), cast(ffn2_down)
    w_out_b = cast(w_out)
    w_in_b = [jnp.pad(w_in[l], ((0, 0), (0, PROJ_W - in_width))).astype(BF16) for l in range(depth)]
    cache_k4 = jnp.transpose(cache_k, (0, 1, 3, 4, 2)).reshape(depth, n_pool, W_HEAD, page)
    cache_v4 = jnp.transpose(cache_v, (0, 1, 3, 4, 2)).reshape(depth, n_pool, W_HEAD, page)
    cache_lft = jnp.pad(jnp.swapaxes(cache_logf, 2, 3), ((0, 0), (0, 0), (0, 8 - N_HEADS), (0, 0)))

    xp = x_prompt.reshape(bp * lp, d)
    xs = x_sample.reshape(bs, d)
    zero_s = jnp.zeros((bp, W_HEAD, HEAD_DIM), F32)
    zero_c = jnp.zeros((bp, 8, w_lru), F32)
    zero_h = jnp.zeros((bp, 1, w_lru), F32)

    outs_p = [[] for _ in range(6)]
    outs_s = [[] for _ in range(6)]
    for l in range(depth):
        lw = dict(ret_gn=ret_gn[l], conv_w=conv_w[l], conv_b=conv_b[l],
                  wr=_block_diag(lru_wr[l]).astype(BF16), wi=_block_diag(lru_wi[l]).astype(BF16),
                  br=lru_br[l], bi=lru_bi[l], lam=lru_lambda[l])

        xp = _ffn(xp, norm_ffn1[l], g1[l], u1[l], d1[l])
        proj = _inproj(xp, norm_mix[l], w_in_b[l])
        kn_t, v_t, logf_t, qa, ka, va = _fox_prep(proj, fox_qn[l], fox_kn[l], fox_bf[l], bp, lp, True)
        y_fox = _fox_attn(qa, ka, va)
        y_ret, y_lru, s_new, h_last, conv_new = _mixer_common(proj, bp, lp, lp, 0, zero_s, zero_c, zero_h, lw)
        xp = _ffn(xp, norm_ffn2[l], g2[l], u2[l], d2[l], mix=(y_ret, y_lru, y_fox, w_out_b[l]))
        heads_last = lambda a: jnp.transpose(a.reshape(bp, N_HEADS, HEAD_DIM, lp), (0, 3, 1, 2))
        outs_p[0].append(heads_last(kn_t))
        outs_p[1].append(heads_last(v_t))
        outs_p[2].append(jnp.transpose(logf_t[:, :N_HEADS, :], (0, 2, 1)))
        outs_p[3].append(s_new.reshape(bp, N_HEADS, HEAD_DIM, HEAD_DIM))
        outs_p[4].append(h_last.reshape(bp, w_lru))
        outs_p[5].append(conv_new[:, 8 - (CONV_W - 1):, :])

        xs = _ffn(xs, norm_ffn1[l], g1[l], u1[l], d1[l])
        xs_pad = jnp.pad(xs[:, None, :], ((0, 0), (0, sr - 1), (0, 0))).reshape(bs * sr, d)
        proj_s = _inproj(xs_pad, norm_mix[l], w_in_b[l])
        kn_s, logf_s, qn_s = _fox_prep(proj_s, fox_qn[l], fox_kn[l], fox_bf[l], 1, bs * sr, False)
        y_fox_s = _fox_decode(l, page_table, qn_s, kn_s, proj_s, logf_s, cache_k4, cache_v4, cache_lft, bs)
        s0 = state_ret[l].reshape(bs, W_HEAD, HEAD_DIM)
        conv0 = jnp.pad(state_conv[l], ((0, 0), (8 - (CONV_W - 1), 0), (0, 0)))
        h0 = state_lru[l].reshape(bs, 1, w_lru)
        y_ret_s, y_lru_s, s_new_s, h_last_s, conv_new_s = _mixer_common(
            proj_s, bs, sr, 1, past_len, s0, conv0, h0, lw)
        first = lambda a: a.reshape(bs, sr, a.shape[-1])[:, 0, :]
        xs = _ffn(xs, norm_ffn2[l], g2[l], u2[l], d2[l],
                  mix=(first(y_ret_s), first(y_lru_s), y_fox_s.reshape(bs, W_HEAD).astype(BF16), w_out_b[l]))
        outs_s[0].append(first(kn_s).reshape(bs, 1, N_HEADS, HEAD_DIM))
        outs_s[1].append(first(proj_s)[:, COL_FV * W_HEAD:(COL_FV + 1) * W_HEAD].reshape(bs, 1, N_HEADS, HEAD_DIM))
        outs_s[2].append(first(logf_s)[:, :N_HEADS].reshape(bs, 1, N_HEADS))
        outs_s[3].append(s_new_s.reshape(bs, N_HEADS, HEAD_DIM, HEAD_DIM))
        outs_s[4].append(h_last_s.reshape(bs, w_lru))
        outs_s[5].append(conv_new_s[:, 8 - (CONV_W - 1):, :])

    stk = lambda lst: jnp.stack(lst, axis=0)
    return (xp.reshape(bp, lp, d), xs.reshape(bs, 1, d),
            *[stk(o) for o in outs_p], *[stk(o) for o in outs_s])
```

```python
import functools

import numpy as np
import jax
import jax.numpy as jnp
from jax import lax
from jax.experimental import pallas as pl
from jax.experimental.pallas import tpu as pltpu

F32 = jnp.float32
BF16 = jnp.bfloat16

HEAD_DIM = 64
N_HEADS = 4
W_HEAD = N_HEADS * HEAD_DIM
CONV_W = 4
LRU_C = 8.0
RET_CHUNK = 256
ROPE_BASE = 10000.0
EPS = 1e-6
NEG_INF = -1e30
FOX_SCALE = HEAD_DIM ** -0.5
LOG2E = 1.4426950408889634
V_AUG_ROWS = 80
SAMPLE_ROWS = 16
PAGES_PER_STEP = 32
VMEM_LIMIT = 48 * 1024 * 1024

COL_RQ, COL_RK, COL_RV, COL_RG = 0, 1, 2, 3
COL_LX, COL_LG = 2, 3
COL_FQ, COL_FK, COL_FV = 8, 9, 10
COL_FF = 22
PROJ_W = 23 * 128


def _dot(a, b):
    return jnp.dot(a, b, preferred_element_type=F32)


def _dot_nt(a, b):
    return lax.dot_general(a, b, (((1,), (1,)), ((), ())), preferred_element_type=F32)


def _dot_tn(a, b):
    return lax.dot_general(a, b, (((0,), (0,)), ((), ())), preferred_element_type=F32)


def _sigmoid(x):
    return 0.5 * jnp.tanh(0.5 * x) + 0.5


def _split2(x):
    hi = x.astype(BF16)
    lo = (x - hi.astype(F32)).astype(BF16)
    return hi, lo


def _split3(x):
    hi = x.astype(BF16)
    r = x - hi.astype(F32)
    mid = r.astype(BF16)
    lo = (r - mid.astype(F32)).astype(BF16)
    return hi, mid, lo


def _params(*sem):
    return pltpu.CompilerParams(dimension_semantics=sem, vmem_limit_bytes=VMEM_LIMIT)


def _ffn_body(has_mix, tf, *refs):
    if has_mix:
        (x_ref, yr_ref, yl_ref, yf_ref, wo_ref, g_ref, wg_ref, wu_ref, wd_ref,
         o_ref, xres_ref, xn_ref, hid_ref) = refs
    else:
        x_ref, g_ref, wg_ref, wu_ref, wd_ref, o_ref, xres_ref, xn_ref, hid_ref = refs
    x = x_ref[...]
    if has_mix:
        w1 = yr_ref.shape[1]
        w2 = w1 + yl_ref.shape[1]
        x = (x + _dot(yr_ref[...], wo_ref[0:w1, :]) + _dot(yl_ref[...], wo_ref[w1:w2, :])
             + _dot(yf_ref[...], wo_ref[w2:, :]))
    xres_ref[...] = x
    ms = jnp.mean(x * x, axis=-1, keepdims=True)
    xn_ref[...] = (x * lax.rsqrt(ms + EPS) * g_ref[...]).astype(BF16)

    xn = xn_ref[...]
    for j in range(wg_ref.shape[1] // tf):
        gate = _dot(xn, wg_ref[:, j * tf:(j + 1) * tf])
        up = _dot(xn, wu_ref[:, j * tf:(j + 1) * tf])
        hid_ref[:, j * tf:(j + 1) * tf] = (gate * _sigmoid(gate) * up).astype(BF16)
    o_ref[...] = xres_ref[...] + 0.5 * _dot(hid_ref[...], wd_ref[...])


def _ffn(x, g, wg, wu, wd, layer, mix=None, tm=512, tf=256):
    m, d = x.shape
    f = wg.shape[2]
    tm = min(tm, m)
    row = lambda i: (i, 0)
    resident = lambda a: pl.BlockSpec((None,) + a.shape[1:], lambda i: (layer, 0, 0),
                                      pipeline_mode=pl.Buffered(1))
    in_specs = [pl.BlockSpec((tm, d), row)]
    args = [x]
    if mix is not None:
        yr, yl, yf, wo = mix
        in_specs += [pl.BlockSpec((tm, yr.shape[1]), row), pl.BlockSpec((tm, yl.shape[1]), row),
                     pl.BlockSpec((tm, yf.shape[1]), row), resident(wo)]
        args += [yr, yl, yf, wo]
    in_specs += [pl.BlockSpec((1, d), lambda i: (0, 0), pipeline_mode=pl.Buffered(1)),
                 resident(wg), resident(wu), resident(wd)]
    args += [g.reshape(1, d), wg, wu, wd]
    return pl.pallas_call(
        functools.partial(_ffn_body, mix is not None, tf),
        grid=(m // tm,),
        in_specs=in_specs,
        out_specs=pl.BlockSpec((tm, d), row),
        out_shape=jax.ShapeDtypeStruct((m, d), F32),
        scratch_shapes=[pltpu.VMEM((tm, d), F32), pltpu.VMEM((tm, d), BF16), pltpu.VMEM((tm, f), BF16)],
        compiler_params=_params("parallel"),
        name="ffn_mix" if mix is not None else "ffn",
    )(*args)


def _inproj_body(x_ref, g_ref, w_ref, o_ref):
    x = x_ref[...]
    ms = jnp.mean(x * x, axis=-1, keepdims=True)
    xn = (x * lax.rsqrt(ms + EPS) * g_ref[...]).astype(BF16)
    o_ref[...] = _dot(xn, w_ref[...])


def _inproj(x, g, w, layer, tm=512):
    m, d = x.shape
    n = w.shape[2]
    tm = min(tm, m)
    return pl.pallas_call(
        _inproj_body,
        grid=(m // tm,),
        in_specs=[pl.BlockSpec((tm, d), lambda i: (i, 0)),
                  pl.BlockSpec((1, d), lambda i: (0, 0)),
                  pl.BlockSpec((None, d, n), lambda i: (layer, 0, 0))],
        out_specs=pl.BlockSpec((tm, n), lambda i: (i, 0)),
        out_shape=jax.ShapeDtypeStruct((m, n), F32),
        compiler_params=_params("parallel"),
        name="inproj",
    )(x, g.reshape(1, d), w)


def _head_norm(x, g):
    outs = []
    for h in range(N_HEADS):
        xh = x[:, h * HEAD_DIM:(h + 1) * HEAD_DIM]
        ms = jnp.mean(xh * xh, axis=-1, keepdims=True)
        outs.append(xh * lax.rsqrt(ms + EPS) * g)
    return outs


def _pad16(x):
    return jnp.concatenate([x, jnp.zeros_like(x)], axis=0)


def _fox_prep_sample_body(tm, fq_ref, fk_ref, fv_ref, ff_ref, gq_ref, gk_ref, bf_ref,
                          kn_ref, logf_ref, qn_ref):
    del fv_ref
    lane = lax.broadcasted_iota(jnp.int32, (tm, 128), 1)
    z = ff_ref[...] + bf_ref[...]
    logf = jnp.minimum(z, 0.0) - jnp.log1p(jnp.exp(-jnp.abs(z)))
    logf_ref[...] = jnp.where(lane < N_HEADS, logf, 0.0)
    kn_ref[...] = jnp.concatenate(_head_norm(fk_ref[...], gk_ref[...]), axis=-1)
    qn_ref[...] = jnp.concatenate(_head_norm(fq_ref[...], gq_ref[...]), axis=-1)


def _fox_prep_prompt_body(tm, fq_ref, fk_ref, fv_ref, ff_ref, gq_ref, gk_ref, bf_ref,
                          knt_ref, vt_ref, lft_ref, qa_ref, ka_ref, va_ref, carry_ref):
    @pl.when(pl.program_id(1) == 0)
    def _():
        carry_ref[...] = jnp.zeros_like(carry_ref)

    q_t = fq_ref[...].T
    k_t = fk_ref[...].T
    v_t = fv_ref[...].T
    z = ff_ref[...].T[0:8, :] + bf_ref[...]
    lf = jnp.minimum(z, 0.0) - jnp.log1p(jnp.exp(-jnp.abs(z)))
    lft_ref[...] = lf
    vt_ref[...] = v_t

    r_i = lax.broadcasted_iota(jnp.int32, (tm, tm), 0)
    c_i = lax.broadcasted_iota(jnp.int32, (tm, tm), 1)
    upper = jnp.where(r_i <= c_i, 1.0, 0.0).astype(BF16)
    hi, mid, lo = _split3(_pad16(lf))
    cs = (_dot(hi, upper) + _dot(mid, upper) + _dot(lo, upper))[0:8, :] + carry_ref[...]
    carry_ref[...] = cs[:, tm - 1:tm]
    cs = cs * LOG2E

    sub = lax.broadcasted_iota(jnp.int32, (HEAD_DIM, tm), 0)
    v_tail = jnp.where(lax.broadcasted_iota(jnp.int32, (V_AUG_ROWS - HEAD_DIM, tm), 0) == 0, 1.0, 0.0)
    kn_parts = []
    for h in range(N_HEADS):
        rows = slice(h * HEAD_DIM, (h + 1) * HEAD_DIM)
        qh, kh = q_t[rows], k_t[rows]
        qn = qh * lax.rsqrt(jnp.mean(qh * qh, axis=0, keepdims=True) + EPS) * gq_ref[...]
        kn = kh * lax.rsqrt(jnp.mean(kh * kh, axis=0, keepdims=True) + EPS) * gk_ref[...]
        kn_parts.append(kn)
        ch = cs[h:h + 1, :]
        chi = ch.astype(BF16).astype(F32)
        r1 = ch - chi
        cmid = r1.astype(BF16).astype(F32)
        clo = r1 - cmid
        q_tail = jnp.where(sub == 0, chi, jnp.where(sub == 1, cmid, jnp.where(
            sub == 2, clo, jnp.where(sub < 6, 1.0, 0.0))))
        k_tail = jnp.where(sub < 3, 1.0, jnp.where(sub == 3, -chi, jnp.where(
            sub == 4, -cmid, jnp.where(sub == 5, -clo, 0.0))))
        qa_ref[h] = jnp.concatenate([qn * (FOX_SCALE * LOG2E), q_tail], axis=0).astype(BF16)
        ka_ref[h] = jnp.concatenate([kn, k_tail], axis=0).T.astype(BF16)
        va_ref[h] = jnp.concatenate([v_t[rows], v_tail], axis=0).astype(BF16)
    knt_ref[...] = jnp.concatenate(kn_parts, axis=0)


def _fox_prep(proj, gq, gk, bf, nb, l, prompt, tm=512):
    m = proj.shape[0]
    tm = min(tm, l)
    nl = l // tm
    rows = lambda b, i: b * nl + i
    const2 = lambda b, i: (0, 0)
    in_specs = [pl.BlockSpec((tm, W_HEAD), lambda b, i: (rows(b, i), COL_FQ)),
                pl.BlockSpec((tm, W_HEAD), lambda b, i: (rows(b, i), COL_FK)),
                pl.BlockSpec((tm, W_HEAD), lambda b, i: (rows(b, i), COL_FV)),
                pl.BlockSpec((tm, 128), lambda b, i: (rows(b, i), COL_FF))]
    if prompt:
        in_specs += [pl.BlockSpec((HEAD_DIM, 1), const2), pl.BlockSpec((HEAD_DIM, 1), const2),
                     pl.BlockSpec((8, 1), const2)]
        params = (gq.reshape(HEAD_DIM, 1), gk.reshape(HEAD_DIM, 1),
                  jnp.zeros((8, 1), F32).at[:N_HEADS, 0].set(bf))
        pos_minor = lambda r: pl.BlockSpec((None, r, tm), lambda b, i: (b, 0, i))
        aug = pl.BlockSpec((None, N_HEADS, tm, 128), lambda b, i: (b, 0, i, 0))
        aug_t = pl.BlockSpec((None, N_HEADS, 128, tm), lambda b, i: (b, 0, 0, i))
        aug_v = pl.BlockSpec((None, N_HEADS, V_AUG_ROWS, tm), lambda b, i: (b, 0, 0, i))
        out_specs = [pos_minor(W_HEAD), pos_minor(W_HEAD), pos_minor(8), aug_t, aug, aug_v]
        out_shape = [jax.ShapeDtypeStruct((nb, W_HEAD, l), F32), jax.ShapeDtypeStruct((nb, W_HEAD, l), F32),
                     jax.ShapeDtypeStruct((nb, 8, l), F32),
                     jax.ShapeDtypeStruct((nb, N_HEADS, 128, l), BF16),
                     jax.ShapeDtypeStruct((nb, N_HEADS, l, 128), BF16),
                     jax.ShapeDtypeStruct((nb, N_HEADS, V_AUG_ROWS, l), BF16)]
        scratch = [pltpu.VMEM((8, 1), F32)]
        body = functools.partial(_fox_prep_prompt_body, tm)
    else:
        in_specs += [pl.BlockSpec((1, HEAD_DIM), const2), pl.BlockSpec((1, HEAD_DIM), const2),
                     pl.BlockSpec((1, 128), const2)]
        params = (gq.reshape(1, HEAD_DIM), gk.reshape(1, HEAD_DIM),
                  jnp.zeros((1, 128), F32).at[0, :N_HEADS].set(bf))
        row_spec = lambda w: pl.BlockSpec((tm, w), lambda b, i: (rows(b, i), 0))
        out_specs = [row_spec(W_HEAD), row_spec(128), row_spec(W_HEAD)]
        out_shape = [jax.ShapeDtypeStruct((m, W_HEAD), F32), jax.ShapeDtypeStruct((m, 128), F32),
                     jax.ShapeDtypeStruct((m, W_HEAD), F32)]
        scratch = []
        body = functools.partial(_fox_prep_sample_body, tm)
    return pl.pallas_call(
        body,
        grid=(nb, nl),
        in_specs=in_specs,
        out_specs=out_specs,
        out_shape=out_shape,
        scratch_shapes=scratch,
        compiler_params=_params("arbitrary", "arbitrary"),
        name="fox_prep_prompt" if prompt else "fox_prep_sample",
    )(proj, proj, proj, proj, *params)


def _fox_attn_body(tq, tk, qt_ref, ka_ref, vt_ref, o_ref, s_ref, m_ref, acc_ref):
    qi = pl.program_id(2)
    n_diag = tq // tk
    heads = range(2)
    m_ref[...] = jnp.full_like(m_ref, NEG_INF)
    acc_ref[...] = jnp.zeros_like(acc_ref)

    def scores(slot, j):
        k0 = pl.multiple_of(j * tk, tk)
        for h in heads:
            s_ref[slot, h] = _dot(ka_ref[h, pl.ds(k0, tk), :], qt_ref[h])

    def absorb(h, s, v_t, lanes):
        m_old = m_ref[h, :, lanes]
        m_new = jnp.maximum(m_old, jnp.max(s, axis=0, keepdims=True))
        alpha = jnp.exp2(m_old - m_new)
        p = jnp.exp2(s - m_new)
        acc_ref[h, :, lanes] = alpha * acc_ref[h, :, lanes] + _dot(v_t, p.astype(BF16))
        m_ref[h, :, lanes] = m_new

    def consume(slot, j):
        k0 = pl.multiple_of(j * tk, tk)
        for h in heads:
            absorb(h, s_ref[slot, h], vt_ref[h, :, pl.ds(k0, tk)], slice(None))

    scores(0, 0)

    def pair(jj):
        scores(1, 2 * jj + 1)
        consume(0, 2 * jj)
        scores(0, 2 * jj + 2)
        consume(1, 2 * jj + 1)

    def four_pairs(t, carry):
        for u in range(4):
            pair(4 * t + u)
        return carry

    n_pairs = qi * (n_diag // 2)
    lax.fori_loop(0, n_pairs // 4, four_pairs, 0)
    done4 = (n_pairs // 4) * 4

    @pl.when(n_pairs % 4 >= 2)
    def _():
        pair(done4)
        pair(done4 + 1)

    @pl.when(n_pairs % 2 == 1)
    def _():
        pair(n_pairs - 1)

    def key_start(d):
        return pl.multiple_of((qi * n_diag + d) * tk, tk)

    def causal(n):
        return lax.broadcasted_iota(jnp.int32, (tk, n), 0) <= lax.broadcasted_iota(jnp.int32, (tk, n), 1)

    late = {d: [_dot(ka_ref[h, pl.ds(key_start(d), tk), :], qt_ref[h, :, d * tk:tq]) for h in heads]
            for d in range(1, n_diag)}
    for d in range(n_diag):
        for h in heads:
            s = s_ref[0, h] if d == 0 else late[d][h]
            absorb(h, jnp.where(causal(tq - d * tk), s, NEG_INF), vt_ref[h, :, pl.ds(key_start(d), tk)],
                   slice(d * tk, tq))

    outs = []
    for h in heads:
        acc = acc_ref[h]
        outs.append(acc[0:HEAD_DIM, :] / acc[HEAD_DIM:HEAD_DIM + 1, :])
    o_ref[...] = jnp.concatenate(outs, axis=0).T.astype(BF16)


def _fox_attn(qt, ka, vt, tq=512):
    nb, nh, l, _ = ka.shape
    tq = min(tq, l)
    tk = tq // 2
    nq = l // tq
    return pl.pallas_call(
        functools.partial(_fox_attn_body, tq, tk),
        grid=(nb, nh // 2, nq),
        in_specs=[pl.BlockSpec((None, 2, 128, tq), lambda b, hp, i: (b, hp, 0, i)),
                  pl.BlockSpec((None, 2, l, 128), lambda b, hp, i: (b, hp, 0, 0)),
                  pl.BlockSpec((None, 2, V_AUG_ROWS, l), lambda b, hp, i: (b, hp, 0, 0))],
        out_specs=pl.BlockSpec((tq, 128), lambda b, hp, i: (b * nq + i, hp)),
        out_shape=jax.ShapeDtypeStruct((nb * l, W_HEAD), BF16),
        scratch_shapes=[pltpu.VMEM((2, 2, tk, tq), F32), pltpu.VMEM((2, 1, tq), F32),
                        pltpu.VMEM((2, V_AUG_ROWS, tq), F32)],
        compiler_params=_params("parallel", "parallel", "arbitrary"),
        name="fox_attn",
    )(qt, ka, vt)


def _ret_tables(pos0, l_pad, l_real, chunk):
    c_real = chunk if l_real % chunk == 0 else l_real
    half = HEAD_DIM // 2
    inv = ROPE_BASE ** (-np.arange(half, dtype=np.float64) / half)
    inv_l = np.tile(inv, 2 * N_HEADS)
    sgn = np.tile(np.concatenate([-np.ones(half), np.ones(half)]), N_HEADS)
    n_a = l_pad // chunk
    ang_a = (pos0 + chunk * np.arange(n_a))[:, None] * inv_l[None, :]
    ang_b = np.arange(chunk)[:, None] * inv_l[None, :]
    log_g = np.log1p(-np.exp2(-5.0 - np.arange(N_HEADS, dtype=np.float64)))
    idx = np.arange(chunk, dtype=np.float64)
    diff = idx[:, None] - idx[None, :]
    decay = np.where(diff >= 0, np.exp(np.maximum(diff, 0.0)[None] * log_g[:, None, None]), 0.0)
    lg_l = np.repeat(log_g, HEAD_DIM)
    xi = np.exp((idx[:, None] + 1.0) * lg_l[None, :])
    zeta = np.exp((c_real - 1.0 - idx[:, None]) * lg_l[None, :])
    head = np.arange(W_HEAD) // HEAD_DIM
    bd = (head[:, None] == head[None, :]).astype(np.float64)
    gmat = bd * np.exp(c_real * lg_l)[:, None]
    f = lambda a: jnp.asarray(a, dtype=F32)
    return dict(ca=f(np.cos(ang_a)), sa=f(sgn * np.sin(ang_a)), cb=f(np.cos(ang_b)), sb=f(sgn * np.sin(ang_b)),
                decay=f(decay), xi=f(xi), zeta=f(zeta), gmat=f(gmat), bd=f(bd),
                bdavg=jnp.asarray(bd / HEAD_DIM, dtype=BF16))


def _ret_body(tm, ch, n_seq, q_ref, k_ref, v_ref, g_ref, ca_ref, sa_ref, cb_ref, sb_ref, dec_ref, xi_ref,
              zeta_ref, gmat_ref, bd_ref, bdavg_ref, gn_ref, s0_ref, y_ref, sout_ref, st_ref):
    i = pl.program_id(1)
    n_chunk = tm // ch

    @pl.when(i == 0)
    def _():
        for s in range(n_seq):
            st_ref[s] = jnp.concatenate([s0_ref[s]] * N_HEADS, axis=-1) * bd_ref[...]

    lane = lax.broadcasted_iota(jnp.int32, (ch, W_HEAD), 1)
    lane128 = lax.broadcasted_iota(jnp.int32, (ch, 128), 1)
    first_half = (lane128 % HEAD_DIM) < (HEAD_DIM // 2)

    def swap_halves(x):
        parts = []
        for blk in range(W_HEAD // 128):
            xb = x[:, blk * 128:(blk + 1) * 128]
            parts.append(jnp.where(first_half, pltpu.roll(xb, 128 - HEAD_DIM // 2, 1),
                                   pltpu.roll(xb, HEAD_DIM // 2, 1)))
        return jnp.concatenate(parts, axis=-1)

    def gmean(t):
        hi, lo = _split2(t)
        return _dot(hi, bdavg_ref[...]) + _dot(lo, bdavg_ref[...])

    def chunk(s, c):
        r0 = s * tm + c * ch
        a = i * n_chunk + c
        c_a = ca_ref[pl.ds(a, 1), :]
        s_a = sa_ref[pl.ds(a, 1), :]
        cos = c_a * cb_ref[...] - s_a * sb_ref[...]
        sin = s_a * cb_ref[...] + c_a * sb_ref[...]
        q = q_ref[pl.ds(r0, ch), :]
        k = k_ref[pl.ds(r0, ch), :]
        q = q * cos + swap_halves(q) * sin
        k = (k * cos + swap_halves(k) * sin) * FOX_SCALE
        qb = q.astype(BF16)
        kb = k.astype(BF16)
        vb = v_ref[pl.ds(r0, ch), :].astype(BF16)
        st = st_ref[s]
        o = _dot(qb, st.astype(BF16)) * xi_ref[...]
        for h in range(N_HEADS):
            mh = (lane // HEAD_DIM) == h
            qm = jnp.where(mh, q, 0.0).astype(BF16)
            att = _dot_nt(qm, kb) * dec_ref[h]
            o = o + jnp.where(mh, _dot(att.astype(BF16), vb), 0.0)
        kz = (k * zeta_ref[...]).astype(BF16)
        st_ref[s] = st * gmat_ref[...] + _dot_tn(kz, vb) * bd_ref[...]
        mu = gmean(o)
        d = o - mu
        var = gmean(d * d)
        on = d * lax.rsqrt(var + EPS) * gn_ref[...]
        gate = g_ref[pl.ds(r0, ch), :]
        y_ref[pl.ds(r0, ch), :] = (on * (gate * _sigmoid(gate))).astype(BF16)

    for c in range(n_chunk):
        for s in range(n_seq):
            chunk(s, c)

    @pl.when(i == pl.num_programs(1) - 1)
    def _():
        for s in range(n_seq):
            st = st_ref[s]
            acc = st[:, 0:HEAD_DIM]
            for h in range(1, N_HEADS):
                acc = acc + st[:, h * HEAD_DIM:(h + 1) * HEAD_DIM]
            sout_ref[s] = acc


def _retention(proj, gn, s0, nb, l, l_real, pos0, tm=512, n_seq=1):
    m = proj.shape[0]
    tm = min(tm, l)
    ch = min(RET_CHUNK, tm)
    nl = l // tm
    assert n_seq == 1 or (nl == 1 and nb % n_seq == 0)
    t = _ret_tables(pos0, l, l_real, ch)
    rows = lambda b, i: b * nl + i
    const2 = lambda b, i: (0, 0)
    col = lambda cidx: pl.BlockSpec((n_seq * tm, W_HEAD), lambda b, i: (rows(b, i), cidx))
    full = lambda a: pl.BlockSpec(a.shape, (lambda b, i: (0,) * a.ndim))
    in_specs = [col(COL_RQ), col(COL_RK), col(COL_RV), col(COL_RG),
                full(t["ca"]), full(t["sa"]), full(t["cb"]), full(t["sb"]), full(t["decay"]),
                full(t["xi"]), full(t["zeta"]), full(t["gmat"]), full(t["bd"]), full(t["bdavg"]),
                pl.BlockSpec((1, W_HEAD), const2),
                pl.BlockSpec((n_seq, W_HEAD, HEAD_DIM), lambda b, i: (b, 0, 0))]
    return pl.pallas_call(
        functools.partial(_ret_body, tm, ch, n_seq),
        grid=(nb // n_seq, nl),
        in_specs=in_specs,
        out_specs=[pl.BlockSpec((n_seq * tm, W_HEAD), lambda b, i: (rows(b, i), 0)),
                   pl.BlockSpec((n_seq, W_HEAD, HEAD_DIM), lambda b, i: (b, 0, 0))],
        out_shape=[jax.ShapeDtypeStruct((m, W_HEAD), BF16),
                   jax.ShapeDtypeStruct((nb, W_HEAD, HEAD_DIM), F32)],
        scratch_shapes=[pltpu.VMEM((n_seq, W_HEAD, W_HEAD), F32)],
        compiler_params=_params("arbitrary", "arbitrary"),
        name="retention",
    )(proj, proj, proj, proj, t["ca"], t["sa"], t["cb"], t["sb"], t["decay"], t["xi"], t["zeta"],
      t["gmat"], t["bd"], t["bdavg"], gn.reshape(1, W_HEAD), s0)


def _lru_body(tm, l_real, lx_ref, lg_ref, cw_ref, cb_ref, wr_ref, wi_ref, br_ref, bi_ref, lam_ref,
              conv0_ref, h0_ref, y_ref, hlast_ref, convnew_ref, xpad_ref, a_ref, b_ref, h_ref, hcar_ref):
    i = pl.program_id(1)
    w = lx_ref.shape[1]
    t_last, r_last = (l_real - 1) // tm, (l_real - 1) % tm

    @pl.when(i == 0)
    def _():
        xpad_ref[0:8, :] = conv0_ref[...]
        hcar_ref[...] = h0_ref[...]

    x = lx_ref[...]
    xpad_ref[8:8 + tm, :] = x
    xc = cb_ref[...] + cw_ref[CONV_W - 1:CONV_W, :] * x
    for j in range(1, CONV_W):
        xc = xc + cw_ref[CONV_W - 1 - j:CONV_W - j, :] * xpad_ref[pl.ds(8 - j, tm), :]

    xb = xc.astype(BF16)
    r = _sigmoid(_dot(xb, wr_ref[...]) + br_ref[...])
    ig = _sigmoid(_dot(xb, wi_ref[...]) + bi_ref[...])
    z = -lam_ref[...]
    softplus = jnp.maximum(z, 0.0) + jnp.log1p(jnp.exp(-jnp.abs(z)))
    log_a = (-LRU_C) * r * softplus
    a = jnp.exp(log_a)
    th = jnp.tanh(log_a)
    em = -2.0 * th / (1.0 - th)
    u = jnp.where(em > 0.0, em * lax.rsqrt(em), 0.0) * (ig * xc)

    a = a.reshape(tm // 8, 8, w)
    u = u.reshape(tm // 8, 8, w)
    rowmod = lax.broadcasted_iota(jnp.int32, (tm // 8, 8, w), 1)
    for sh in (1, 2, 4):
        a_sh = pltpu.roll(a, sh, 1)
        u_sh = pltpu.roll(u, sh, 1)
        valid = rowmod >= sh
        u = jnp.where(valid, a * u_sh + u, u)
        a = jnp.where(valid, a * a_sh, a)
    a_ref[...] = a.reshape(tm, w)
    b_ref[...] = u.reshape(tm, w)

    def group(gi, h):
        r0 = pl.multiple_of(gi * 8, 8)
        hg = a_ref[pl.ds(r0, 8), :] * h + b_ref[pl.ds(r0, 8), :]
        h_ref[pl.ds(r0, 8), :] = hg
        return hg[7:8, :]

    hcar_ref[...] = lax.fori_loop(0, tm // 8, group, hcar_ref[...])

    g = lg_ref[...]
    gelu = 0.5 * g * (1.0 + jnp.tanh(0.7978845608028654 * (g + 0.044715 * (g * g * g))))
    y_ref[...] = (h_ref[...] * gelu).astype(BF16)

    @pl.when(i == t_last)
    def _():
        hlast_ref[...] = h_ref[r_last:r_last + 1, :]
        convnew_ref[...] = xpad_ref[pl.ds(r_last + 1, 8), :]

    xpad_ref[0:8, :] = xpad_ref[tm:tm + 8, :]


def _lru(proj, cw, cb, wr, wi, br, bi, lam, conv0, h0, nb, l, l_real, tm=512):
    m = proj.shape[0]
    w = cw.shape[1]
    tm = min(tm, l)
    nl = l // tm
    rows = lambda b, i: b * nl + i
    const2 = lambda b, i: (0, 0)
    vec = pl.BlockSpec((1, w), const2)
    per_b = lambda r: pl.BlockSpec((None, r, w), lambda b, i: (b, 0, 0))
    return pl.pallas_call(
        functools.partial(_lru_body, tm, l_real),
        grid=(nb, nl),
        in_specs=[pl.BlockSpec((tm, w), lambda b, i: (rows(b, i), COL_LX)),
                  pl.BlockSpec((tm, w), lambda b, i: (rows(b, i), COL_LG)),
                  pl.BlockSpec((CONV_W, w), const2), vec,
                  pl.BlockSpec((w, w), const2), pl.BlockSpec((w, w), const2), vec, vec, vec,
                  per_b(8), per_b(1)],
        out_specs=[pl.BlockSpec((tm, w), lambda b, i: (rows(b, i), 0)), per_b(1), per_b(8)],
        out_shape=[jax.ShapeDtypeStruct((m, w), BF16), jax.ShapeDtypeStruct((nb, 1, w), F32),
                   jax.ShapeDtypeStruct((nb, 8, w), F32)],
        scratch_shapes=[pltpu.VMEM((tm + 8, w), F32), pltpu.VMEM((tm, w), F32), pltpu.VMEM((tm, w), F32),
                        pltpu.VMEM((tm, w), F32), pltpu.VMEM((1, w), F32)],
        compiler_params=_params("arbitrary", "arbitrary"),
        name="lru",
    )(proj, proj, cw, cb.reshape(1, w), wr, wi, br.reshape(1, w), bi.reshape(1, w), lam.reshape(1, w),
      conv0, h0)


def _decode_body(n_pg, layer, pt_ref, qn_ref, knew_ref, vnew_ref, lfnew_ref, kt_hbm, vt_hbm, lf_hbm,
                 o_ref, kbuf, vbuf, lfbuf, sem, m_ref, l_ref, acc_ref, car_ref, qm_ref):
    b_i = pl.program_id(0)
    p_i = pl.program_id(1)
    n_steps = pl.num_programs(1)
    t = b_i * n_steps + p_i
    slot = lax.rem(t, 2)
    page = kbuf.shape[3]

    def page_copies(pages, sl):
        cps = []
        for g in range(n_pg):
            pg = pages(g)
            cps.append(pltpu.make_async_copy(kt_hbm.at[layer, pg], kbuf.at[sl, g], sem.at[0, sl]))
            cps.append(pltpu.make_async_copy(vt_hbm.at[layer, pg], vbuf.at[sl, g], sem.at[1, sl]))
            cps.append(pltpu.make_async_copy(lf_hbm.at[layer, pg], lfbuf.at[sl, g], sem.at[2, sl]))
        return cps

    @pl.when(t == 0)
    def _():
        for cp in page_copies(lambda g: pt_ref[0, g], 0):
            cp.start()

    @pl.when(t + 1 < pl.num_programs(0) * n_steps)
    def _():
        wrap = p_i + 1 == n_steps
        b_n = jnp.where(wrap, b_i + 1, b_i)
        p_n = jnp.where(wrap, 0, p_i + 1)
        for cp in page_copies(lambda g: pt_ref[b_n, p_n * n_pg + g], 1 - slot):
            cp.start()

    for cp in page_copies(lambda g: 0, slot):
        cp.wait()
    k_refs = [kbuf.at[slot, g] for g in range(n_pg)]
    v_refs = [vbuf.at[slot, g] for g in range(n_pg)]
    lf_refs = [lfbuf.at[slot, g] for g in range(n_pg)]

    row = lax.broadcasted_iota(jnp.int32, (8, W_HEAD), 0)
    lane = lax.broadcasted_iota(jnp.int32, (8, W_HEAD), 1)
    own = (lane // HEAD_DIM) == row

    @pl.when(p_i == 0)
    def _():
        q = jnp.where(own, jnp.broadcast_to(qn_ref[0:1, :] * FOX_SCALE, (8, W_HEAD)), 0.0)
        qm_ref[...] = _pad16(q).astype(BF16)
        m_ref[...] = jnp.full_like(m_ref, NEG_INF)
        l_ref[...] = jnp.zeros_like(l_ref)
        acc_ref[...] = jnp.zeros_like(acc_ref)
        car_ref[...] = jnp.zeros_like(car_ref)

    qm = qm_ref[...]

    def update(s, pv_fn):
        m_old = m_ref[...]
        m_new = jnp.maximum(m_old, jnp.max(s, axis=-1, keepdims=True))
        alpha = jnp.exp(m_old - m_new)
        p = jnp.exp(s - m_new)
        l_ref[...] = alpha * l_ref[...] + jnp.sum(p, axis=-1, keepdims=True)
        acc_ref[...] = alpha * acc_ref[...] + pv_fn(p)
        m_ref[...] = m_new

    r_i = lax.broadcasted_iota(jnp.int32, (page, page), 0)
    c_i = lax.broadcasted_iota(jnp.int32, (page, page), 1)
    upper = jnp.where(r_i <= c_i, 1.0, 0.0).astype(BF16)
    hi, mid, lo = _split3(jnp.concatenate([lf_refs[g][...] for g in range(n_pg)], axis=0))
    c_all = _dot(hi, upper) + _dot(mid, upper) + _dot(lo, upper)
    nr = n_pg * 8
    rr = lax.broadcasted_iota(jnp.int32, (nr, nr), 0)
    cc = lax.broadcasted_iota(jnp.int32, (nr, nr), 1)
    before = jnp.where((rr % 8 == cc % 8) & (cc // 8 < rr // 8), 1.0, 0.0).astype(BF16)
    chi, cmid, clo = _split3(c_all)
    pref = _dot(before, chi) + _dot(before, cmid) + _dot(before, clo)
    c_all = c_all + pref[:, page - 1:page] + jnp.concatenate([car_ref[...]] * n_pg, axis=0)
    car_ref[...] = c_all[nr - 8:nr, page - 1:page]
    s_parts = [_dot(qm, k_refs[g][...].astype(BF16))[0:8, :] - c_all[g * 8:(g + 1) * 8, :]
               for g in range(n_pg)]

    def pv_pages(p):
        pb = _pad16(p).astype(BF16)
        pv = _dot_nt(pb[:, 0:page], v_refs[0][...].astype(BF16))
        for g in range(1, n_pg):
            pv = pv + _dot_nt(pb[:, g * page:(g + 1) * page], v_refs[g][...].astype(BF16))
        return pv[0:8, :]

    update(jnp.concatenate(s_parts, axis=-1), pv_pages)

    @pl.when(p_i == pl.num_programs(1) - 1)
    def _():
        lane128 = lax.broadcasted_iota(jnp.int32, (8, 128), 1)
        row128 = lax.broadcasted_iota(jnp.int32, (8, 128), 0)
        lf_col = jnp.sum(jnp.where(lane128 == row128, jnp.broadcast_to(lfnew_ref[0:1, :], (8, 128)), 0.0),
                         axis=-1, keepdims=True)
        k_new = knew_ref[0:1, :].astype(BF16).astype(F32)
        v_new = vnew_ref[0:1, :].astype(BF16).astype(F32)
        s_new = jnp.sum(qm[0:8, :].astype(F32) * k_new, axis=-1, keepdims=True) - (car_ref[...] + lf_col)
        update(s_new, lambda p: p.astype(BF16).astype(F32) * v_new)
        o_ref[...] = jnp.sum(jnp.where(own, acc_ref[...] / l_ref[...], 0.0), axis=0, keepdims=True)


def _fox_decode(layer, page_table, qn, kn, proj, logf, cache_kt, cache_vt, cache_lft, nb):
    n_pages = page_table.shape[1]
    n_pg = min(PAGES_PER_STEP, n_pages)
    sr = SAMPLE_ROWS
    page = cache_kt.shape[3]
    tok = lambda cidx: (lambda b, p, pt: (b, cidx))
    hbm = pl.BlockSpec(memory_space=pl.ANY)
    in_specs = [pl.BlockSpec((sr, W_HEAD), tok(0)), pl.BlockSpec((sr, W_HEAD), tok(0)),
                pl.BlockSpec((sr, W_HEAD), tok(COL_FV)), pl.BlockSpec((sr, 128), tok(0)),
                hbm, hbm, hbm]
    grid_spec = pltpu.PrefetchScalarGridSpec(
        num_scalar_prefetch=1,
        grid=(nb, n_pages // n_pg),
        in_specs=in_specs,
        out_specs=pl.BlockSpec((None, 1, W_HEAD), lambda b, p, pt: (b, 0, 0)),
        scratch_shapes=[pltpu.VMEM((2, n_pg, W_HEAD, page), F32), pltpu.VMEM((2, n_pg, W_HEAD, page), F32),
                        pltpu.VMEM((2, n_pg, 8, page), F32), pltpu.SemaphoreType.DMA((3, 2)),
                        pltpu.VMEM((8, 1), F32), pltpu.VMEM((8, 1), F32), pltpu.VMEM((8, W_HEAD), F32),
                        pltpu.VMEM((8, 1), F32), pltpu.VMEM((16, W_HEAD), BF16)])
    return pl.pallas_call(
        functools.partial(_decode_body, n_pg, layer),
        grid_spec=grid_spec,
        out_shape=jax.ShapeDtypeStruct((nb, 1, W_HEAD), F32),
        compiler_params=_params("arbitrary", "arbitrary"),
        name="fox_decode",
    )(page_table, qn, kn, proj, logf, cache_kt, cache_vt, cache_lft)


def _block_diag(w):
    n, d, e = w.shape
    eye = jnp.eye(n, dtype=w.dtype)
    return (eye[:, None, :, None] * w[:, :, None, :]).reshape(n * d, n * e)


def _mixer_common(proj, nb, l, l_real, pos0, s0, conv0, h0, lw):
    n_seq = 4 if (l <= RET_CHUNK and nb % 4 == 0) else 1
    y_ret, s_new = _retention(proj, lw["ret_gn"], s0, nb, l, l_real, pos0, n_seq=n_seq)
    y_lru, h_last, conv_new = _lru(proj, lw["conv_w"], lw["conv_b"], lw["wr"], lw["wi"], lw["br"], lw["bi"],
                                   lw["lam"], conv0, h0, nb, l, l_real)
    return y_ret, y_lru, s_new, h_last, conv_new


def kernel(x_prompt, x_sample, cache_k, cache_v, cache_logf, state_ret, state_lru, state_conv, page_table, norm_ffn1, ffn1_gate, ffn1_up, ffn1_down, norm_mix, w_in, ret_gn, conv_w, conv_b, lru_wr, lru_br, lru_wi, lru_bi, lru_lambda, fox_qn, fox_kn, fox_bf, w_out, norm_ffn2, ffn2_gate, ffn2_up, ffn2_down):
    bp, lp, d = x_prompt.shape
    bs, ls, _ = x_sample.shape
    assert ls == 1
    depth = w_in.shape[0]
    n_pool, page = cache_k.shape[1], cache_k.shape[2]
    past_len = page_table.shape[1] * page
    w_lru = conv_w.shape[2]
    sr = SAMPLE_ROWS
    in_width = w_in.shape[2]

    g1, u1, d1 = ffn1_gate.astype(BF16), ffn1_up.astype(BF16), ffn1_down.astype(BF16)
    g2, u2, d2 = ffn2_gate.astype(BF16), ffn2_up.astype(BF16), ffn2_down.astype(BF16)
    w_out_b = w_out.astype(BF16)
    w_in_b = jnp.pad(w_in, ((0, 0), (0, 0), (0, PROJ_W - in_width))).astype(BF16)
    cache_k4 = jnp.transpose(cache_k, (0, 1, 3, 4, 2)).reshape(depth, n_pool, W_HEAD, page)
    cache_v4 = jnp.transpose(cache_v, (0, 1, 3, 4, 2)).reshape(depth, n_pool, W_HEAD, page)
    cache_lft = jnp.pad(jnp.swapaxes(cache_logf, 2, 3), ((0, 0), (0, 0), (0, 8 - N_HEADS), (0, 0)))

    xp = x_prompt.reshape(bp * lp, d)
    xs = x_sample.reshape(bs, d)
    zero_s = jnp.zeros((bp, W_HEAD, HEAD_DIM), F32)
    zero_c = jnp.zeros((bp, 8, w_lru), F32)
    zero_h = jnp.zeros((bp, 1, w_lru), F32)

    outs_p = [[] for _ in range(6)]
    outs_s = [[] for _ in range(6)]
    for l in range(depth):
        lw = dict(ret_gn=ret_gn[l], conv_w=conv_w[l], conv_b=conv_b[l],
                  wr=_block_diag(lru_wr[l]).astype(BF16), wi=_block_diag(lru_wi[l]).astype(BF16),
                  br=lru_br[l], bi=lru_bi[l], lam=lru_lambda[l])

        xp = _ffn(xp, norm_ffn1[l], g1, u1, d1, l)
        proj = _inproj(xp, norm_mix[l], w_in_b, l)
        kn_t, v_t, logf_t, qa, ka, va = _fox_prep(proj, fox_qn[l], fox_kn[l], fox_bf[l], bp, lp, True)
        y_fox = _fox_attn(qa, ka, va)
        y_ret, y_lru, s_new, h_last, conv_new = _mixer_common(proj, bp, lp, lp, 0, zero_s, zero_c, zero_h, lw)
        xp = _ffn(xp, norm_ffn2[l], g2, u2, d2, l, mix=(y_ret, y_lru, y_fox, w_out_b))
        heads_last = lambda a: jnp.transpose(a.reshape(bp, N_HEADS, HEAD_DIM, lp), (0, 3, 1, 2))
        outs_p[0].append(heads_last(kn_t))
        outs_p[1].append(heads_last(v_t))
        outs_p[2].append(jnp.transpose(logf_t[:, :N_HEADS, :], (0, 2, 1)))
        outs_p[3].append(s_new.reshape(bp, N_HEADS, HEAD_DIM, HEAD_DIM))
        outs_p[4].append(h_last.reshape(bp, w_lru))
        outs_p[5].append(conv_new[:, 8 - (CONV_W - 1):, :])

        xs = _ffn(xs, norm_ffn1[l], g1, u1, d1, l)
        xs_pad = jnp.pad(xs[:, None, :], ((0, 0), (0, sr - 1), (0, 0))).reshape(bs * sr, d)
        proj_s = _inproj(xs_pad, norm_mix[l], w_in_b, l)
        kn_s, logf_s, qn_s = _fox_prep(proj_s, fox_qn[l], fox_kn[l], fox_bf[l], 1, bs * sr, False)
        y_fox_s = _fox_decode(l, page_table, qn_s, kn_s, proj_s, logf_s, cache_k4, cache_v4, cache_lft, bs)
        s0 = state_ret[l].reshape(bs, W_HEAD, HEAD_DIM)
        conv0 = jnp.pad(state_conv[l], ((0, 0), (8 - (CONV_W - 1), 0), (0, 0)))
        h0 = state_lru[l].reshape(bs, 1, w_lru)
        y_ret_s, y_lru_s, s_new_s, h_last_s, conv_new_s = _mixer_common(
            proj_s, bs, sr, 1, past_len, s0, conv0, h0, lw)
        first = lambda a: a.reshape(bs, sr, a.shape[-1])[:, 0, :]
        xs = _ffn(xs, norm_ffn2[l], g2, u2, d2, l,
                  mix=(first(y_ret_s), first(y_lru_s), y_fox_s.reshape(bs, W_HEAD).astype(BF16), w_out_b))
        outs_s[0].append(first(kn_s).reshape(bs, 1, N_HEADS, HEAD_DIM))
        outs_s[1].append(first(proj_s)[:, COL_FV * W_HEAD:(COL_FV + 1) * W_HEAD].reshape(bs, 1, N_HEADS, HEAD_DIM))
        outs_s[2].append(first(logf_s)[:, :N_HEADS].reshape(bs, 1, N_HEADS))
        outs_s[3].append(s_new_s.reshape(bs, N_HEADS, HEAD_DIM, HEAD_DIM))
        outs_s[4].append(h_last_s.reshape(bs, w_lru))
        outs_s[5].append(conv_new_s[:, 8 - (CONV_W - 1):, :])

    stk = lambda lst: jnp.stack(lst, axis=0)
    return (xp.reshape(bp, lp, d), xs.reshape(bs, 1, d),
            *[stk(o) for o in outs_p], *[stk(o) for o in outs_s])
```

```python
import functools

import numpy as np
import jax
import jax.numpy as jnp
from jax import lax
from jax.experimental import pallas as pl
from jax.experimental.pallas import tpu as pltpu

F32 = jnp.float32
BF16 = jnp.bfloat16

HEAD_DIM = 64
N_HEADS = 4
W_HEAD = N_HEADS * HEAD_DIM
CONV_W = 4
LRU_C = 8.0
RET_CHUNK = 256
ROPE_BASE = 10000.0
EPS = 1e-6
NEG_INF = -1e30
FOX_SCALE = HEAD_DIM ** -0.5
LOG2E = 1.4426950408889634
V_AUG_ROWS = 80
SAMPLE_ROWS = 16
PAGES_PER_STEP = 32
VMEM_LIMIT = 48 * 1024 * 1024
VMEM_LIMIT_FUSED = 58 * 1024 * 1024

COL_RQ, COL_RK, COL_RV, COL_RG = 0, 1, 2, 3
COL_LX, COL_LG = 2, 3
COL_FQ, COL_FK, COL_FV = 8, 9, 10
COL_FF = 22
PROJ_W = 23 * 128


def _dot(a, b):
    return jnp.dot(a, b, preferred_element_type=F32)


def _dot_nt(a, b):
    return lax.dot_general(a, b, (((1,), (1,)), ((), ())), preferred_element_type=F32)


def _dot_tn(a, b):
    return lax.dot_general(a, b, (((0,), (0,)), ((), ())), preferred_element_type=F32)


def _sigmoid(x):
    return 0.5 * jnp.tanh(0.5 * x) + 0.5


def _split2(x):
    hi = x.astype(BF16)
    lo = (x - hi.astype(F32)).astype(BF16)
    return hi, lo


def _split3(x):
    hi = x.astype(BF16)
    r = x - hi.astype(F32)
    mid = r.astype(BF16)
    lo = (r - mid.astype(F32)).astype(BF16)
    return hi, mid, lo


def _params(*sem):
    return pltpu.CompilerParams(dimension_semantics=sem, vmem_limit_bytes=VMEM_LIMIT)


def _ffn_body(has_mix, has_proj, tf, *refs):
    refs = list(refs)
    x_ref = refs.pop(0)
    if has_mix:
        yr_ref, yl_ref, yf_ref, wo_ref = refs[:4]
        del refs[:4]
    g_ref, wg_ref, wu_ref, wd_ref = refs[:4]
    del refs[:4]
    if has_proj:
        gm_ref, win_ref = refs[:2]
        del refs[:2]
    o_ref = refs.pop(0)
    if has_proj:
        proj_ref = refs.pop(0)
    xres_ref, xn_ref, hid_ref = refs

    x = x_ref[...]
    if has_mix:
        w1 = yr_ref.shape[1]
        w2 = w1 + yl_ref.shape[1]
        x = (x + _dot(yr_ref[...], wo_ref[0:w1, :]) + _dot(yl_ref[...], wo_ref[w1:w2, :])
             + _dot(yf_ref[...], wo_ref[w2:, :]))
    xres_ref[...] = x
    ms = jnp.mean(x * x, axis=-1, keepdims=True)
    xn_ref[...] = (x * lax.rsqrt(ms + EPS) * g_ref[...]).astype(BF16)

    xn = xn_ref[...]
    for j in range(wg_ref.shape[1] // tf):
        gate = _dot(xn, wg_ref[:, j * tf:(j + 1) * tf])
        up = _dot(xn, wu_ref[:, j * tf:(j + 1) * tf])
        hid_ref[:, j * tf:(j + 1) * tf] = (gate * _sigmoid(gate) * up).astype(BF16)
    out = xres_ref[...] + 0.5 * _dot(hid_ref[...], wd_ref[...])
    o_ref[...] = out
    if has_proj:
        ms_o = jnp.mean(out * out, axis=-1, keepdims=True)
        proj_ref[...] = _dot((out * lax.rsqrt(ms_o + EPS) * gm_ref[...]).astype(BF16), win_ref[...])


def _ffn(x, g, wg, wu, wd, layer, mix=None, proj=None, tm=512, tf=256):
    m, d = x.shape
    f = wg.shape[2]
    tm = min(tm, m)
    row = lambda i: (i, 0)
    resident = lambda a: pl.BlockSpec((None,) + a.shape[1:], lambda i: (layer, 0, 0),
                                      pipeline_mode=pl.Buffered(1))
    vec = pl.BlockSpec((1, d), lambda i: (0, 0), pipeline_mode=pl.Buffered(1))
    in_specs = [pl.BlockSpec((tm, d), row)]
    args = [x]
    if mix is not None:
        yr, yl, yf, wo = mix
        in_specs += [pl.BlockSpec((tm, yr.shape[1]), row), pl.BlockSpec((tm, yl.shape[1]), row),
                     pl.BlockSpec((tm, yf.shape[1]), row), resident(wo)]
        args += [yr, yl, yf, wo]
    in_specs += [vec, resident(wg), resident(wu), resident(wd)]
    args += [g.reshape(1, d), wg, wu, wd]
    out_specs = [pl.BlockSpec((tm, d), row)]
    out_shape = [jax.ShapeDtypeStruct((m, d), F32)]
    vmem = VMEM_LIMIT
    if proj is not None:
        g_mix, w_in = proj
        in_specs += [vec, resident(w_in)]
        args += [g_mix.reshape(1, d), w_in]
        out_specs.append(pl.BlockSpec((tm, w_in.shape[2]), row))
        out_shape.append(jax.ShapeDtypeStruct((m, w_in.shape[2]), F32))
        vmem = VMEM_LIMIT_FUSED
    res = pl.pallas_call(
        functools.partial(_ffn_body, mix is not None, proj is not None, tf),
        grid=(m // tm,),
        in_specs=in_specs,
        out_specs=out_specs,
        out_shape=out_shape,
        scratch_shapes=[pltpu.VMEM((tm, d), F32), pltpu.VMEM((tm, d), BF16), pltpu.VMEM((tm, f), BF16)],
        compiler_params=pltpu.CompilerParams(dimension_semantics=("parallel",), vmem_limit_bytes=vmem),
        name="ffn_mix" if mix is not None else ("ffn_proj" if proj is not None else "ffn"),
    )(*args)
    return res if proj is not None else res[0]


def _inproj_body(x_ref, g_ref, w_ref, o_ref):
    x = x_ref[...]
    ms = jnp.mean(x * x, axis=-1, keepdims=True)
    xn = (x * lax.rsqrt(ms + EPS) * g_ref[...]).astype(BF16)
    o_ref[...] = _dot(xn, w_ref[...])


def _inproj(x, g, w, layer, tm=512):
    m, d = x.shape
    n = w.shape[2]
    tm = min(tm, m)
    return pl.pallas_call(
        _inproj_body,
        grid=(m // tm,),
        in_specs=[pl.BlockSpec((tm, d), lambda i: (i, 0)),
                  pl.BlockSpec((1, d), lambda i: (0, 0)),
                  pl.BlockSpec((None, d, n), lambda i: (layer, 0, 0))],
        out_specs=pl.BlockSpec((tm, n), lambda i: (i, 0)),
        out_shape=jax.ShapeDtypeStruct((m, n), F32),
        compiler_params=_params("parallel"),
        name="inproj",
    )(x, g.reshape(1, d), w)


def _head_norm(x, g):
    outs = []
    for h in range(N_HEADS):
        xh = x[:, h * HEAD_DIM:(h + 1) * HEAD_DIM]
        ms = jnp.mean(xh * xh, axis=-1, keepdims=True)
        outs.append(xh * lax.rsqrt(ms + EPS) * g)
    return outs


def _pad16(x):
    return jnp.concatenate([x, jnp.zeros_like(x)], axis=0)


def _fox_prep_sample_body(tm, fq_ref, fk_ref, fv_ref, ff_ref, gq_ref, gk_ref, bf_ref,
                          kn_ref, logf_ref, qn_ref):
    del fv_ref
    lane = lax.broadcasted_iota(jnp.int32, (tm, 128), 1)
    z = ff_ref[...] + bf_ref[...]
    logf = jnp.minimum(z, 0.0) - jnp.log1p(jnp.exp(-jnp.abs(z)))
    logf_ref[...] = jnp.where(lane < N_HEADS, logf, 0.0)
    kn_ref[...] = jnp.concatenate(_head_norm(fk_ref[...], gk_ref[...]), axis=-1)
    qn_ref[...] = jnp.concatenate(_head_norm(fq_ref[...], gq_ref[...]), axis=-1)


def _fox_prep_prompt_body(tm, fq_ref, fk_ref, fv_ref, ff_ref, gq_ref, gk_ref, bf_ref,
                          knt_ref, vt_ref, lft_ref, qa_ref, ka_ref, va_ref, carry_ref):
    @pl.when(pl.program_id(1) == 0)
    def _():
        carry_ref[...] = jnp.zeros_like(carry_ref)

    q_t = fq_ref[...].T
    k_t = fk_ref[...].T
    v_t = fv_ref[...].T
    z = ff_ref[...].T[0:8, :] + bf_ref[...]
    lf = jnp.minimum(z, 0.0) - jnp.log1p(jnp.exp(-jnp.abs(z)))
    lft_ref[...] = lf
    vt_ref[...] = v_t

    r_i = lax.broadcasted_iota(jnp.int32, (tm, tm), 0)
    c_i = lax.broadcasted_iota(jnp.int32, (tm, tm), 1)
    upper = jnp.where(r_i <= c_i, 1.0, 0.0).astype(BF16)
    hi, mid, lo = _split3(_pad16(lf))
    cs = (_dot(hi, upper) + _dot(mid, upper) + _dot(lo, upper))[0:8, :] + carry_ref[...]
    carry_ref[...] = cs[:, tm - 1:tm]
    cs = cs * LOG2E

    sub = lax.broadcasted_iota(jnp.int32, (HEAD_DIM, tm), 0)
    v_tail = jnp.where(lax.broadcasted_iota(jnp.int32, (V_AUG_ROWS - HEAD_DIM, tm), 0) == 0, 1.0, 0.0)
    kn_parts = []
    for h in range(N_HEADS):
        rows = slice(h * HEAD_DIM, (h + 1) * HEAD_DIM)
        qh, kh = q_t[rows], k_t[rows]
        qn = qh * lax.rsqrt(jnp.mean(qh * qh, axis=0, keepdims=True) + EPS) * gq_ref[...]
        kn = kh * lax.rsqrt(jnp.mean(kh * kh, axis=0, keepdims=True) + EPS) * gk_ref[...]
        kn_parts.append(kn)
        ch = cs[h:h + 1, :]
        chi = ch.astype(BF16).astype(F32)
        r1 = ch - chi
        cmid = r1.astype(BF16).astype(F32)
        clo = r1 - cmid
        q_tail = jnp.where(sub == 0, chi, jnp.where(sub == 1, cmid, jnp.where(
            sub == 2, clo, jnp.where(sub < 6, 1.0, 0.0))))
        k_tail = jnp.where(sub < 3, 1.0, jnp.where(sub == 3, -chi, jnp.where(
            sub == 4, -cmid, jnp.where(sub == 5, -clo, 0.0))))
        qa_ref[h] = jnp.concatenate([qn * (FOX_SCALE * LOG2E), q_tail], axis=0).astype(BF16)
        ka_ref[h] = jnp.concatenate([kn, k_tail], axis=0).T.astype(BF16)
        va_ref[h] = jnp.concatenate([v_t[rows], v_tail], axis=0).astype(BF16)
    knt_ref[...] = jnp.concatenate(kn_parts, axis=0)


def _fox_prep(proj, gq, gk, bf, nb, l, prompt, tm=512):
    m = proj.shape[0]
    tm = min(tm, l)
    nl = l // tm
    rows = lambda b, i: b * nl + i
    const2 = lambda b, i: (0, 0)
    in_specs = [pl.BlockSpec((tm, W_HEAD), lambda b, i: (rows(b, i), COL_FQ)),
                pl.BlockSpec((tm, W_HEAD), lambda b, i: (rows(b, i), COL_FK)),
                pl.BlockSpec((tm, W_HEAD), lambda b, i: (rows(b, i), COL_FV)),
                pl.BlockSpec((tm, 128), lambda b, i: (rows(b, i), COL_FF))]
    if prompt:
        in_specs += [pl.BlockSpec((HEAD_DIM, 1), const2), pl.BlockSpec((HEAD_DIM, 1), const2),
                     pl.BlockSpec((8, 1), const2)]
        params = (gq.reshape(HEAD_DIM, 1), gk.reshape(HEAD_DIM, 1),
                  jnp.zeros((8, 1), F32).at[:N_HEADS, 0].set(bf))
        pos_minor = lambda r: pl.BlockSpec((None, r, tm), lambda b, i: (b, 0, i))
        aug = pl.BlockSpec((None, N_HEADS, tm, 128), lambda b, i: (b, 0, i, 0))
        aug_t = pl.BlockSpec((None, N_HEADS, 128, tm), lambda b, i: (b, 0, 0, i))
        aug_v = pl.BlockSpec((None, N_HEADS, V_AUG_ROWS, tm), lambda b, i: (b, 0, 0, i))
        out_specs = [pos_minor(W_HEAD), pos_minor(W_HEAD), pos_minor(8), aug_t, aug, aug_v]
        out_shape = [jax.ShapeDtypeStruct((nb, W_HEAD, l), F32), jax.ShapeDtypeStruct((nb, W_HEAD, l), F32),
                     jax.ShapeDtypeStruct((nb, 8, l), F32),
                     jax.ShapeDtypeStruct((nb, N_HEADS, 128, l), BF16),
                     jax.ShapeDtypeStruct((nb, N_HEADS, l, 128), BF16),
                     jax.ShapeDtypeStruct((nb, N_HEADS, V_AUG_ROWS, l), BF16)]
        scratch = [pltpu.VMEM((8, 1), F32)]
        body = functools.partial(_fox_prep_prompt_body, tm)
    else:
        in_specs += [pl.BlockSpec((1, HEAD_DIM), const2), pl.BlockSpec((1, HEAD_DIM), const2),
                     pl.BlockSpec((1, 128), const2)]
        params = (gq.reshape(1, HEAD_DIM), gk.reshape(1, HEAD_DIM),
                  jnp.zeros((1, 128), F32).at[0, :N_HEADS].set(bf))
        row_spec = lambda w: pl.BlockSpec((tm, w), lambda b, i: (rows(b, i), 0))
        out_specs = [row_spec(W_HEAD), row_spec(128), row_spec(W_HEAD)]
        out_shape = [jax.ShapeDtypeStruct((m, W_HEAD), F32), jax.ShapeDtypeStruct((m, 128), F32),
                     jax.ShapeDtypeStruct((m, W_HEAD), F32)]
        scratch = []
        body = functools.partial(_fox_prep_sample_body, tm)
    return pl.pallas_call(
        body,
        grid=(nb, nl),
        in_specs=in_specs,
        out_specs=out_specs,
        out_shape=out_shape,
        scratch_shapes=scratch,
        compiler_params=_params("arbitrary", "arbitrary"),
        name="fox_prep_prompt" if prompt else "fox_prep_sample",
    )(proj, proj, proj, proj, *params)


def _fox_attn_body(tq, tk, hps, qt_ref, ka_ref, vt_ref, o_ref, s_ref, m_ref, acc_ref):
    qi = pl.program_id(2)
    n_diag = tq // tk
    heads = range(hps)
    m_ref[...] = jnp.full_like(m_ref, NEG_INF)
    acc_ref[...] = jnp.zeros_like(acc_ref)

    def scores(slot, j):
        k0 = pl.multiple_of(j * tk, tk)
        for h in heads:
            s_ref[slot, h] = _dot(ka_ref[h, pl.ds(k0, tk), :], qt_ref[h])

    def absorb(h, s, v_t, lanes):
        m_old = m_ref[h, :, lanes]
        m_new = jnp.maximum(m_old, jnp.max(s, axis=0, keepdims=True))
        alpha = jnp.exp2(m_old - m_new)
        p = jnp.exp2(s - m_new)
        acc_ref[h, :, lanes] = alpha * acc_ref[h, :, lanes] + _dot(v_t, p.astype(BF16))
        m_ref[h, :, lanes] = m_new

    def consume(slot, j):
        k0 = pl.multiple_of(j * tk, tk)
        for h in heads:
            absorb(h, s_ref[slot, h], vt_ref[h, :, pl.ds(k0, tk)], slice(None))

    scores(0, 0)

    def pair(jj):
        scores(1, 2 * jj + 1)
        consume(0, 2 * jj)
        scores(0, 2 * jj + 2)
        consume(1, 2 * jj + 1)

    def four_pairs(t, carry):
        for u in range(4):
            pair(4 * t + u)
        return carry

    n_pairs = qi * (n_diag // 2)
    lax.fori_loop(0, n_pairs // 4, four_pairs, 0)
    done4 = (n_pairs // 4) * 4

    @pl.when(n_pairs % 4 >= 2)
    def _():
        pair(done4)
        pair(done4 + 1)

    @pl.when(n_pairs % 2 == 1)
    def _():
        pair(n_pairs - 1)

    def key_start(d):
        return pl.multiple_of((qi * n_diag + d) * tk, tk)

    def causal(n):
        return lax.broadcasted_iota(jnp.int32, (tk, n), 0) <= lax.broadcasted_iota(jnp.int32, (tk, n), 1)

    late = {d: [_dot(ka_ref[h, pl.ds(key_start(d), tk), :], qt_ref[h, :, d * tk:tq]) for h in heads]
            for d in range(1, n_diag)}
    for d in range(n_diag):
        for h in heads:
            s = s_ref[0, h] if d == 0 else late[d][h]
            absorb(h, jnp.where(causal(tq - d * tk), s, NEG_INF), vt_ref[h, :, pl.ds(key_start(d), tk)],
                   slice(d * tk, tq))

    outs = []
    for h in heads:
        acc = acc_ref[h]
        outs.append(acc[0:HEAD_DIM, :] / acc[HEAD_DIM:HEAD_DIM + 1, :])
    o_ref[...] = jnp.concatenate(outs, axis=0).T.astype(BF16)


def _fox_attn(qt, ka, vt, tq=512, hps=2):
    nb, nh, l, _ = ka.shape
    tq = min(tq, l)
    tk = tq // 2
    nq = l // tq
    return pl.pallas_call(
        functools.partial(_fox_attn_body, tq, tk, hps),
        grid=(nb, nh // hps, nq),
        in_specs=[pl.BlockSpec((None, hps, 128, tq), lambda b, hp, i: (b, hp, 0, i)),
                  pl.BlockSpec((None, hps, l, 128), lambda b, hp, i: (b, hp, 0, 0)),
                  pl.BlockSpec((None, hps, V_AUG_ROWS, l), lambda b, hp, i: (b, hp, 0, 0))],
        out_specs=pl.BlockSpec((tq, hps * HEAD_DIM), lambda b, hp, i: (b * nq + i, hp)),
        out_shape=jax.ShapeDtypeStruct((nb * l, W_HEAD), BF16),
        scratch_shapes=[pltpu.VMEM((2, hps, tk, tq), F32), pltpu.VMEM((hps, 1, tq), F32),
                        pltpu.VMEM((hps, V_AUG_ROWS, tq), F32)],
        compiler_params=_params("parallel", "parallel", "arbitrary"),
        name="fox_attn",
    )(qt, ka, vt)


def _ret_tables(pos0, l_pad, l_real, chunk):
    c_real = chunk if l_real % chunk == 0 else l_real
    half = HEAD_DIM // 2
    inv = ROPE_BASE ** (-np.arange(half, dtype=np.float64) / half)
    inv_l = np.tile(inv, 2 * N_HEADS)
    sgn = np.tile(np.concatenate([-np.ones(half), np.ones(half)]), N_HEADS)
    n_a = l_pad // chunk
    ang_a = (pos0 + chunk * np.arange(n_a))[:, None] * inv_l[None, :]
    ang_b = np.arange(chunk)[:, None] * inv_l[None, :]
    log_g = np.log1p(-np.exp2(-5.0 - np.arange(N_HEADS, dtype=np.float64)))
    idx = np.arange(chunk, dtype=np.float64)
    diff = idx[:, None] - idx[None, :]
    decay = np.where(diff >= 0, np.exp(np.maximum(diff, 0.0)[None] * log_g[:, None, None]), 0.0)
    lg_l = np.repeat(log_g, HEAD_DIM)
    xi = np.exp((idx[:, None] + 1.0) * lg_l[None, :])
    zeta = np.exp((c_real - 1.0 - idx[:, None]) * lg_l[None, :])
    head = np.arange(W_HEAD) // HEAD_DIM
    bd = (head[:, None] == head[None, :]).astype(np.float64)
    gmat = bd * np.exp(c_real * lg_l)[:, None]
    f = lambda a: jnp.asarray(a, dtype=F32)
    return dict(ca=f(np.cos(ang_a)), sa=f(sgn * np.sin(ang_a)), cb=f(np.cos(ang_b)), sb=f(sgn * np.sin(ang_b)),
                decay=f(decay), xi=f(xi), zeta=f(zeta), gmat=f(gmat), bd=f(bd),
                bdavg=jnp.asarray(bd / HEAD_DIM, dtype=BF16))


def _ret_body(tm, ch, n_seq, q_ref, k_ref, v_ref, g_ref, ca_ref, sa_ref, cb_ref, sb_ref, dec_ref, xi_ref,
              zeta_ref, gmat_ref, bd_ref, bdavg_ref, gn_ref, s0_ref, y_ref, sout_ref, st_ref):
    i = pl.program_id(1)
    n_chunk = tm // ch

    @pl.when(i == 0)
    def _():
        for s in range(n_seq):
            st_ref[s] = jnp.concatenate([s0_ref[s]] * N_HEADS, axis=-1) * bd_ref[...]

    lane = lax.broadcasted_iota(jnp.int32, (ch, W_HEAD), 1)
    lane128 = lax.broadcasted_iota(jnp.int32, (ch, 128), 1)
    first_half = (lane128 % HEAD_DIM) < (HEAD_DIM // 2)

    def swap_halves(x):
        parts = []
        for blk in range(W_HEAD // 128):
            xb = x[:, blk * 128:(blk + 1) * 128]
            parts.append(jnp.where(first_half, pltpu.roll(xb, 128 - HEAD_DIM // 2, 1),
                                   pltpu.roll(xb, HEAD_DIM // 2, 1)))
        return jnp.concatenate(parts, axis=-1)

    def gmean(t):
        hi, lo = _split2(t)
        return _dot(hi, bdavg_ref[...]) + _dot(lo, bdavg_ref[...])

    def chunk(s, c):
        r0 = s * tm + c * ch
        a = i * n_chunk + c
        c_a = ca_ref[pl.ds(a, 1), :]
        s_a = sa_ref[pl.ds(a, 1), :]
        cos = c_a * cb_ref[...] - s_a * sb_ref[...]
        sin = s_a * cb_ref[...] + c_a * sb_ref[...]
        q = q_ref[pl.ds(r0, ch), :]
        k = k_ref[pl.ds(r0, ch), :]
        q = q * cos + swap_halves(q) * sin
        k = (k * cos + swap_halves(k) * sin) * FOX_SCALE
        qb = q.astype(BF16)
        kb = k.astype(BF16)
        vb = v_ref[pl.ds(r0, ch), :].astype(BF16)
        st = st_ref[s]
        o = _dot(qb, st.astype(BF16)) * xi_ref[...]
        for h in range(N_HEADS):
            mh = (lane // HEAD_DIM) == h
            qm = jnp.where(mh, q, 0.0).astype(BF16)
            att = _dot_nt(qm, kb) * dec_ref[h]
            o = o + jnp.where(mh, _dot(att.astype(BF16), vb), 0.0)
        kz = (k * zeta_ref[...]).astype(BF16)
        st_ref[s] = st * gmat_ref[...] + _dot_tn(kz, vb) * bd_ref[...]
        mu = gmean(o)
        d = o - mu
        var = gmean(d * d)
        on = d * lax.rsqrt(var + EPS) * gn_ref[...]
        gate = g_ref[pl.ds(r0, ch), :]
        y_ref[pl.ds(r0, ch), :] = (on * (gate * _sigmoid(gate))).astype(BF16)

    for c in range(n_chunk):
        for s in range(n_seq):
            chunk(s, c)

    @pl.when(i == pl.num_programs(1) - 1)
    def _():
        for s in range(n_seq):
            st = st_ref[s]
            acc = st[:, 0:HEAD_DIM]
            for h in range(1, N_HEADS):
                acc = acc + st[:, h * HEAD_DIM:(h + 1) * HEAD_DIM]
            sout_ref[s] = acc


def _retention(proj, gn, s0, nb, l, l_real, pos0, tm=512, n_seq=1):
    m = proj.shape[0]
    tm = min(tm, l)
    ch = min(RET_CHUNK, tm)
    nl = l // tm
    assert n_seq == 1 or (nl == 1 and nb % n_seq == 0)
    t = _ret_tables(pos0, l, l_real, ch)
    rows = lambda b, i: b * nl + i
    const2 = lambda b, i: (0, 0)
    col = lambda cidx: pl.BlockSpec((n_seq * tm, W_HEAD), lambda b, i: (rows(b, i), cidx))
    full = lambda a: pl.BlockSpec(a.shape, (lambda b, i: (0,) * a.ndim))
    in_specs = [col(COL_RQ), col(COL_RK), col(COL_RV), col(COL_RG),
                full(t["ca"]), full(t["sa"]), full(t["cb"]), full(t["sb"]), full(t["decay"]),
                full(t["xi"]), full(t["zeta"]), full(t["gmat"]), full(t["bd"]), full(t["bdavg"]),
                pl.BlockSpec((1, W_HEAD), const2),
                pl.BlockSpec((n_seq, W_HEAD, HEAD_DIM), lambda b, i: (b, 0, 0))]
    return pl.pallas_call(
        functools.partial(_ret_body, tm, ch, n_seq),
        grid=(nb // n_seq, nl),
        in_specs=in_specs,
        out_specs=[pl.BlockSpec((n_seq * tm, W_HEAD), lambda b, i: (rows(b, i), 0)),
                   pl.BlockSpec((n_seq, W_HEAD, HEAD_DIM), lambda b, i: (b, 0, 0))],
        out_shape=[jax.ShapeDtypeStruct((m, W_HEAD), BF16),
                   jax.ShapeDtypeStruct((nb, W_HEAD, HEAD_DIM), F32)],
        scratch_shapes=[pltpu.VMEM((n_seq, W_HEAD, W_HEAD), F32)],
        compiler_params=_params("arbitrary", "arbitrary"),
        name="retention",
    )(proj, proj, proj, proj, t["ca"], t["sa"], t["cb"], t["sb"], t["decay"], t["xi"], t["zeta"],
      t["gmat"], t["bd"], t["bdavg"], gn.reshape(1, W_HEAD), s0)


def _lru_body(tm, l_real, lx_ref, lg_ref, cw_ref, cb_ref, wr_ref, wi_ref, br_ref, bi_ref, lam_ref,
              conv0_ref, h0_ref, y_ref, hlast_ref, convnew_ref, xpad_ref, a_ref, b_ref, h_ref, hcar_ref):
    i = pl.program_id(1)
    w = lx_ref.shape[1]
    t_last, r_last = (l_real - 1) // tm, (l_real - 1) % tm

    @pl.when(i == 0)
    def _():
        xpad_ref[0:8, :] = conv0_ref[...]
        hcar_ref[...] = h0_ref[...]

    x = lx_ref[...]
    xpad_ref[8:8 + tm, :] = x
    xc = cb_ref[...] + cw_ref[CONV_W - 1:CONV_W, :] * x
    for j in range(1, CONV_W):
        xc = xc + cw_ref[CONV_W - 1 - j:CONV_W - j, :] * xpad_ref[pl.ds(8 - j, tm), :]

    xb = xc.astype(BF16)
    r = _sigmoid(_dot(xb, wr_ref[...]) + br_ref[...])
    ig = _sigmoid(_dot(xb, wi_ref[...]) + bi_ref[...])
    z = -lam_ref[...]
    softplus = jnp.maximum(z, 0.0) + jnp.log1p(jnp.exp(-jnp.abs(z)))
    log_a = (-LRU_C) * r * softplus
    a = jnp.exp(log_a)
    th = jnp.tanh(log_a)
    em = -2.0 * th / (1.0 - th)
    u = jnp.where(em > 0.0, em * lax.rsqrt(em), 0.0) * (ig * xc)

    a = a.reshape(tm // 8, 8, w)
    u = u.reshape(tm // 8, 8, w)
    rowmod = lax.broadcasted_iota(jnp.int32, (tm // 8, 8, w), 1)
    for sh in (1, 2, 4):
        a_sh = pltpu.roll(a, sh, 1)
        u_sh = pltpu.roll(u, sh, 1)
        valid = rowmod >= sh
        u = jnp.where(valid, a * u_sh + u, u)
        a = jnp.where(valid, a * a_sh, a)
    a_ref[...] = a.reshape(tm, w)
    b_ref[...] = u.reshape(tm, w)

    def group(gi, h):
        r0 = pl.multiple_of(gi * 8, 8)
        hg = a_ref[pl.ds(r0, 8), :] * h + b_ref[pl.ds(r0, 8), :]
        h_ref[pl.ds(r0, 8), :] = hg
        return hg[7:8, :]

    hcar_ref[...] = lax.fori_loop(0, tm // 8, group, hcar_ref[...])

    g = lg_ref[...]
    gelu = 0.5 * g * (1.0 + jnp.tanh(0.7978845608028654 * (g + 0.044715 * (g * g * g))))
    y_ref[...] = (h_ref[...] * gelu).astype(BF16)

    @pl.when(i == t_last)
    def _():
        hlast_ref[...] = h_ref[r_last:r_last + 1, :]
        convnew_ref[...] = xpad_ref[pl.ds(r_last + 1, 8), :]

    xpad_ref[0:8, :] = xpad_ref[tm:tm + 8, :]


def _lru(proj, cw, cb, wr, wi, br, bi, lam, conv0, h0, nb, l, l_real, tm=512):
    m = proj.shape[0]
    w = cw.shape[1]
    tm = min(tm, l)
    nl = l // tm
    rows = lambda b, i: b * nl + i
    const2 = lambda b, i: (0, 0)
    vec = pl.BlockSpec((1, w), const2)
    per_b = lambda r: pl.BlockSpec((None, r, w), lambda b, i: (b, 0, 0))
    return pl.pallas_call(
        functools.partial(_lru_body, tm, l_real),
        grid=(nb, nl),
        in_specs=[pl.BlockSpec((tm, w), lambda b, i: (rows(b, i), COL_LX)),
                  pl.BlockSpec((tm, w), lambda b, i: (rows(b, i), COL_LG)),
                  pl.BlockSpec((CONV_W, w), const2), vec,
                  pl.BlockSpec((w, w), const2), pl.BlockSpec((w, w), const2), vec, vec, vec,
                  per_b(8), per_b(1)],
        out_specs=[pl.BlockSpec((tm, w), lambda b, i: (rows(b, i), 0)), per_b(1), per_b(8)],
        out_shape=[jax.ShapeDtypeStruct((m, w), BF16), jax.ShapeDtypeStruct((nb, 1, w), F32),
                   jax.ShapeDtypeStruct((nb, 8, w), F32)],
        scratch_shapes=[pltpu.VMEM((tm + 8, w), F32), pltpu.VMEM((tm, w), F32), pltpu.VMEM((tm, w), F32),
                        pltpu.VMEM((tm, w), F32), pltpu.VMEM((1, w), F32)],
        compiler_params=_params("arbitrary", "arbitrary"),
        name="lru",
    )(proj, proj, cw, cb.reshape(1, w), wr, wi, br.reshape(1, w), bi.reshape(1, w), lam.reshape(1, w),
      conv0, h0)


def _decode_body(n_pg, layer, pt_ref, qn_ref, knew_ref, vnew_ref, lfnew_ref, kt_hbm, vt_hbm, lf_hbm,
                 o_ref, kbuf, vbuf, lfbuf, sem, m_ref, l_ref, acc_ref, car_ref, qm_ref):
    b_i = pl.program_id(0)
    p_i = pl.program_id(1)
    n_steps = pl.num_programs(1)
    t = b_i * n_steps + p_i
    slot = lax.rem(t, 2)
    page = kbuf.shape[3]

    def page_copies(pages, sl):
        cps = []
        for g in range(n_pg):
            pg = pages(g)
            cps.append(pltpu.make_async_copy(kt_hbm.at[layer, pg], kbuf.at[sl, g], sem.at[0, sl]))
            cps.append(pltpu.make_async_copy(vt_hbm.at[layer, pg], vbuf.at[sl, g], sem.at[1, sl]))
            cps.append(pltpu.make_async_copy(lf_hbm.at[layer, pg], lfbuf.at[sl, g], sem.at[2, sl]))
        return cps

    @pl.when(t == 0)
    def _():
        for cp in page_copies(lambda g: pt_ref[0, g], 0):
            cp.start()

    @pl.when(t + 1 < pl.num_programs(0) * n_steps)
    def _():
        wrap = p_i + 1 == n_steps
        b_n = jnp.where(wrap, b_i + 1, b_i)
        p_n = jnp.where(wrap, 0, p_i + 1)
        for cp in page_copies(lambda g: pt_ref[b_n, p_n * n_pg + g], 1 - slot):
            cp.start()

    for cp in page_copies(lambda g: 0, slot):
        cp.wait()
    k_refs = [kbuf.at[slot, g] for g in range(n_pg)]
    v_refs = [vbuf.at[slot, g] for g in range(n_pg)]
    lf_refs = [lfbuf.at[slot, g] for g in range(n_pg)]

    row = lax.broadcasted_iota(jnp.int32, (8, W_HEAD), 0)
    lane = lax.broadcasted_iota(jnp.int32, (8, W_HEAD), 1)
    own = (lane // HEAD_DIM) == row

    @pl.when(p_i == 0)
    def _():
        q = jnp.where(own, jnp.broadcast_to(qn_ref[0:1, :] * FOX_SCALE, (8, W_HEAD)), 0.0)
        qm_ref[...] = _pad16(q).astype(BF16)
        m_ref[...] = jnp.full_like(m_ref, NEG_INF)
        l_ref[...] = jnp.zeros_like(l_ref)
        acc_ref[...] = jnp.zeros_like(acc_ref)
        car_ref[...] = jnp.zeros_like(car_ref)

    qm = qm_ref[...]

    def update(s, pv_fn):
        m_old = m_ref[...]
        m_new = jnp.maximum(m_old, jnp.max(s, axis=-1, keepdims=True))
        alpha = jnp.exp(m_old - m_new)
        p = jnp.exp(s - m_new)
        l_ref[...] = alpha * l_ref[...] + jnp.sum(p, axis=-1, keepdims=True)
        acc_ref[...] = alpha * acc_ref[...] + pv_fn(p)
        m_ref[...] = m_new

    r_i = lax.broadcasted_iota(jnp.int32, (page, page), 0)
    c_i = lax.broadcasted_iota(jnp.int32, (page, page), 1)
    upper = jnp.where(r_i <= c_i, 1.0, 0.0).astype(BF16)
    hi, mid, lo = _split3(jnp.concatenate([lf_refs[g][...] for g in range(n_pg)], axis=0))
    c_all = _dot(hi, upper) + _dot(mid, upper) + _dot(lo, upper)
    nr = n_pg * 8
    rr = lax.broadcasted_iota(jnp.int32, (nr, nr), 0)
    cc = lax.broadcasted_iota(jnp.int32, (nr, nr), 1)
    before = jnp.where((rr % 8 == cc % 8) & (cc // 8 < rr // 8), 1.0, 0.0).astype(BF16)
    chi, cmid, clo = _split3(c_all)
    pref = _dot(before, chi) + _dot(before, cmid) + _dot(before, clo)
    c_all = c_all + pref[:, page - 1:page] + jnp.concatenate([car_ref[...]] * n_pg, axis=0)
    car_ref[...] = c_all[nr - 8:nr, page - 1:page]
    s_parts = [_dot(qm, k_refs[g][...].astype(BF16))[0:8, :] - c_all[g * 8:(g + 1) * 8, :]
               for g in range(n_pg)]

    def pv_pages(p):
        pb = _pad16(p).astype(BF16)
        pv = _dot_nt(pb[:, 0:page], v_refs[0][...].astype(BF16))
        for g in range(1, n_pg):
            pv = pv + _dot_nt(pb[:, g * page:(g + 1) * page], v_refs[g][...].astype(BF16))
        return pv[0:8, :]

    update(jnp.concatenate(s_parts, axis=-1), pv_pages)

    @pl.when(p_i == pl.num_programs(1) - 1)
    def _():
        lane128 = lax.broadcasted_iota(jnp.int32, (8, 128), 1)
        row128 = lax.broadcasted_iota(jnp.int32, (8, 128), 0)
        lf_col = jnp.sum(jnp.where(lane128 == row128, jnp.broadcast_to(lfnew_ref[0:1, :], (8, 128)), 0.0),
                         axis=-1, keepdims=True)
        k_new = knew_ref[0:1, :].astype(BF16).astype(F32)
        v_new = vnew_ref[0:1, :].astype(BF16).astype(F32)
        s_new = jnp.sum(qm[0:8, :].astype(F32) * k_new, axis=-1, keepdims=True) - (car_ref[...] + lf_col)
        update(s_new, lambda p: p.astype(BF16).astype(F32) * v_new)
        o_ref[...] = jnp.sum(jnp.where(own, acc_ref[...] / l_ref[...], 0.0), axis=0, keepdims=True)


def _fox_decode(layer, page_table, qn, kn, proj, logf, cache_kt, cache_vt, cache_lft, nb):
    n_pages = page_table.shape[1]
    n_pg = min(PAGES_PER_STEP, n_pages)
    sr = SAMPLE_ROWS
    page = cache_kt.shape[3]
    tok = lambda cidx: (lambda b, p, pt: (b, cidx))
    hbm = pl.BlockSpec(memory_space=pl.ANY)
    in_specs = [pl.BlockSpec((sr, W_HEAD), tok(0)), pl.BlockSpec((sr, W_HEAD), tok(0)),
                pl.BlockSpec((sr, W_HEAD), tok(COL_FV)), pl.BlockSpec((sr, 128), tok(0)),
                hbm, hbm, hbm]
    grid_spec = pltpu.PrefetchScalarGridSpec(
        num_scalar_prefetch=1,
        grid=(nb, n_pages // n_pg),
        in_specs=in_specs,
        out_specs=pl.BlockSpec((None, 1, W_HEAD), lambda b, p, pt: (b, 0, 0)),
        scratch_shapes=[pltpu.VMEM((2, n_pg, W_HEAD, page), F32), pltpu.VMEM((2, n_pg, W_HEAD, page), F32),
                        pltpu.VMEM((2, n_pg, 8, page), F32), pltpu.SemaphoreType.DMA((3, 2)),
                        pltpu.VMEM((8, 1), F32), pltpu.VMEM((8, 1), F32), pltpu.VMEM((8, W_HEAD), F32),
                        pltpu.VMEM((8, 1), F32), pltpu.VMEM((16, W_HEAD), BF16)])
    return pl.pallas_call(
        functools.partial(_decode_body, n_pg, layer),
        grid_spec=grid_spec,
        out_shape=jax.ShapeDtypeStruct((nb, 1, W_HEAD), F32),
        compiler_params=_params("arbitrary", "arbitrary"),
        name="fox_decode",
    )(page_table, qn, kn, proj, logf, cache_kt, cache_vt, cache_lft)


def _block_diag(w):
    n, d, e = w.shape
    eye = jnp.eye(n, dtype=w.dtype)
    return (eye[:, None, :, None] * w[:, :, None, :]).reshape(n * d, n * e)


def _mixer_common(proj, nb, l, l_real, pos0, s0, conv0, h0, lw):
    n_seq = 4 if (l <= RET_CHUNK and nb % 4 == 0) else 1
    y_ret, s_new = _retention(proj, lw["ret_gn"], s0, nb, l, l_real, pos0, n_seq=n_seq)
    y_lru, h_last, conv_new = _lru(proj, lw["conv_w"], lw["conv_b"], lw["wr"], lw["wi"], lw["br"], lw["bi"],
                                   lw["lam"], conv0, h0, nb, l, l_real)
    return y_ret, y_lru, s_new, h_last, conv_new


def kernel(x_prompt, x_sample, cache_k, cache_v, cache_logf, state_ret, state_lru, state_conv, page_table, norm_ffn1, ffn1_gate, ffn1_up, ffn1_down, norm_mix, w_in, ret_gn, conv_w, conv_b, lru_wr, lru_br, lru_wi, lru_bi, lru_lambda, fox_qn, fox_kn, fox_bf, w_out, norm_ffn2, ffn2_gate, ffn2_up, ffn2_down):
    bp, lp, d = x_prompt.shape
    bs, ls, _ = x_sample.shape
    assert ls == 1
    depth = w_in.shape[0]
    n_pool, page = cache_k.shape[1], cache_k.shape[2]
    past_len = page_table.shape[1] * page
    w_lru = conv_w.shape[2]
    sr = SAMPLE_ROWS
    in_width = w_in.shape[2]

    g1, u1, d1 = ffn1_gate.astype(BF16), ffn1_up.astype(BF16), ffn1_down.astype(BF16)
    g2, u2, d2 = ffn2_gate.astype(BF16), ffn2_up.astype(BF16), ffn2_down.astype(BF16)
    w_out_b = w_out.astype(BF16)
    w_in_b = jnp.pad(w_in, ((0, 0), (0, 0), (0, PROJ_W - in_width))).astype(BF16)
    cache_k4 = jnp.transpose(cache_k, (0, 1, 3, 4, 2)).reshape(depth, n_pool, W_HEAD, page)
    cache_v4 = jnp.transpose(cache_v, (0, 1, 3, 4, 2)).reshape(depth, n_pool, W_HEAD, page)
    cache_lft = jnp.pad(jnp.swapaxes(cache_logf, 2, 3), ((0, 0), (0, 0), (0, 8 - N_HEADS), (0, 0)))

    xp = x_prompt.reshape(bp * lp, d)
    xs = x_sample.reshape(bs, d)
    zero_s = jnp.zeros((bp, W_HEAD, HEAD_DIM), F32)
    zero_c = jnp.zeros((bp, 8, w_lru), F32)
    zero_h = jnp.zeros((bp, 1, w_lru), F32)

    outs_p = [[] for _ in range(6)]
    outs_s = [[] for _ in range(6)]
    for l in range(depth):
        lw = dict(ret_gn=ret_gn[l], conv_w=conv_w[l], conv_b=conv_b[l],
                  wr=_block_diag(lru_wr[l]).astype(BF16), wi=_block_diag(lru_wi[l]).astype(BF16),
                  br=lru_br[l], bi=lru_bi[l], lam=lru_lambda[l])

        xp, proj = _ffn(xp, norm_ffn1[l], g1, u1, d1, l, proj=(norm_mix[l], w_in_b))
        kn_t, v_t, logf_t, qa, ka, va = _fox_prep(proj, fox_qn[l], fox_kn[l], fox_bf[l], bp, lp, True)
        y_fox = _fox_attn(qa, ka, va)
        y_ret, y_lru, s_new, h_last, conv_new = _mixer_common(proj, bp, lp, lp, 0, zero_s, zero_c, zero_h, lw)
        xp = _ffn(xp, norm_ffn2[l], g2, u2, d2, l, mix=(y_ret, y_lru, y_fox, w_out_b))
        heads_last = lambda a: jnp.transpose(a.reshape(bp, N_HEADS, HEAD_DIM, lp), (0, 3, 1, 2))
        outs_p[0].append(heads_last(kn_t))
        outs_p[1].append(heads_last(v_t))
        outs_p[2].append(jnp.transpose(logf_t[:, :N_HEADS, :], (0, 2, 1)))
        outs_p[3].append(s_new.reshape(bp, N_HEADS, HEAD_DIM, HEAD_DIM))
        outs_p[4].append(h_last.reshape(bp, w_lru))
        outs_p[5].append(conv_new[:, 8 - (CONV_W - 1):, :])

        xs = _ffn(xs, norm_ffn1[l], g1, u1, d1, l)
        xs_pad = jnp.pad(xs[:, None, :], ((0, 0), (0, sr - 1), (0, 0))).reshape(bs * sr, d)
        proj_s = _inproj(xs_pad, norm_mix[l], w_in_b, l)
        kn_s, logf_s, qn_s = _fox_prep(proj_s, fox_qn[l], fox_kn[l], fox_bf[l], 1, bs * sr, False)
        y_fox_s = _fox_decode(l, page_table, qn_s, kn_s, proj_s, logf_s, cache_k4, cache_v4, cache_lft, bs)
        s0 = state_ret[l].reshape(bs, W_HEAD, HEAD_DIM)
        conv0 = jnp.pad(state_conv[l], ((0, 0), (8 - (CONV_W - 1), 0), (0, 0)))
        h0 = state_lru[l].reshape(bs, 1, w_lru)
        y_ret_s, y_lru_s, s_new_s, h_last_s, conv_new_s = _mixer_common(
            proj_s, bs, sr, 1, past_len, s0, conv0, h0, lw)
        first = lambda a: a.reshape(bs, sr, a.shape[-1])[:, 0, :]
        xs = _ffn(xs, norm_ffn2[l], g2, u2, d2, l,
                  mix=(first(y_ret_s), first(y_lru_s), y_fox_s.reshape(bs, W_HEAD).astype(BF16), w_out_b))
        outs_s[0].append(first(kn_s).reshape(bs, 1, N_HEADS, HEAD_DIM))
        outs_s[1].append(first(proj_s)[:, COL_FV * W_HEAD:(COL_FV + 1) * W_HEAD].reshape(bs, 1, N_HEADS, HEAD_DIM))
        outs_s[2].append(first(logf_s)[:, :N_HEADS].reshape(bs, 1, N_HEADS))
        outs_s[3].append(s_new_s.reshape(bs, N_HEADS, HEAD_DIM, HEAD_DIM))
        outs_s[4].append(h_last_s.reshape(bs, w_lru))
        outs_s[5].append(conv_new_s[:, 8 - (CONV_W - 1):, :])

    stk = lambda lst: jnp.stack(lst, axis=0)
    return (xp.reshape(bp, lp, d), xs.reshape(bs, 1, d),
            *[stk(o) for o in outs_p], *[stk(o) for o in outs_s])
```

```python
import functools

import numpy as np
import jax
import jax.numpy as jnp
from jax import lax
from jax.experimental import pallas as pl
from jax.experimental.pallas import tpu as pltpu

F32 = jnp.float32
BF16 = jnp.bfloat16

HEAD_DIM = 64
N_HEADS = 4
W_HEAD = N_HEADS * HEAD_DIM
CONV_W = 4
LRU_C = 8.0
RET_CHUNK = 256
ROPE_BASE = 10000.0
EPS = 1e-6
NEG_INF = -1e30
FOX_SCALE = HEAD_DIM ** -0.5
LOG2E = 1.4426950408889634
V_AUG_ROWS = 80
SAMPLE_ROWS = 16
PAGES_PER_STEP = 32
VMEM_LIMIT = 48 * 1024 * 1024
VMEM_LIMIT_FUSED = 58 * 1024 * 1024

COL_RQ, COL_RK, COL_RV, COL_RG = 0, 1, 2, 3
COL_LX, COL_LG = 2, 3
COL_FQ, COL_FK, COL_FV = 8, 9, 10
COL_FF = 22
PROJ_W = 23 * 128


def _dot(a, b):
    return jnp.dot(a, b, preferred_element_type=F32)


def _dot_nt(a, b):
    return lax.dot_general(a, b, (((1,), (1,)), ((), ())), preferred_element_type=F32)


def _dot_tn(a, b):
    return lax.dot_general(a, b, (((0,), (0,)), ((), ())), preferred_element_type=F32)


def _sigmoid(x):
    return 0.5 * jnp.tanh(0.5 * x) + 0.5


def _split2(x):
    hi = x.astype(BF16)
    lo = (x - hi.astype(F32)).astype(BF16)
    return hi, lo


def _split3(x):
    hi = x.astype(BF16)
    r = x - hi.astype(F32)
    mid = r.astype(BF16)
    lo = (r - mid.astype(F32)).astype(BF16)
    return hi, mid, lo


def _params(*sem):
    return pltpu.CompilerParams(dimension_semantics=sem, vmem_limit_bytes=VMEM_LIMIT)


def _ffn_body(has_mix, has_proj, tf, *refs):
    refs = list(refs)
    x_ref = refs.pop(0)
    if has_mix:
        yr_ref, yl_ref, yf_ref, wo_ref = refs[:4]
        del refs[:4]
    g_ref, wg_ref, wu_ref, wd_ref = refs[:4]
    del refs[:4]
    if has_proj:
        gm_ref, win_ref = refs[:2]
        del refs[:2]
    o_ref = refs.pop(0)
    if has_proj:
        proj_ref = refs.pop(0)
    xres_ref, xn_ref, hid_ref = refs

    x = x_ref[...]
    if has_mix:
        w1 = yr_ref.shape[1]
        w2 = w1 + yl_ref.shape[1]
        x = (x + _dot(yr_ref[...], wo_ref[0:w1, :]) + _dot(yl_ref[...], wo_ref[w1:w2, :])
             + _dot(yf_ref[...], wo_ref[w2:, :]))
    xres_ref[...] = x
    ms = jnp.mean(x * x, axis=-1, keepdims=True)
    xn_ref[...] = (x * lax.rsqrt(ms + EPS) * g_ref[...]).astype(BF16)

    xn = xn_ref[...]
    for j in range(wg_ref.shape[1] // tf):
        gate = _dot(xn, wg_ref[:, j * tf:(j + 1) * tf])
        up = _dot(xn, wu_ref[:, j * tf:(j + 1) * tf])
        hid_ref[:, j * tf:(j + 1) * tf] = (gate * _sigmoid(gate) * up).astype(BF16)
    out = xres_ref[...] + 0.5 * _dot(hid_ref[...], wd_ref[...])
    o_ref[...] = out
    if has_proj:
        ms_o = jnp.mean(out * out, axis=-1, keepdims=True)
        proj_ref[...] = _dot((out * lax.rsqrt(ms_o + EPS) * gm_ref[...]).astype(BF16), win_ref[...])


def _ffn(x, g, wg, wu, wd, layer, mix=None, proj=None, tm=512, tf=256):
    m, d = x.shape
    f = wg.shape[2]
    tm = min(tm, m)
    row = lambda i: (i, 0)
    resident = lambda a: pl.BlockSpec((None,) + a.shape[1:], lambda i: (layer, 0, 0),
                                      pipeline_mode=pl.Buffered(1))
    vec = pl.BlockSpec((1, d), lambda i: (0, 0), pipeline_mode=pl.Buffered(1))
    in_specs = [pl.BlockSpec((tm, d), row)]
    args = [x]
    if mix is not None:
        yr, yl, yf, wo = mix
        in_specs += [pl.BlockSpec((tm, yr.shape[1]), row), pl.BlockSpec((tm, yl.shape[1]), row),
                     pl.BlockSpec((tm, yf.shape[1]), row), resident(wo)]
        args += [yr, yl, yf, wo]
    in_specs += [vec, resident(wg), resident(wu), resident(wd)]
    args += [g.reshape(1, d), wg, wu, wd]
    out_specs = [pl.BlockSpec((tm, d), row)]
    out_shape = [jax.ShapeDtypeStruct((m, d), F32)]
    vmem = VMEM_LIMIT
    if proj is not None:
        g_mix, w_in = proj
        in_specs += [vec, resident(w_in)]
        args += [g_mix.reshape(1, d), w_in]
        out_specs.append(pl.BlockSpec((tm, w_in.shape[2]), row))
        out_shape.append(jax.ShapeDtypeStruct((m, w_in.shape[2]), F32))
        vmem = VMEM_LIMIT_FUSED
    res = pl.pallas_call(
        functools.partial(_ffn_body, mix is not None, proj is not None, tf),
        grid=(m // tm,),
        in_specs=in_specs,
        out_specs=out_specs,
        out_shape=out_shape,
        scratch_shapes=[pltpu.VMEM((tm, d), F32), pltpu.VMEM((tm, d), BF16), pltpu.VMEM((tm, f), BF16)],
        compiler_params=pltpu.CompilerParams(dimension_semantics=("parallel",), vmem_limit_bytes=vmem),
        name="ffn_mix" if mix is not None else ("ffn_proj" if proj is not None else "ffn"),
    )(*args)
    return res if proj is not None else res[0]


def _head_norm(x, g):
    outs = []
    for h in range(N_HEADS):
        xh = x[:, h * HEAD_DIM:(h + 1) * HEAD_DIM]
        ms = jnp.mean(xh * xh, axis=-1, keepdims=True)
        outs.append(xh * lax.rsqrt(ms + EPS) * g)
    return outs


def _pad16(x):
    return jnp.concatenate([x, jnp.zeros_like(x)], axis=0)


def _fox_prep_sample_body(tm, fq_ref, fk_ref, fv_ref, ff_ref, gq_ref, gk_ref, bf_ref,
                          kn_ref, logf_ref, qn_ref):
    del fv_ref
    lane = lax.broadcasted_iota(jnp.int32, (tm, 128), 1)
    z = ff_ref[...] + bf_ref[...]
    logf = jnp.minimum(z, 0.0) - jnp.log1p(jnp.exp(-jnp.abs(z)))
    logf_ref[...] = jnp.where(lane < N_HEADS, logf, 0.0)
    kn_ref[...] = jnp.concatenate(_head_norm(fk_ref[...], gk_ref[...]), axis=-1)
    qn_ref[...] = jnp.concatenate(_head_norm(fq_ref[...], gq_ref[...]), axis=-1)


def _fox_prep_prompt_body(tm, fq_ref, fk_ref, fv_ref, ff_ref, gq_ref, gk_ref, bf_ref,
                          knt_ref, vt_ref, lft_ref, qa_ref, ka_ref, va_ref, carry_ref):
    @pl.when(pl.program_id(1) == 0)
    def _():
        carry_ref[...] = jnp.zeros_like(carry_ref)

    q_t = fq_ref[...].T
    k_t = fk_ref[...].T
    v_t = fv_ref[...].T
    z = ff_ref[...].T[0:8, :] + bf_ref[...]
    lf = jnp.minimum(z, 0.0) - jnp.log1p(jnp.exp(-jnp.abs(z)))
    lft_ref[...] = lf
    vt_ref[...] = v_t

    r_i = lax.broadcasted_iota(jnp.int32, (tm, tm), 0)
    c_i = lax.broadcasted_iota(jnp.int32, (tm, tm), 1)
    upper = jnp.where(r_i <= c_i, 1.0, 0.0).astype(BF16)
    hi, mid, lo = _split3(_pad16(lf))
    cs = (_dot(hi, upper) + _dot(mid, upper) + _dot(lo, upper))[0:8, :] + carry_ref[...]
    carry_ref[...] = cs[:, tm - 1:tm]
    cs = cs * LOG2E

    sub = lax.broadcasted_iota(jnp.int32, (HEAD_DIM, tm), 0)
    v_tail = jnp.where(lax.broadcasted_iota(jnp.int32, (V_AUG_ROWS - HEAD_DIM, tm), 0) == 0, 1.0, 0.0)
    kn_parts = []
    for h in range(N_HEADS):
        rows = slice(h * HEAD_DIM, (h + 1) * HEAD_DIM)
        qh, kh = q_t[rows], k_t[rows]
        qn = qh * lax.rsqrt(jnp.mean(qh * qh, axis=0, keepdims=True) + EPS) * gq_ref[...]
        kn = kh * lax.rsqrt(jnp.mean(kh * kh, axis=0, keepdims=True) + EPS) * gk_ref[...]
        kn_parts.append(kn)
        ch = cs[h:h + 1, :]
        chi = ch.astype(BF16).astype(F32)
        r1 = ch - chi
        cmid = r1.astype(BF16).astype(F32)
        clo = r1 - cmid
        q_tail = jnp.where(sub == 0, chi, jnp.where(sub == 1, cmid, jnp.where(
            sub == 2, clo, jnp.where(sub < 6, 1.0, 0.0))))
        k_tail = jnp.where(sub < 3, 1.0, jnp.where(sub == 3, -chi, jnp.where(
            sub == 4, -cmid, jnp.where(sub == 5, -clo, 0.0))))
        qa_ref[h] = jnp.concatenate([qn * (FOX_SCALE * LOG2E), q_tail], axis=0).astype(BF16)
        ka_ref[h] = jnp.concatenate([kn, k_tail], axis=0).T.astype(BF16)
        va_ref[h] = jnp.concatenate([v_t[rows], v_tail], axis=0).astype(BF16)
    knt_ref[...] = jnp.concatenate(kn_parts, axis=0)


def _fox_prep(proj, gq, gk, bf, nb, l, prompt, tm=512):
    m = proj.shape[0]
    tm = min(tm, l)
    nl = l // tm
    rows = lambda b, i: b * nl + i
    const2 = lambda b, i: (0, 0)
    in_specs = [pl.BlockSpec((tm, W_HEAD), lambda b, i: (rows(b, i), COL_FQ)),
                pl.BlockSpec((tm, W_HEAD), lambda b, i: (rows(b, i), COL_FK)),
                pl.BlockSpec((tm, W_HEAD), lambda b, i: (rows(b, i), COL_FV)),
                pl.BlockSpec((tm, 128), lambda b, i: (rows(b, i), COL_FF))]
    if prompt:
        in_specs += [pl.BlockSpec((HEAD_DIM, 1), const2), pl.BlockSpec((HEAD_DIM, 1), const2),
                     pl.BlockSpec((8, 1), const2)]
        params = (gq.reshape(HEAD_DIM, 1), gk.reshape(HEAD_DIM, 1),
                  jnp.zeros((8, 1), F32).at[:N_HEADS, 0].set(bf))
        pos_minor = lambda r: pl.BlockSpec((None, r, tm), lambda b, i: (b, 0, i))
        aug = pl.BlockSpec((None, N_HEADS, tm, 128), lambda b, i: (b, 0, i, 0))
        aug_t = pl.BlockSpec((None, N_HEADS, 128, tm), lambda b, i: (b, 0, 0, i))
        aug_v = pl.BlockSpec((None, N_HEADS, V_AUG_ROWS, tm), lambda b, i: (b, 0, 0, i))
        out_specs = [pos_minor(W_HEAD), pos_minor(W_HEAD), pos_minor(8), aug_t, aug, aug_v]
        out_shape = [jax.ShapeDtypeStruct((nb, W_HEAD, l), F32), jax.ShapeDtypeStruct((nb, W_HEAD, l), F32),
                     jax.ShapeDtypeStruct((nb, 8, l), F32),
                     jax.ShapeDtypeStruct((nb, N_HEADS, 128, l), BF16),
                     jax.ShapeDtypeStruct((nb, N_HEADS, l, 128), BF16),
                     jax.ShapeDtypeStruct((nb, N_HEADS, V_AUG_ROWS, l), BF16)]
        scratch = [pltpu.VMEM((8, 1), F32)]
        body = functools.partial(_fox_prep_prompt_body, tm)
    else:
        in_specs += [pl.BlockSpec((1, HEAD_DIM), const2), pl.BlockSpec((1, HEAD_DIM), const2),
                     pl.BlockSpec((1, 128), const2)]
        params = (gq.reshape(1, HEAD_DIM), gk.reshape(1, HEAD_DIM),
                  jnp.zeros((1, 128), F32).at[0, :N_HEADS].set(bf))
        row_spec = lambda w: pl.BlockSpec((tm, w), lambda b, i: (rows(b, i), 0))
        out_specs = [row_spec(W_HEAD), row_spec(128), row_spec(W_HEAD)]
        out_shape = [jax.ShapeDtypeStruct((m, W_HEAD), F32), jax.ShapeDtypeStruct((m, 128), F32),
                     jax.ShapeDtypeStruct((m, W_HEAD), F32)]
        scratch = []
        body = functools.partial(_fox_prep_sample_body, tm)
    return pl.pallas_call(
        body,
        grid=(nb, nl),
        in_specs=in_specs,
        out_specs=out_specs,
        out_shape=out_shape,
        scratch_shapes=scratch,
        compiler_params=_params("arbitrary", "arbitrary"),
        name="fox_prep_prompt" if prompt else "fox_prep_sample",
    )(proj, proj, proj, proj, *params)


def _fox_attn_body(tq, tk, hps, qt_ref, ka_ref, vt_ref, o_ref, s_ref, m_ref, acc_ref):
    qi = pl.program_id(2)
    n_diag = tq // tk
    heads = range(hps)
    m_ref[...] = jnp.full_like(m_ref, NEG_INF)
    acc_ref[...] = jnp.zeros_like(acc_ref)

    def scores(slot, j):
        k0 = pl.multiple_of(j * tk, tk)
        for h in heads:
            s_ref[slot, h] = _dot(ka_ref[h, pl.ds(k0, tk), :], qt_ref[h])

    def absorb(h, s, v_t, lanes):
        m_old = m_ref[h, :, lanes]
        m_new = jnp.maximum(m_old, jnp.max(s, axis=0, keepdims=True))
        alpha = jnp.exp2(m_old - m_new)
        p = jnp.exp2(s - m_new)
        acc_ref[h, :, lanes] = alpha * acc_ref[h, :, lanes] + _dot(v_t, p.astype(BF16))
        m_ref[h, :, lanes] = m_new

    def consume(slot, j):
        k0 = pl.multiple_of(j * tk, tk)
        for h in heads:
            absorb(h, s_ref[slot, h], vt_ref[h, :, pl.ds(k0, tk)], slice(None))

    scores(0, 0)

    def pair(jj):
        scores(1, 2 * jj + 1)
        consume(0, 2 * jj)
        scores(0, 2 * jj + 2)
        consume(1, 2 * jj + 1)

    def four_pairs(t, carry):
        for u in range(4):
            pair(4 * t + u)
        return carry

    n_pairs = qi * (n_diag // 2)
    lax.fori_loop(0, n_pairs // 4, four_pairs, 0)
    done4 = (n_pairs // 4) * 4

    @pl.when(n_pairs % 4 >= 2)
    def _():
        pair(done4)
        pair(done4 + 1)

    @pl.when(n_pairs % 2 == 1)
    def _():
        pair(n_pairs - 1)

    def key_start(d):
        return pl.multiple_of((qi * n_diag + d) * tk, tk)

    def causal(n):
        return lax.broadcasted_iota(jnp.int32, (tk, n), 0) <= lax.broadcasted_iota(jnp.int32, (tk, n), 1)

    late = {d: [_dot(ka_ref[h, pl.ds(key_start(d), tk), :], qt_ref[h, :, d * tk:tq]) for h in heads]
            for d in range(1, n_diag)}
    for d in range(n_diag):
        for h in heads:
            s = s_ref[0, h] if d == 0 else late[d][h]
            absorb(h, jnp.where(causal(tq - d * tk), s, NEG_INF), vt_ref[h, :, pl.ds(key_start(d), tk)],
                   slice(d * tk, tq))

    outs = []
    for h in heads:
        acc = acc_ref[h]
        outs.append(acc[0:HEAD_DIM, :] / acc[HEAD_DIM:HEAD_DIM + 1, :])
    o_ref[...] = jnp.concatenate(outs, axis=0).T.astype(BF16)


def _fox_attn(qt, ka, vt, tq=512, hps=2):
    nb, nh, l, _ = ka.shape
    tq = min(tq, l)
    tk = tq // 2
    nq = l // tq
    return pl.pallas_call(
        functools.partial(_fox_attn_body, tq, tk, hps),
        grid=(nb, nh // hps, nq),
        in_specs=[pl.BlockSpec((None, hps, 128, tq), lambda b, hp, i: (b, hp, 0, i)),
                  pl.BlockSpec((None, hps, l, 128), lambda b, hp, i: (b, hp, 0, 0)),
                  pl.BlockSpec((None, hps, V_AUG_ROWS, l), lambda b, hp, i: (b, hp, 0, 0))],
        out_specs=pl.BlockSpec((tq, hps * HEAD_DIM), lambda b, hp, i: (b * nq + i, hp)),
        out_shape=jax.ShapeDtypeStruct((nb * l, W_HEAD), BF16),
        scratch_shapes=[pltpu.VMEM((2, hps, tk, tq), F32), pltpu.VMEM((hps, 1, tq), F32),
                        pltpu.VMEM((hps, V_AUG_ROWS, tq), F32)],
        compiler_params=_params("parallel", "parallel", "arbitrary"),
        name="fox_attn",
    )(qt, ka, vt)


def _ret_tables(pos0, l_pad, l_real, chunk):
    c_real = chunk if l_real % chunk == 0 else l_real
    half = HEAD_DIM // 2
    inv = ROPE_BASE ** (-np.arange(half, dtype=np.float64) / half)
    inv_l = np.tile(inv, 2 * N_HEADS)
    sgn = np.tile(np.concatenate([-np.ones(half), np.ones(half)]), N_HEADS)
    n_a = l_pad // chunk
    ang_a = (pos0 + chunk * np.arange(n_a))[:, None] * inv_l[None, :]
    ang_b = np.arange(chunk)[:, None] * inv_l[None, :]
    log_g = np.log1p(-np.exp2(-5.0 - np.arange(N_HEADS, dtype=np.float64)))
    idx = np.arange(chunk, dtype=np.float64)
    diff = idx[:, None] - idx[None, :]
    decay = np.where(diff >= 0, np.exp(np.maximum(diff, 0.0)[None] * log_g[:, None, None]), 0.0)
    lg_l = np.repeat(log_g, HEAD_DIM)
    xi = np.exp((idx[:, None] + 1.0) * lg_l[None, :])
    zeta = np.exp((c_real - 1.0 - idx[:, None]) * lg_l[None, :])
    head = np.arange(W_HEAD) // HEAD_DIM
    bd = (head[:, None] == head[None, :]).astype(np.float64)
    gmat = bd * np.exp(c_real * lg_l)[:, None]
    f = lambda a: jnp.asarray(a, dtype=F32)
    return dict(ca=f(np.cos(ang_a)), sa=f(sgn * np.sin(ang_a)), cb=f(np.cos(ang_b)), sb=f(sgn * np.sin(ang_b)),
                decay=f(decay), xi=f(xi), zeta=f(zeta), gmat=f(gmat), bd=f(bd),
                bdavg=jnp.asarray(bd / HEAD_DIM, dtype=BF16))


def _ret_body(tm, ch, n_seq, q_ref, k_ref, v_ref, g_ref, ca_ref, sa_ref, cb_ref, sb_ref, dec_ref, xi_ref,
              zeta_ref, gmat_ref, bd_ref, bdavg_ref, gn_ref, s0_ref, y_ref, sout_ref, st_ref):
    i = pl.program_id(1)
    n_chunk = tm // ch

    @pl.when(i == 0)
    def _():
        for s in range(n_seq):
            st_ref[s] = jnp.concatenate([s0_ref[s]] * N_HEADS, axis=-1) * bd_ref[...]

    lane = lax.broadcasted_iota(jnp.int32, (ch, W_HEAD), 1)
    lane128 = lax.broadcasted_iota(jnp.int32, (ch, 128), 1)
    first_half = (lane128 % HEAD_DIM) < (HEAD_DIM // 2)

    def swap_halves(x):
        parts = []
        for blk in range(W_HEAD // 128):
            xb = x[:, blk * 128:(blk + 1) * 128]
            parts.append(jnp.where(first_half, pltpu.roll(xb, 128 - HEAD_DIM // 2, 1),
                                   pltpu.roll(xb, HEAD_DIM // 2, 1)))
        return jnp.concatenate(parts, axis=-1)

    def gmean(t):
        hi, lo = _split2(t)
        return _dot(hi, bdavg_ref[...]) + _dot(lo, bdavg_ref[...])

    def chunk(s, c):
        r0 = s * tm + c * ch
        a = i * n_chunk + c
        c_a = ca_ref[pl.ds(a, 1), :]
        s_a = sa_ref[pl.ds(a, 1), :]
        cos = c_a * cb_ref[...] - s_a * sb_ref[...]
        sin = s_a * cb_ref[...] + c_a * sb_ref[...]
        q = q_ref[pl.ds(r0, ch), :]
        k = k_ref[pl.ds(r0, ch), :]
        q = q * cos + swap_halves(q) * sin
        k = (k * cos + swap_halves(k) * sin) * FOX_SCALE
        qb = q.astype(BF16)
        kb = k.astype(BF16)
        vb = v_ref[pl.ds(r0, ch), :].astype(BF16)
        st = st_ref[s]
        o = _dot(qb, st.astype(BF16)) * xi_ref[...]
        for h in range(N_HEADS):
            mh = (lane // HEAD_DIM) == h
            qm = jnp.where(mh, q, 0.0).astype(BF16)
            att = _dot_nt(qm, kb) * dec_ref[h]
            o = o + jnp.where(mh, _dot(att.astype(BF16), vb), 0.0)
        kz = (k * zeta_ref[...]).astype(BF16)
        st_ref[s] = st * gmat_ref[...] + _dot_tn(kz, vb) * bd_ref[...]
        mu = gmean(o)
        d = o - mu
        var = gmean(d * d)
        on = d * lax.rsqrt(var + EPS) * gn_ref[...]
        gate = g_ref[pl.ds(r0, ch), :]
        y_ref[pl.ds(r0, ch), :] = (on * (gate * _sigmoid(gate))).astype(BF16)

    for c in range(n_chunk):
        for s in range(n_seq):
            chunk(s, c)

    @pl.when(i == pl.num_programs(1) - 1)
    def _():
        for s in range(n_seq):
            st = st_ref[s]
            acc = st[:, 0:HEAD_DIM]
            for h in range(1, N_HEADS):
                acc = acc + st[:, h * HEAD_DIM:(h + 1) * HEAD_DIM]
            sout_ref[s] = acc


def _retention(proj, gn, s0, nb, l, l_real, pos0, tm=512, n_seq=1):
    m = proj.shape[0]
    tm = min(tm, l)
    ch = min(RET_CHUNK, tm)
    nl = l // tm
    assert n_seq == 1 or (nl == 1 and nb % n_seq == 0)
    t = _ret_tables(pos0, l, l_real, ch)
    rows = lambda b, i: b * nl + i
    const2 = lambda b, i: (0, 0)
    col = lambda cidx: pl.BlockSpec((n_seq * tm, W_HEAD), lambda b, i: (rows(b, i), cidx))
    full = lambda a: pl.BlockSpec(a.shape, (lambda b, i: (0,) * a.ndim))
    in_specs = [col(COL_RQ), col(COL_RK), col(COL_RV), col(COL_RG),
                full(t["ca"]), full(t["sa"]), full(t["cb"]), full(t["sb"]), full(t["decay"]),
                full(t["xi"]), full(t["zeta"]), full(t["gmat"]), full(t["bd"]), full(t["bdavg"]),
                pl.BlockSpec((1, W_HEAD), const2),
                pl.BlockSpec((n_seq, W_HEAD, HEAD_DIM), lambda b, i: (b, 0, 0))]
    return pl.pallas_call(
        functools.partial(_ret_body, tm, ch, n_seq),
        grid=(nb // n_seq, nl),
        in_specs=in_specs,
        out_specs=[pl.BlockSpec((n_seq * tm, W_HEAD), lambda b, i: (rows(b, i), 0)),
                   pl.BlockSpec((n_seq, W_HEAD, HEAD_DIM), lambda b, i: (b, 0, 0))],
        out_shape=[jax.ShapeDtypeStruct((m, W_HEAD), BF16),
                   jax.ShapeDtypeStruct((nb, W_HEAD, HEAD_DIM), F32)],
        scratch_shapes=[pltpu.VMEM((n_seq, W_HEAD, W_HEAD), F32)],
        compiler_params=_params("arbitrary", "arbitrary"),
        name="retention",
    )(proj, proj, proj, proj, t["ca"], t["sa"], t["cb"], t["sb"], t["decay"], t["xi"], t["zeta"],
      t["gmat"], t["bd"], t["bdavg"], gn.reshape(1, W_HEAD), s0)


def _lru_body(tm, l_real, lx_ref, lg_ref, cw_ref, cb_ref, wr_ref, wi_ref, br_ref, bi_ref, lam_ref,
              conv0_ref, h0_ref, y_ref, hlast_ref, convnew_ref, xpad_ref, a_ref, b_ref, h_ref, hcar_ref):
    i = pl.program_id(1)
    w = lx_ref.shape[1]
    t_last, r_last = (l_real - 1) // tm, (l_real - 1) % tm

    @pl.when(i == 0)
    def _():
        xpad_ref[0:8, :] = conv0_ref[...]
        hcar_ref[...] = h0_ref[...]

    x = lx_ref[...]
    xpad_ref[8:8 + tm, :] = x
    xc = cb_ref[...] + cw_ref[CONV_W - 1:CONV_W, :] * x
    for j in range(1, CONV_W):
        xc = xc + cw_ref[CONV_W - 1 - j:CONV_W - j, :] * xpad_ref[pl.ds(8 - j, tm), :]

    xb = xc.astype(BF16)
    r = _sigmoid(_dot(xb, wr_ref[...]) + br_ref[...])
    ig = _sigmoid(_dot(xb, wi_ref[...]) + bi_ref[...])
    z = -lam_ref[...]
    softplus = jnp.maximum(z, 0.0) + jnp.log1p(jnp.exp(-jnp.abs(z)))
    log_a = (-LRU_C) * r * softplus
    a = jnp.exp(log_a)
    th = jnp.tanh(log_a)
    em = -2.0 * th / (1.0 - th)
    u = jnp.where(em > 0.0, em * lax.rsqrt(em), 0.0) * (ig * xc)

    a = a.reshape(tm // 8, 8, w)
    u = u.reshape(tm // 8, 8, w)
    rowmod = lax.broadcasted_iota(jnp.int32, (tm // 8, 8, w), 1)
    for sh in (1, 2, 4):
        a_sh = pltpu.roll(a, sh, 1)
        u_sh = pltpu.roll(u, sh, 1)
        valid = rowmod >= sh
        u = jnp.where(valid, a * u_sh + u, u)
        a = jnp.where(valid, a * a_sh, a)
    a_ref[...] = a.reshape(tm, w)
    b_ref[...] = u.reshape(tm, w)

    def group(gi, h):
        r0 = pl.multiple_of(gi * 8, 8)
        hg = a_ref[pl.ds(r0, 8), :] * h + b_ref[pl.ds(r0, 8), :]
        h_ref[pl.ds(r0, 8), :] = hg
        return hg[7:8, :]

    hcar_ref[...] = lax.fori_loop(0, tm // 8, group, hcar_ref[...])

    g = lg_ref[...]
    gelu = 0.5 * g * (1.0 + jnp.tanh(0.7978845608028654 * (g + 0.044715 * (g * g * g))))
    y_ref[...] = (h_ref[...] * gelu).astype(BF16)

    @pl.when(i == t_last)
    def _():
        hlast_ref[...] = h_ref[r_last:r_last + 1, :]
        convnew_ref[...] = xpad_ref[pl.ds(r_last + 1, 8), :]

    xpad_ref[0:8, :] = xpad_ref[tm:tm + 8, :]


def _lru(proj, cw, cb, wr, wi, br, bi, lam, conv0, h0, nb, l, l_real, tm=512):
    m = proj.shape[0]
    w = cw.shape[1]
    tm = min(tm, l)
    nl = l // tm
    rows = lambda b, i: b * nl + i
    const2 = lambda b, i: (0, 0)
    vec = pl.BlockSpec((1, w), const2)
    per_b = lambda r: pl.BlockSpec((None, r, w), lambda b, i: (b, 0, 0))
    return pl.pallas_call(
        functools.partial(_lru_body, tm, l_real),
        grid=(nb, nl),
        in_specs=[pl.BlockSpec((tm, w), lambda b, i: (rows(b, i), COL_LX)),
                  pl.BlockSpec((tm, w), lambda b, i: (rows(b, i), COL_LG)),
                  pl.BlockSpec((CONV_W, w), const2), vec,
                  pl.BlockSpec((w, w), const2), pl.BlockSpec((w, w), const2), vec, vec, vec,
                  per_b(8), per_b(1)],
        out_specs=[pl.BlockSpec((tm, w), lambda b, i: (rows(b, i), 0)), per_b(1), per_b(8)],
        out_shape=[jax.ShapeDtypeStruct((m, w), BF16), jax.ShapeDtypeStruct((nb, 1, w), F32),
                   jax.ShapeDtypeStruct((nb, 8, w), F32)],
        scratch_shapes=[pltpu.VMEM((tm + 8, w), F32), pltpu.VMEM((tm, w), F32), pltpu.VMEM((tm, w), F32),
                        pltpu.VMEM((tm, w), F32), pltpu.VMEM((1, w), F32)],
        compiler_params=_params("arbitrary", "arbitrary"),
        name="lru",
    )(proj, proj, cw, cb.reshape(1, w), wr, wi, br.reshape(1, w), bi.reshape(1, w), lam.reshape(1, w),
      conv0, h0)


def _decode_body(n_pg, layer, pt_ref, qn_ref, knew_ref, vnew_ref, lfnew_ref, kt_hbm, vt_hbm, lf_hbm,
                 o_ref, kbuf, vbuf, lfbuf, sem, m_ref, l_ref, acc_ref, car_ref, qm_ref):
    b_i = pl.program_id(0)
    p_i = pl.program_id(1)
    n_steps = pl.num_programs(1)
    t = b_i * n_steps + p_i
    slot = lax.rem(t, 2)
    page = kbuf.shape[3]

    def page_copies(pages, sl):
        cps = []
        for g in range(n_pg):
            pg = pages(g)
            cps.append(pltpu.make_async_copy(kt_hbm.at[layer, pg], kbuf.at[sl, g], sem.at[0, sl]))
            cps.append(pltpu.make_async_copy(vt_hbm.at[layer, pg], vbuf.at[sl, g], sem.at[1, sl]))
            cps.append(pltpu.make_async_copy(lf_hbm.at[layer, pg], lfbuf.at[sl, g], sem.at[2, sl]))
        return cps

    @pl.when(t == 0)
    def _():
        for cp in page_copies(lambda g: pt_ref[0, g], 0):
            cp.start()

    @pl.when(t + 1 < pl.num_programs(0) * n_steps)
    def _():
        wrap = p_i + 1 == n_steps
        b_n = jnp.where(wrap, b_i + 1, b_i)
        p_n = jnp.where(wrap, 0, p_i + 1)
        for cp in page_copies(lambda g: pt_ref[b_n, p_n * n_pg + g], 1 - slot):
            cp.start()

    for cp in page_copies(lambda g: 0, slot):
        cp.wait()
    k_refs = [kbuf.at[slot, g] for g in range(n_pg)]
    v_refs = [vbuf.at[slot, g] for g in range(n_pg)]
    lf_refs = [lfbuf.at[slot, g] for g in range(n_pg)]

    row = lax.broadcasted_iota(jnp.int32, (8, W_HEAD), 0)
    lane = lax.broadcasted_iota(jnp.int32, (8, W_HEAD), 1)
    own = (lane // HEAD_DIM) == row

    @pl.when(p_i == 0)
    def _():
        q = jnp.where(own, jnp.broadcast_to(qn_ref[0:1, :] * FOX_SCALE, (8, W_HEAD)), 0.0)
        qm_ref[...] = _pad16(q).astype(BF16)
        m_ref[...] = jnp.full_like(m_ref, NEG_INF)
        l_ref[...] = jnp.zeros_like(l_ref)
        acc_ref[...] = jnp.zeros_like(acc_ref)
        car_ref[...] = jnp.zeros_like(car_ref)

    qm = qm_ref[...]

    def update(s, pv_fn):
        m_old = m_ref[...]
        m_new = jnp.maximum(m_old, jnp.max(s, axis=-1, keepdims=True))
        alpha = jnp.exp(m_old - m_new)
        p = jnp.exp(s - m_new)
        l_ref[...] = alpha * l_ref[...] + jnp.sum(p, axis=-1, keepdims=True)
        acc_ref[...] = alpha * acc_ref[...] + pv_fn(p)
        m_ref[...] = m_new

    r_i = lax.broadcasted_iota(jnp.int32, (page, page), 0)
    c_i = lax.broadcasted_iota(jnp.int32, (page, page), 1)
    upper = jnp.where(r_i <= c_i, 1.0, 0.0).astype(BF16)
    hi, mid, lo = _split3(jnp.concatenate([lf_refs[g][...] for g in range(n_pg)], axis=0))
    c_all = _dot(hi, upper) + _dot(mid, upper) + _dot(lo, upper)
    nr = n_pg * 8
    rr = lax.broadcasted_iota(jnp.int32, (nr, nr), 0)
    cc = lax.broadcasted_iota(jnp.int32, (nr, nr), 1)
    before = jnp.where((rr % 8 == cc % 8) & (cc // 8 < rr // 8), 1.0, 0.0).astype(BF16)
    chi, cmid, clo = _split3(c_all)
    pref = _dot(before, chi) + _dot(before, cmid) + _dot(before, clo)
    c_all = c_all + pref[:, page - 1:page] + jnp.concatenate([car_ref[...]] * n_pg, axis=0)
    car_ref[...] = c_all[nr - 8:nr, page - 1:page]
    s_parts = [_dot(qm, k_refs[g][...].astype(BF16))[0:8, :] - c_all[g * 8:(g + 1) * 8, :]
               for g in range(n_pg)]

    def pv_pages(p):
        pb = _pad16(p).astype(BF16)
        pv = _dot_nt(pb[:, 0:page], v_refs[0][...].astype(BF16))
        for g in range(1, n_pg):
            pv = pv + _dot_nt(pb[:, g * page:(g + 1) * page], v_refs[g][...].astype(BF16))
        return pv[0:8, :]

    update(jnp.concatenate(s_parts, axis=-1), pv_pages)

    @pl.when(p_i == pl.num_programs(1) - 1)
    def _():
        lane128 = lax.broadcasted_iota(jnp.int32, (8, 128), 1)
        row128 = lax.broadcasted_iota(jnp.int32, (8, 128), 0)
        lf_col = jnp.sum(jnp.where(lane128 == row128, jnp.broadcast_to(lfnew_ref[0:1, :], (8, 128)), 0.0),
                         axis=-1, keepdims=True)
        k_new = knew_ref[0:1, :].astype(BF16).astype(F32)
        v_new = vnew_ref[0:1, :].astype(BF16).astype(F32)
        s_new = jnp.sum(qm[0:8, :].astype(F32) * k_new, axis=-1, keepdims=True) - (car_ref[...] + lf_col)
        update(s_new, lambda p: p.astype(BF16).astype(F32) * v_new)
        o_ref[...] = jnp.sum(jnp.where(own, acc_ref[...] / l_ref[...], 0.0), axis=0, keepdims=True)


def _fox_decode(layer, page_table, qn, kn, proj, logf, cache_kt, cache_vt, cache_lft, nb):
    n_pages = page_table.shape[1]
    n_pg = min(PAGES_PER_STEP, n_pages)
    sr = SAMPLE_ROWS
    page = cache_kt.shape[3]
    tok = lambda cidx: (lambda b, p, pt: (b, cidx))
    hbm = pl.BlockSpec(memory_space=pl.ANY)
    in_specs = [pl.BlockSpec((sr, W_HEAD), tok(0)), pl.BlockSpec((sr, W_HEAD), tok(0)),
                pl.BlockSpec((sr, W_HEAD), tok(COL_FV)), pl.BlockSpec((sr, 128), tok(0)),
                hbm, hbm, hbm]
    grid_spec = pltpu.PrefetchScalarGridSpec(
        num_scalar_prefetch=1,
        grid=(nb, n_pages // n_pg),
        in_specs=in_specs,
        out_specs=pl.BlockSpec((None, 1, W_HEAD), lambda b, p, pt: (b, 0, 0)),
        scratch_shapes=[pltpu.VMEM((2, n_pg, W_HEAD, page), F32), pltpu.VMEM((2, n_pg, W_HEAD, page), F32),
                        pltpu.VMEM((2, n_pg, 8, page), F32), pltpu.SemaphoreType.DMA((3, 2)),
                        pltpu.VMEM((8, 1), F32), pltpu.VMEM((8, 1), F32), pltpu.VMEM((8, W_HEAD), F32),
                        pltpu.VMEM((8, 1), F32), pltpu.VMEM((16, W_HEAD), BF16)])
    return pl.pallas_call(
        functools.partial(_decode_body, n_pg, layer),
        grid_spec=grid_spec,
        out_shape=jax.ShapeDtypeStruct((nb, 1, W_HEAD), F32),
        compiler_params=_params("arbitrary", "arbitrary"),
        name="fox_decode",
    )(page_table, qn, kn, proj, logf, cache_kt, cache_vt, cache_lft)


def _block_diag(w):
    n, d, e = w.shape
    eye = jnp.eye(n, dtype=w.dtype)
    return (eye[:, None, :, None] * w[:, :, None, :]).reshape(n * d, n * e)


def _mixer_common(proj, nb, l, l_real, pos0, s0, conv0, h0, lw):
    n_seq = 4 if (l <= RET_CHUNK and nb % 4 == 0) else 1
    y_ret, s_new = _retention(proj, lw["ret_gn"], s0, nb, l, l_real, pos0, n_seq=n_seq)
    y_lru, h_last, conv_new = _lru(proj, lw["conv_w"], lw["conv_b"], lw["wr"], lw["wi"], lw["br"], lw["bi"],
                                   lw["lam"], conv0, h0, nb, l, l_real)
    return y_ret, y_lru, s_new, h_last, conv_new


def kernel(x_prompt, x_sample, cache_k, cache_v, cache_logf, state_ret, state_lru, state_conv, page_table, norm_ffn1, ffn1_gate, ffn1_up, ffn1_down, norm_mix, w_in, ret_gn, conv_w, conv_b, lru_wr, lru_br, lru_wi, lru_bi, lru_lambda, fox_qn, fox_kn, fox_bf, w_out, norm_ffn2, ffn2_gate, ffn2_up, ffn2_down):
    bp, lp, d = x_prompt.shape
    bs, ls, _ = x_sample.shape
    assert ls == 1
    depth = w_in.shape[0]
    n_pool, page = cache_k.shape[1], cache_k.shape[2]
    past_len = page_table.shape[1] * page
    w_lru = conv_w.shape[2]
    sr = SAMPLE_ROWS
    in_width = w_in.shape[2]

    g1, u1, d1 = ffn1_gate.astype(BF16), ffn1_up.astype(BF16), ffn1_down.astype(BF16)
    g2, u2, d2 = ffn2_gate.astype(BF16), ffn2_up.astype(BF16), ffn2_down.astype(BF16)
    w_out_b = w_out.astype(BF16)
    w_in_b = jnp.pad(w_in, ((0, 0), (0, 0), (0, PROJ_W - in_width))).astype(BF16)
    cache_k4 = jnp.transpose(cache_k, (0, 1, 3, 4, 2)).reshape(depth, n_pool, W_HEAD, page)
    cache_v4 = jnp.transpose(cache_v, (0, 1, 3, 4, 2)).reshape(depth, n_pool, W_HEAD, page)
    cache_lft = jnp.pad(jnp.swapaxes(cache_logf, 2, 3), ((0, 0), (0, 0), (0, 8 - N_HEADS), (0, 0)))

    xp = x_prompt.reshape(bp * lp, d)
    xs = x_sample.reshape(bs, d)
    zero_s = jnp.zeros((bp, W_HEAD, HEAD_DIM), F32)
    zero_c = jnp.zeros((bp, 8, w_lru), F32)
    zero_h = jnp.zeros((bp, 1, w_lru), F32)

    outs_p = [[] for _ in range(6)]
    outs_s = [[] for _ in range(6)]
    for l in range(depth):
        lw = dict(ret_gn=ret_gn[l], conv_w=conv_w[l], conv_b=conv_b[l],
                  wr=_block_diag(lru_wr[l]).astype(BF16), wi=_block_diag(lru_wi[l]).astype(BF16),
                  br=lru_br[l], bi=lru_bi[l], lam=lru_lambda[l])

        xp, proj = _ffn(xp, norm_ffn1[l], g1, u1, d1, l, proj=(norm_mix[l], w_in_b))
        kn_t, v_t, logf_t, qa, ka, va = _fox_prep(proj, fox_qn[l], fox_kn[l], fox_bf[l], bp, lp, True)
        y_fox = _fox_attn(qa, ka, va)
        y_ret, y_lru, s_new, h_last, conv_new = _mixer_common(proj, bp, lp, lp, 0, zero_s, zero_c, zero_h, lw)
        xp = _ffn(xp, norm_ffn2[l], g2, u2, d2, l, mix=(y_ret, y_lru, y_fox, w_out_b))
        heads_last = lambda a: jnp.transpose(a.reshape(bp, N_HEADS, HEAD_DIM, lp), (0, 3, 1, 2))
        outs_p[0].append(heads_last(kn_t))
        outs_p[1].append(heads_last(v_t))
        outs_p[2].append(jnp.transpose(logf_t[:, :N_HEADS, :], (0, 2, 1)))
        outs_p[3].append(s_new.reshape(bp, N_HEADS, HEAD_DIM, HEAD_DIM))
        outs_p[4].append(h_last.reshape(bp, w_lru))
        outs_p[5].append(conv_new[:, 8 - (CONV_W - 1):, :])

        xs, proj_tok = _ffn(xs, norm_ffn1[l], g1, u1, d1, l, proj=(norm_mix[l], w_in_b))
        proj_s = jnp.pad(proj_tok[:, None, :], ((0, 0), (0, sr - 1), (0, 0))).reshape(bs * sr, PROJ_W)
        kn_s, logf_s, qn_s = _fox_prep(proj_s, fox_qn[l], fox_kn[l], fox_bf[l], 1, bs * sr, False)
        y_fox_s = _fox_decode(l, page_table, qn_s, kn_s, proj_s, logf_s, cache_k4, cache_v4, cache_lft, bs)
        s0 = state_ret[l].reshape(bs, W_HEAD, HEAD_DIM)
        conv0 = jnp.pad(state_conv[l], ((0, 0), (8 - (CONV_W - 1), 0), (0, 0)))
        h0 = state_lru[l].reshape(bs, 1, w_lru)
        y_ret_s, y_lru_s, s_new_s, h_last_s, conv_new_s = _mixer_common(
            proj_s, bs, sr, 1, past_len, s0, conv0, h0, lw)
        first = lambda a: a.reshape(bs, sr, a.shape[-1])[:, 0, :]
        xs = _ffn(xs, norm_ffn2[l], g2, u2, d2, l,
                  mix=(first(y_ret_s), first(y_lru_s), y_fox_s.reshape(bs, W_HEAD).astype(BF16), w_out_b))
        outs_s[0].append(first(kn_s).reshape(bs, 1, N_HEADS, HEAD_DIM))
        outs_s[1].append(first(proj_s)[:, COL_FV * W_HEAD:(COL_FV + 1) * W_HEAD].reshape(bs, 1, N_HEADS, HEAD_DIM))
        outs_s[2].append(first(logf_s)[:, :N_HEADS].reshape(bs, 1, N_HEADS))
        outs_s[3].append(s_new_s.reshape(bs, N_HEADS, HEAD_DIM, HEAD_DIM))
        outs_s[4].append(h_last_s.reshape(bs, w_lru))
        outs_s[5].append(conv_new_s[:, 8 - (CONV_W - 1):, :])

    stk = lambda lst: jnp.stack(lst, axis=0)
    return (xp.reshape(bp, lp, d), xs.reshape(bs, 1, d),
            *[stk(o) for o in outs_p], *[stk(o) for o in outs_s])
```

```python
import functools

import numpy as np
import jax
import jax.numpy as jnp
from jax import lax
from jax.experimental import pallas as pl
from jax.experimental.pallas import tpu as pltpu

F32 = jnp.float32
BF16 = jnp.bfloat16

HEAD_DIM = 64
N_HEADS = 4
W_HEAD = N_HEADS * HEAD_DIM
CONV_W = 4
LRU_C = 8.0
RET_CHUNK = 256
ROPE_BASE = 10000.0
EPS = 1e-6
NEG_INF = -1e30
FOX_SCALE = HEAD_DIM ** -0.5
LOG2E = 1.4426950408889634
V_AUG_ROWS = 80
SAMPLE_ROWS = 16
PAGES_PER_STEP = 64
VMEM_LIMIT = 48 * 1024 * 1024
VMEM_LIMIT_FUSED = 58 * 1024 * 1024

COL_RQ, COL_RK, COL_RV, COL_RG = 0, 1, 2, 3
COL_LX, COL_LG = 2, 3
COL_FQ, COL_FK, COL_FV = 8, 9, 10
COL_FF = 22
PROJ_W = 23 * 128


def _dot(a, b):
    return jnp.dot(a, b, preferred_element_type=F32)


def _dot_nt(a, b):
    return lax.dot_general(a, b, (((1,), (1,)), ((), ())), preferred_element_type=F32)


def _dot_tn(a, b):
    return lax.dot_general(a, b, (((0,), (0,)), ((), ())), preferred_element_type=F32)


def _sigmoid(x):
    return 0.5 * jnp.tanh(0.5 * x) + 0.5


def _split2(x):
    hi = x.astype(BF16)
    lo = (x - hi.astype(F32)).astype(BF16)
    return hi, lo


def _split3(x):
    hi = x.astype(BF16)
    r = x - hi.astype(F32)
    mid = r.astype(BF16)
    lo = (r - mid.astype(F32)).astype(BF16)
    return hi, mid, lo


def _params(*sem):
    return pltpu.CompilerParams(dimension_semantics=sem, vmem_limit_bytes=VMEM_LIMIT)


def _ffn_body(has_mix, has_proj, tf, *refs):
    refs = list(refs)
    x_ref = refs.pop(0)
    if has_mix:
        yr_ref, yl_ref, yf_ref, wo_ref = refs[:4]
        del refs[:4]
    g_ref, wg_ref, wu_ref, wd_ref = refs[:4]
    del refs[:4]
    if has_proj:
        gm_ref, win_ref = refs[:2]
        del refs[:2]
    o_ref = refs.pop(0)
    if has_proj:
        proj_ref = refs.pop(0)
    xres_ref, xn_ref, hid_ref = refs

    x = x_ref[...]
    if has_mix:
        w1 = yr_ref.shape[1]
        w2 = w1 + yl_ref.shape[1]
        x = (x + _dot(yr_ref[...], wo_ref[0:w1, :]) + _dot(yl_ref[...], wo_ref[w1:w2, :])
             + _dot(yf_ref[...], wo_ref[w2:, :]))
    xres_ref[...] = x
    ms = jnp.mean(x * x, axis=-1, keepdims=True)
    xn_ref[...] = (x * lax.rsqrt(ms + EPS) * g_ref[...]).astype(BF16)

    xn = xn_ref[...]
    for j in range(wg_ref.shape[1] // tf):
        gate = _dot(xn, wg_ref[:, j * tf:(j + 1) * tf])
        up = _dot(xn, wu_ref[:, j * tf:(j + 1) * tf])
        hid_ref[:, j * tf:(j + 1) * tf] = (gate * _sigmoid(gate) * up).astype(BF16)
    out = xres_ref[...] + 0.5 * _dot(hid_ref[...], wd_ref[...])
    o_ref[...] = out
    if has_proj:
        ms_o = jnp.mean(out * out, axis=-1, keepdims=True)
        proj_ref[...] = _dot((out * lax.rsqrt(ms_o + EPS) * gm_ref[...]).astype(BF16), win_ref[...])


def _ffn(x, g, wg, wu, wd, layer, mix=None, proj=None, tm=512, tf=256):
    m, d = x.shape
    f = wg.shape[2]
    tm = min(tm, m)
    row = lambda i: (i, 0)
    resident = lambda a: pl.BlockSpec((None,) + a.shape[1:], lambda i: (layer, 0, 0),
                                      pipeline_mode=pl.Buffered(1))
    vec = pl.BlockSpec((1, d), lambda i: (0, 0), pipeline_mode=pl.Buffered(1))
    in_specs = [pl.BlockSpec((tm, d), row)]
    args = [x]
    if mix is not None:
        yr, yl, yf, wo = mix
        in_specs += [pl.BlockSpec((tm, yr.shape[1]), row), pl.BlockSpec((tm, yl.shape[1]), row),
                     pl.BlockSpec((tm, yf.shape[1]), row), resident(wo)]
        args += [yr, yl, yf, wo]
    in_specs += [vec, resident(wg), resident(wu), resident(wd)]
    args += [g.reshape(1, d), wg, wu, wd]
    out_specs = [pl.BlockSpec((tm, d), row)]
    out_shape = [jax.ShapeDtypeStruct((m, d), F32)]
    vmem = VMEM_LIMIT
    if proj is not None:
        g_mix, w_in = proj
        in_specs += [vec, resident(w_in)]
        args += [g_mix.reshape(1, d), w_in]
        out_specs.append(pl.BlockSpec((tm, w_in.shape[2]), row))
        out_shape.append(jax.ShapeDtypeStruct((m, w_in.shape[2]), F32))
        vmem = VMEM_LIMIT_FUSED
    res = pl.pallas_call(
        functools.partial(_ffn_body, mix is not None, proj is not None, tf),
        grid=(m // tm,),
        in_specs=in_specs,
        out_specs=out_specs,
        out_shape=out_shape,
        scratch_shapes=[pltpu.VMEM((tm, d), F32), pltpu.VMEM((tm, d), BF16), pltpu.VMEM((tm, f), BF16)],
        compiler_params=pltpu.CompilerParams(dimension_semantics=("parallel",), vmem_limit_bytes=vmem),
        name="ffn_mix" if mix is not None else ("ffn_proj" if proj is not None else "ffn"),
    )(*args)
    return res if proj is not None else res[0]


def _head_norm(x, g):
    outs = []
    for h in range(N_HEADS):
        xh = x[:, h * HEAD_DIM:(h + 1) * HEAD_DIM]
        ms = jnp.mean(xh * xh, axis=-1, keepdims=True)
        outs.append(xh * lax.rsqrt(ms + EPS) * g)
    return outs


def _pad16(x):
    return jnp.concatenate([x, jnp.zeros_like(x)], axis=0)


def _fox_prep_sample_body(tm, fq_ref, fk_ref, fv_ref, ff_ref, gq_ref, gk_ref, bf_ref,
                          kn_ref, logf_ref, qn_ref):
    del fv_ref
    lane = lax.broadcasted_iota(jnp.int32, (tm, 128), 1)
    z = ff_ref[...] + bf_ref[...]
    logf = jnp.minimum(z, 0.0) - jnp.log1p(jnp.exp(-jnp.abs(z)))
    logf_ref[...] = jnp.where(lane < N_HEADS, logf, 0.0)
    kn_ref[...] = jnp.concatenate(_head_norm(fk_ref[...], gk_ref[...]), axis=-1)
    qn_ref[...] = jnp.concatenate(_head_norm(fq_ref[...], gq_ref[...]), axis=-1)


def _fox_prep_prompt_body(tm, fq_ref, fk_ref, fv_ref, ff_ref, gq_ref, gk_ref, bf_ref,
                          knt_ref, vt_ref, lft_ref, qa_ref, ka_ref, va_ref, carry_ref):
    @pl.when(pl.program_id(1) == 0)
    def _():
        carry_ref[...] = jnp.zeros_like(carry_ref)

    q_t = fq_ref[...].T
    k_t = fk_ref[...].T
    v_t = fv_ref[...].T
    z = ff_ref[...].T[0:8, :] + bf_ref[...]
    lf = jnp.minimum(z, 0.0) - jnp.log1p(jnp.exp(-jnp.abs(z)))
    lft_ref[...] = lf
    vt_ref[...] = v_t

    r_i = lax.broadcasted_iota(jnp.int32, (tm, tm), 0)
    c_i = lax.broadcasted_iota(jnp.int32, (tm, tm), 1)
    upper = jnp.where(r_i <= c_i, 1.0, 0.0).astype(BF16)
    hi, mid, lo = _split3(_pad16(lf))
    cs = (_dot(hi, upper) + _dot(mid, upper) + _dot(lo, upper))[0:8, :] + carry_ref[...]
    carry_ref[...] = cs[:, tm - 1:tm]
    cs = cs * LOG2E

    sub = lax.broadcasted_iota(jnp.int32, (HEAD_DIM, tm), 0)
    v_tail = jnp.where(lax.broadcasted_iota(jnp.int32, (V_AUG_ROWS - HEAD_DIM, tm), 0) == 0, 1.0, 0.0)
    kn_parts = []
    for h in range(N_HEADS):
        rows = slice(h * HEAD_DIM, (h + 1) * HEAD_DIM)
        qh, kh = q_t[rows], k_t[rows]
        qn = qh * lax.rsqrt(jnp.mean(qh * qh, axis=0, keepdims=True) + EPS) * gq_ref[...]
        kn = kh * lax.rsqrt(jnp.mean(kh * kh, axis=0, keepdims=True) + EPS) * gk_ref[...]
        kn_parts.append(kn)
        ch = cs[h:h + 1, :]
        chi = ch.astype(BF16).astype(F32)
        r1 = ch - chi
        cmid = r1.astype(BF16).astype(F32)
        clo = r1 - cmid
        q_tail = jnp.where(sub == 0, chi, jnp.where(sub == 1, cmid, jnp.where(
            sub == 2, clo, jnp.where(sub < 6, 1.0, 0.0))))
        k_tail = jnp.where(sub < 3, 1.0, jnp.where(sub == 3, -chi, jnp.where(
            sub == 4, -cmid, jnp.where(sub == 5, -clo, 0.0))))
        qa_ref[h] = jnp.concatenate([qn * (FOX_SCALE * LOG2E), q_tail], axis=0).astype(BF16)
        ka_ref[h] = jnp.concatenate([kn, k_tail], axis=0).T.astype(BF16)
        va_ref[h] = jnp.concatenate([v_t[rows], v_tail], axis=0).astype(BF16)
    knt_ref[...] = jnp.concatenate(kn_parts, axis=0)


def _fox_prep(proj, gq, gk, bf, nb, l, prompt, tm=512):
    m = proj.shape[0]
    tm = min(tm, l)
    nl = l // tm
    rows = lambda b, i: b * nl + i
    const2 = lambda b, i: (0, 0)
    in_specs = [pl.BlockSpec((tm, W_HEAD), lambda b, i: (rows(b, i), COL_FQ)),
                pl.BlockSpec((tm, W_HEAD), lambda b, i: (rows(b, i), COL_FK)),
                pl.BlockSpec((tm, W_HEAD), lambda b, i: (rows(b, i), COL_FV)),
                pl.BlockSpec((tm, 128), lambda b, i: (rows(b, i), COL_FF))]
    if prompt:
        in_specs += [pl.BlockSpec((HEAD_DIM, 1), const2), pl.BlockSpec((HEAD_DIM, 1), const2),
                     pl.BlockSpec((8, 1), const2)]
        params = (gq.reshape(HEAD_DIM, 1), gk.reshape(HEAD_DIM, 1),
                  jnp.zeros((8, 1), F32).at[:N_HEADS, 0].set(bf))
        pos_minor = lambda r: pl.BlockSpec((None, r, tm), lambda b, i: (b, 0, i))
        aug = pl.BlockSpec((None, N_HEADS, tm, 128), lambda b, i: (b, 0, i, 0))
        aug_t = pl.BlockSpec((None, N_HEADS, 128, tm), lambda b, i: (b, 0, 0, i))
        aug_v = pl.BlockSpec((None, N_HEADS, V_AUG_ROWS, tm), lambda b, i: (b, 0, 0, i))
        out_specs = [pos_minor(W_HEAD), pos_minor(W_HEAD), pos_minor(8), aug_t, aug, aug_v]
        out_shape = [jax.ShapeDtypeStruct((nb, W_HEAD, l), F32), jax.ShapeDtypeStruct((nb, W_HEAD, l), F32),
                     jax.ShapeDtypeStruct((nb, 8, l), F32),
                     jax.ShapeDtypeStruct((nb, N_HEADS, 128, l), BF16),
                     jax.ShapeDtypeStruct((nb, N_HEADS, l, 128), BF16),
                     jax.ShapeDtypeStruct((nb, N_HEADS, V_AUG_ROWS, l), BF16)]
        scratch = [pltpu.VMEM((8, 1), F32)]
        body = functools.partial(_fox_prep_prompt_body, tm)
    else:
        in_specs += [pl.BlockSpec((1, HEAD_DIM), const2), pl.BlockSpec((1, HEAD_DIM), const2),
                     pl.BlockSpec((1, 128), const2)]
        params = (gq.reshape(1, HEAD_DIM), gk.reshape(1, HEAD_DIM),
                  jnp.zeros((1, 128), F32).at[0, :N_HEADS].set(bf))
        row_spec = lambda w: pl.BlockSpec((tm, w), lambda b, i: (rows(b, i), 0))
        out_specs = [row_spec(W_HEAD), row_spec(128), row_spec(W_HEAD)]
        out_shape = [jax.ShapeDtypeStruct((m, W_HEAD), F32), jax.ShapeDtypeStruct((m, 128), F32),
                     jax.ShapeDtypeStruct((m, W_HEAD), F32)]
        scratch = []
        body = functools.partial(_fox_prep_sample_body, tm)
    return pl.pallas_call(
        body,
        grid=(nb, nl),
        in_specs=in_specs,
        out_specs=out_specs,
        out_shape=out_shape,
        scratch_shapes=scratch,
        compiler_params=_params("arbitrary", "arbitrary"),
        name="fox_prep_prompt" if prompt else "fox_prep_sample",
    )(proj, proj, proj, proj, *params)


def _fox_attn_body(tq, tk, hps, qt_ref, ka_ref, vt_ref, o_ref, s_ref, m_ref, acc_ref):
    qi = pl.program_id(2)
    n_diag = tq // tk
    heads = range(hps)
    m_ref[...] = jnp.full_like(m_ref, NEG_INF)
    acc_ref[...] = jnp.zeros_like(acc_ref)

    def scores(slot, j):
        k0 = pl.multiple_of(j * tk, tk)
        for h in heads:
            s_ref[slot, h] = _dot(ka_ref[h, pl.ds(k0, tk), :], qt_ref[h])

    def absorb(h, s, v_t, lanes):
        m_old = m_ref[h, :, lanes]
        m_new = jnp.maximum(m_old, jnp.max(s, axis=0, keepdims=True))
        alpha = jnp.exp2(m_old - m_new)
        p = jnp.exp2(s - m_new)
        acc_ref[h, :, lanes] = alpha * acc_ref[h, :, lanes] + _dot(v_t, p.astype(BF16))
        m_ref[h, :, lanes] = m_new

    def consume(slot, j):
        k0 = pl.multiple_of(j * tk, tk)
        for h in heads:
            absorb(h, s_ref[slot, h], vt_ref[h, :, pl.ds(k0, tk)], slice(None))

    scores(0, 0)

    def pair(jj):
        scores(1, 2 * jj + 1)
        consume(0, 2 * jj)
        scores(0, 2 * jj + 2)
        consume(1, 2 * jj + 1)

    def four_pairs(t, carry):
        for u in range(4):
            pair(4 * t + u)
        return carry

    n_pairs = qi * (n_diag // 2)
    lax.fori_loop(0, n_pairs // 4, four_pairs, 0)
    done4 = (n_pairs // 4) * 4

    @pl.when(n_pairs % 4 >= 2)
    def _():
        pair(done4)
        pair(done4 + 1)

    @pl.when(n_pairs % 2 == 1)
    def _():
        pair(n_pairs - 1)

    def key_start(d):
        return pl.multiple_of((qi * n_diag + d) * tk, tk)

    def causal(n):
        return lax.broadcasted_iota(jnp.int32, (tk, n), 0) <= lax.broadcasted_iota(jnp.int32, (tk, n), 1)

    late = {d: [_dot(ka_ref[h, pl.ds(key_start(d), tk), :], qt_ref[h, :, d * tk:tq]) for h in heads]
            for d in range(1, n_diag)}
    for d in range(n_diag):
        for h in heads:
            s = s_ref[0, h] if d == 0 else late[d][h]
            absorb(h, jnp.where(causal(tq - d * tk), s, NEG_INF), vt_ref[h, :, pl.ds(key_start(d), tk)],
                   slice(d * tk, tq))

    outs = []
    for h in heads:
        acc = acc_ref[h]
        outs.append(acc[0:HEAD_DIM, :] / acc[HEAD_DIM:HEAD_DIM + 1, :])
    o_ref[...] = jnp.concatenate(outs, axis=0).T.astype(BF16)


def _fox_attn(qt, ka, vt, tq=512, hps=2):
    nb, nh, l, _ = ka.shape
    tq = min(tq, l)
    tk = tq // 2
    nq = l // tq
    return pl.pallas_call(
        functools.partial(_fox_attn_body, tq, tk, hps),
        grid=(nb, nh // hps, nq),
        in_specs=[pl.BlockSpec((None, hps, 128, tq), lambda b, hp, i: (b, hp, 0, i)),
                  pl.BlockSpec((None, hps, l, 128), lambda b, hp, i: (b, hp, 0, 0)),
                  pl.BlockSpec((None, hps, V_AUG_ROWS, l), lambda b, hp, i: (b, hp, 0, 0))],
        out_specs=pl.BlockSpec((tq, hps * HEAD_DIM), lambda b, hp, i: (b * nq + i, hp)),
        out_shape=jax.ShapeDtypeStruct((nb * l, W_HEAD), BF16),
        scratch_shapes=[pltpu.VMEM((2, hps, tk, tq), F32), pltpu.VMEM((hps, 1, tq), F32),
                        pltpu.VMEM((hps, V_AUG_ROWS, tq), F32)],
        compiler_params=_params("parallel", "parallel", "arbitrary"),
        name="fox_attn",
    )(qt, ka, vt)


def _ret_tables(pos0, l_pad, l_real, chunk):
    c_real = chunk if l_real % chunk == 0 else l_real
    half = HEAD_DIM // 2
    inv = ROPE_BASE ** (-np.arange(half, dtype=np.float64) / half)
    inv_l = np.tile(inv, 2 * N_HEADS)
    sgn = np.tile(np.concatenate([-np.ones(half), np.ones(half)]), N_HEADS)
    n_a = l_pad // chunk
    ang_a = (pos0 + chunk * np.arange(n_a))[:, None] * inv_l[None, :]
    ang_b = np.arange(chunk)[:, None] * inv_l[None, :]
    log_g = np.log1p(-np.exp2(-5.0 - np.arange(N_HEADS, dtype=np.float64)))
    idx = np.arange(chunk, dtype=np.float64)
    diff = idx[:, None] - idx[None, :]
    decay = np.where(diff >= 0, np.exp(np.maximum(diff, 0.0)[None] * log_g[:, None, None]), 0.0)
    lg_l = np.repeat(log_g, HEAD_DIM)
    xi = np.exp((idx[:, None] + 1.0) * lg_l[None, :])
    zeta = np.exp((c_real - 1.0 - idx[:, None]) * lg_l[None, :])
    head = np.arange(W_HEAD) // HEAD_DIM
    bd = (head[:, None] == head[None, :]).astype(np.float64)
    gmat = bd * np.exp(c_real * lg_l)[:, None]
    f = lambda a: jnp.asarray(a, dtype=F32)
    return dict(ca=f(np.cos(ang_a)), sa=f(sgn * np.sin(ang_a)), cb=f(np.cos(ang_b)), sb=f(sgn * np.sin(ang_b)),
                decay=f(decay), xi=f(xi), zeta=f(zeta), gmat=f(gmat), bd=f(bd),
                bdavg=jnp.asarray(bd / HEAD_DIM, dtype=BF16))


def _ret_body(tm, ch, n_seq, q_ref, k_ref, v_ref, g_ref, ca_ref, sa_ref, cb_ref, sb_ref, dec_ref, xi_ref,
              zeta_ref, gmat_ref, bd_ref, bdavg_ref, gn_ref, s0_ref, y_ref, sout_ref, st_ref):
    i = pl.program_id(1)
    n_chunk = tm // ch

    @pl.when(i == 0)
    def _():
        for s in range(n_seq):
            st_ref[s] = jnp.concatenate([s0_ref[s]] * N_HEADS, axis=-1) * bd_ref[...]

    lane = lax.broadcasted_iota(jnp.int32, (ch, W_HEAD), 1)
    lane128 = lax.broadcasted_iota(jnp.int32, (ch, 128), 1)
    first_half = (lane128 % HEAD_DIM) < (HEAD_DIM // 2)

    def swap_halves(x):
        parts = []
        for blk in range(W_HEAD // 128):
            xb = x[:, blk * 128:(blk + 1) * 128]
            parts.append(jnp.where(first_half, pltpu.roll(xb, 128 - HEAD_DIM // 2, 1),
                                   pltpu.roll(xb, HEAD_DIM // 2, 1)))
        return jnp.concatenate(parts, axis=-1)

    def gmean(t):
        hi, lo = _split2(t)
        return _dot(hi, bdavg_ref[...]) + _dot(lo, bdavg_ref[...])

    def chunk(s, c):
        r0 = s * tm + c * ch
        a = i * n_chunk + c
        c_a = ca_ref[pl.ds(a, 1), :]
        s_a = sa_ref[pl.ds(a, 1), :]
        cos = c_a * cb_ref[...] - s_a * sb_ref[...]
        sin = s_a * cb_ref[...] + c_a * sb_ref[...]
        q = q_ref[pl.ds(r0, ch), :]
        k = k_ref[pl.ds(r0, ch), :]
        q = q * cos + swap_halves(q) * sin
        k = (k * cos + swap_halves(k) * sin) * FOX_SCALE
        qb = q.astype(BF16)
        kb = k.astype(BF16)
        vb = v_ref[pl.ds(r0, ch), :].astype(BF16)
        st = st_ref[s]
        o = _dot(qb, st.astype(BF16)) * xi_ref[...]
        for h in range(N_HEADS):
            mh = (lane // HEAD_DIM) == h
            qm = jnp.where(mh, q, 0.0).astype(BF16)
            att = _dot_nt(qm, kb) * dec_ref[h]
            o = o + jnp.where(mh, _dot(att.astype(BF16), vb), 0.0)
        kz = (k * zeta_ref[...]).astype(BF16)
        st_ref[s] = st * gmat_ref[...] + _dot_tn(kz, vb) * bd_ref[...]
        mu = gmean(o)
        d = o - mu
        var = gmean(d * d)
        on = d * lax.rsqrt(var + EPS) * gn_ref[...]
        gate = g_ref[pl.ds(r0, ch), :]
        y_ref[pl.ds(r0, ch), :] = (on * (gate * _sigmoid(gate))).astype(BF16)

    for c in range(n_chunk):
        for s in range(n_seq):
            chunk(s, c)

    @pl.when(i == pl.num_programs(1) - 1)
    def _():
        for s in range(n_seq):
            st = st_ref[s]
            acc = st[:, 0:HEAD_DIM]
            for h in range(1, N_HEADS):
                acc = acc + st[:, h * HEAD_DIM:(h + 1) * HEAD_DIM]
            sout_ref[s] = acc


def _retention(proj, gn, s0, nb, l, l_real, pos0, tm=512, n_seq=1):
    m = proj.shape[0]
    tm = min(tm, l)
    ch = min(RET_CHUNK, tm)
    nl = l // tm
    assert n_seq == 1 or (nl == 1 and nb % n_seq == 0)
    t = _ret_tables(pos0, l, l_real, ch)
    rows = lambda b, i: b * nl + i
    const2 = lambda b, i: (0, 0)
    col = lambda cidx: pl.BlockSpec((n_seq * tm, W_HEAD), lambda b, i: (rows(b, i), cidx))
    full = lambda a: pl.BlockSpec(a.shape, (lambda b, i: (0,) * a.ndim))
    in_specs = [col(COL_RQ), col(COL_RK), col(COL_RV), col(COL_RG),
                full(t["ca"]), full(t["sa"]), full(t["cb"]), full(t["sb"]), full(t["decay"]),
                full(t["xi"]), full(t["zeta"]), full(t["gmat"]), full(t["bd"]), full(t["bdavg"]),
                pl.BlockSpec((1, W_HEAD), const2),
                pl.BlockSpec((n_seq, W_HEAD, HEAD_DIM), lambda b, i: (b, 0, 0))]
    return pl.pallas_call(
        functools.partial(_ret_body, tm, ch, n_seq),
        grid=(nb // n_seq, nl),
        in_specs=in_specs,
        out_specs=[pl.BlockSpec((n_seq * tm, W_HEAD), lambda b, i: (rows(b, i), 0)),
                   pl.BlockSpec((n_seq, W_HEAD, HEAD_DIM), lambda b, i: (b, 0, 0))],
        out_shape=[jax.ShapeDtypeStruct((m, W_HEAD), BF16),
                   jax.ShapeDtypeStruct((nb, W_HEAD, HEAD_DIM), F32)],
        scratch_shapes=[pltpu.VMEM((n_seq, W_HEAD, W_HEAD), F32)],
        compiler_params=_params("arbitrary", "arbitrary"),
        name="retention",
    )(proj, proj, proj, proj, t["ca"], t["sa"], t["cb"], t["sb"], t["decay"], t["xi"], t["zeta"],
      t["gmat"], t["bd"], t["bdavg"], gn.reshape(1, W_HEAD), s0)


def _lru_body(tm, l_real, lx_ref, lg_ref, cw_ref, cb_ref, wr_ref, wi_ref, br_ref, bi_ref, lam_ref,
              conv0_ref, h0_ref, y_ref, hlast_ref, convnew_ref, xpad_ref, a_ref, b_ref, h_ref, hcar_ref):
    i = pl.program_id(1)
    w = lx_ref.shape[1]
    t_last, r_last = (l_real - 1) // tm, (l_real - 1) % tm

    @pl.when(i == 0)
    def _():
        xpad_ref[0:8, :] = conv0_ref[...]
        hcar_ref[...] = h0_ref[...]

    x = lx_ref[...]
    xpad_ref[8:8 + tm, :] = x
    xc = cb_ref[...] + cw_ref[CONV_W - 1:CONV_W, :] * x
    for j in range(1, CONV_W):
        xc = xc + cw_ref[CONV_W - 1 - j:CONV_W - j, :] * xpad_ref[pl.ds(8 - j, tm), :]

    xb = xc.astype(BF16)
    r = _sigmoid(_dot(xb, wr_ref[...]) + br_ref[...])
    ig = _sigmoid(_dot(xb, wi_ref[...]) + bi_ref[...])
    z = -lam_ref[...]
    softplus = jnp.maximum(z, 0.0) + jnp.log1p(jnp.exp(-jnp.abs(z)))
    log_a = (-LRU_C) * r * softplus
    a = jnp.exp(log_a)
    th = jnp.tanh(log_a)
    em = -2.0 * th / (1.0 - th)
    u = jnp.where(em > 0.0, em * lax.rsqrt(em), 0.0) * (ig * xc)

    a = a.reshape(tm // 8, 8, w)
    u = u.reshape(tm // 8, 8, w)
    rowmod = lax.broadcasted_iota(jnp.int32, (tm // 8, 8, w), 1)
    for sh in (1, 2, 4):
        a_sh = pltpu.roll(a, sh, 1)
        u_sh = pltpu.roll(u, sh, 1)
        valid = rowmod >= sh
        u = jnp.where(valid, a * u_sh + u, u)
        a = jnp.where(valid, a * a_sh, a)
    a_ref[...] = a.reshape(tm, w)
    b_ref[...] = u.reshape(tm, w)

    def group(gi, h):
        r0 = pl.multiple_of(gi * 8, 8)
        hg = a_ref[pl.ds(r0, 8), :] * h + b_ref[pl.ds(r0, 8), :]
        h_ref[pl.ds(r0, 8), :] = hg
        return hg[7:8, :]

    hcar_ref[...] = lax.fori_loop(0, tm // 8, group, hcar_ref[...])

    g = lg_ref[...]
    gelu = 0.5 * g * (1.0 + jnp.tanh(0.7978845608028654 * (g + 0.044715 * (g * g * g))))
    y_ref[...] = (h_ref[...] * gelu).astype(BF16)

    @pl.when(i == t_last)
    def _():
        hlast_ref[...] = h_ref[r_last:r_last + 1, :]
        convnew_ref[...] = xpad_ref[pl.ds(r_last + 1, 8), :]

    xpad_ref[0:8, :] = xpad_ref[tm:tm + 8, :]


def _lru(proj, cw, cb, wr, wi, br, bi, lam, conv0, h0, nb, l, l_real, tm=512):
    m = proj.shape[0]
    w = cw.shape[1]
    tm = min(tm, l)
    nl = l // tm
    rows = lambda b, i: b * nl + i
    const2 = lambda b, i: (0, 0)
    vec = pl.BlockSpec((1, w), const2)
    per_b = lambda r: pl.BlockSpec((None, r, w), lambda b, i: (b, 0, 0))
    return pl.pallas_call(
        functools.partial(_lru_body, tm, l_real),
        grid=(nb, nl),
        in_specs=[pl.BlockSpec((tm, w), lambda b, i: (rows(b, i), COL_LX)),
                  pl.BlockSpec((tm, w), lambda b, i: (rows(b, i), COL_LG)),
                  pl.BlockSpec((CONV_W, w), const2), vec,
                  pl.BlockSpec((w, w), const2), pl.BlockSpec((w, w), const2), vec, vec, vec,
                  per_b(8), per_b(1)],
        out_specs=[pl.BlockSpec((tm, w), lambda b, i: (rows(b, i), 0)), per_b(1), per_b(8)],
        out_shape=[jax.ShapeDtypeStruct((m, w), BF16), jax.ShapeDtypeStruct((nb, 1, w), F32),
                   jax.ShapeDtypeStruct((nb, 8, w), F32)],
        scratch_shapes=[pltpu.VMEM((tm + 8, w), F32), pltpu.VMEM((tm, w), F32), pltpu.VMEM((tm, w), F32),
                        pltpu.VMEM((tm, w), F32), pltpu.VMEM((1, w), F32)],
        compiler_params=_params("arbitrary", "arbitrary"),
        name="lru",
    )(proj, proj, cw, cb.reshape(1, w), wr, wi, br.reshape(1, w), bi.reshape(1, w), lam.reshape(1, w),
      conv0, h0)


def _decode_body(n_pg, layer, pt_ref, qn_ref, knew_ref, vnew_ref, lfnew_ref, kt_hbm, vt_hbm, lf_hbm,
                 o_ref, kbuf, vbuf, lfbuf, sem, m_ref, l_ref, acc_ref, car_ref, qm_ref):
    b_i = pl.program_id(0)
    p_i = pl.program_id(1)
    n_steps = pl.num_programs(1)
    t = b_i * n_steps + p_i
    slot = lax.rem(t, 2)
    page = kbuf.shape[3]

    def page_copies(pages, sl):
        cps = []
        for g in range(n_pg):
            pg = pages(g)
            cps.append(pltpu.make_async_copy(kt_hbm.at[layer, pg], kbuf.at[sl, g], sem.at[0, sl]))
            cps.append(pltpu.make_async_copy(vt_hbm.at[layer, pg], vbuf.at[sl, g], sem.at[1, sl]))
            cps.append(pltpu.make_async_copy(lf_hbm.at[layer, pg], lfbuf.at[sl, g], sem.at[2, sl]))
        return cps

    @pl.when(t == 0)
    def _():
        for cp in page_copies(lambda g: pt_ref[0, g], 0):
            cp.start()

    @pl.when(t + 1 < pl.num_programs(0) * n_steps)
    def _():
        wrap = p_i + 1 == n_steps
        b_n = jnp.where(wrap, b_i + 1, b_i)
        p_n = jnp.where(wrap, 0, p_i + 1)
        for cp in page_copies(lambda g: pt_ref[b_n, p_n * n_pg + g], 1 - slot):
            cp.start()

    for cp in page_copies(lambda g: 0, slot):
        cp.wait()
    k_refs = [kbuf.at[slot, g] for g in range(n_pg)]
    v_refs = [vbuf.at[slot, g] for g in range(n_pg)]
    lf_refs = [lfbuf.at[slot, g] for g in range(n_pg)]

    row = lax.broadcasted_iota(jnp.int32, (8, W_HEAD), 0)
    lane = lax.broadcasted_iota(jnp.int32, (8, W_HEAD), 1)
    own = (lane // HEAD_DIM) == row

    @pl.when(p_i == 0)
    def _():
        q = jnp.where(own, jnp.broadcast_to(qn_ref[0:1, :] * FOX_SCALE, (8, W_HEAD)), 0.0)
        qm_ref[...] = _pad16(q).astype(BF16)
        m_ref[...] = jnp.full_like(m_ref, NEG_INF)
        l_ref[...] = jnp.zeros_like(l_ref)
        acc_ref[...] = jnp.zeros_like(acc_ref)
        car_ref[...] = jnp.zeros_like(car_ref)

    qm = qm_ref[...]

    def update(s, pv_fn):
        m_old = m_ref[...]
        m_new = jnp.maximum(m_old, jnp.max(s, axis=-1, keepdims=True))
        alpha = jnp.exp(m_old - m_new)
        p = jnp.exp(s - m_new)
        l_ref[...] = alpha * l_ref[...] + jnp.sum(p, axis=-1, keepdims=True)
        acc_ref[...] = alpha * acc_ref[...] + pv_fn(p)
        m_ref[...] = m_new

    r_i = lax.broadcasted_iota(jnp.int32, (page, page), 0)
    c_i = lax.broadcasted_iota(jnp.int32, (page, page), 1)
    upper = jnp.where(r_i <= c_i, 1.0, 0.0).astype(BF16)
    hi, mid, lo = _split3(jnp.concatenate([lf_refs[g][...] for g in range(n_pg)], axis=0))
    c_all = _dot(hi, upper) + _dot(mid, upper) + _dot(lo, upper)
    nr = n_pg * 8
    rr = lax.broadcasted_iota(jnp.int32, (nr, nr), 0)
    cc = lax.broadcasted_iota(jnp.int32, (nr, nr), 1)
    before = jnp.where((rr % 8 == cc % 8) & (cc // 8 < rr // 8), 1.0, 0.0).astype(BF16)
    chi, cmid, clo = _split3(c_all)
    pref = _dot(before, chi) + _dot(before, cmid) + _dot(before, clo)
    c_all = c_all + pref[:, page - 1:page] + jnp.concatenate([car_ref[...]] * n_pg, axis=0)
    car_ref[...] = c_all[nr - 8:nr, page - 1:page]
    s_parts = [_dot(qm, k_refs[g][...].astype(BF16))[0:8, :] - c_all[g * 8:(g + 1) * 8, :]
               for g in range(n_pg)]

    def pv_pages(p):
        pb = _pad16(p).astype(BF16)
        pv = _dot_nt(pb[:, 0:page], v_refs[0][...].astype(BF16))
        for g in range(1, n_pg):
            pv = pv + _dot_nt(pb[:, g * page:(g + 1) * page], v_refs[g][...].astype(BF16))
        return pv[0:8, :]

    update(jnp.concatenate(s_parts, axis=-1), pv_pages)

    @pl.when(p_i == pl.num_programs(1) - 1)
    def _():
        lane128 = lax.broadcasted_iota(jnp.int32, (8, 128), 1)
        row128 = lax.broadcasted_iota(jnp.int32, (8, 128), 0)
        lf_col = jnp.sum(jnp.where(lane128 == row128, jnp.broadcast_to(lfnew_ref[0:1, :], (8, 128)), 0.0),
                         axis=-1, keepdims=True)
        k_new = knew_ref[0:1, :].astype(BF16).astype(F32)
        v_new = vnew_ref[0:1, :].astype(BF16).astype(F32)
        s_new = jnp.sum(qm[0:8, :].astype(F32) * k_new, axis=-1, keepdims=True) - (car_ref[...] + lf_col)
        update(s_new, lambda p: p.astype(BF16).astype(F32) * v_new)
        o_ref[...] = jnp.sum(jnp.where(own, acc_ref[...] / l_ref[...], 0.0), axis=0, keepdims=True)


def _fox_decode(layer, page_table, qn, kn, proj, logf, cache_kt, cache_vt, cache_lft, nb):
    n_pages = page_table.shape[1]
    n_pg = min(PAGES_PER_STEP, n_pages)
    sr = SAMPLE_ROWS
    page = cache_kt.shape[3]
    tok = lambda cidx: (lambda b, p, pt: (b, cidx))
    hbm = pl.BlockSpec(memory_space=pl.ANY)
    in_specs = [pl.BlockSpec((sr, W_HEAD), tok(0)), pl.BlockSpec((sr, W_HEAD), tok(0)),
                pl.BlockSpec((sr, W_HEAD), tok(COL_FV)), pl.BlockSpec((sr, 128), tok(0)),
                hbm, hbm, hbm]
    grid_spec = pltpu.PrefetchScalarGridSpec(
        num_scalar_prefetch=1,
        grid=(nb, n_pages // n_pg),
        in_specs=in_specs,
        out_specs=pl.BlockSpec((None, 1, W_HEAD), lambda b, p, pt: (b, 0, 0)),
        scratch_shapes=[pltpu.VMEM((2, n_pg, W_HEAD, page), F32), pltpu.VMEM((2, n_pg, W_HEAD, page), F32),
                        pltpu.VMEM((2, n_pg, 8, page), F32), pltpu.SemaphoreType.DMA((3, 2)),
                        pltpu.VMEM((8, 1), F32), pltpu.VMEM((8, 1), F32), pltpu.VMEM((8, W_HEAD), F32),
                        pltpu.VMEM((8, 1), F32), pltpu.VMEM((16, W_HEAD), BF16)])
    return pl.pallas_call(
        functools.partial(_decode_body, n_pg, layer),
        grid_spec=grid_spec,
        out_shape=jax.ShapeDtypeStruct((nb, 1, W_HEAD), F32),
        compiler_params=_params("arbitrary", "arbitrary"),
        name="fox_decode",
    )(page_table, qn, kn, proj, logf, cache_kt, cache_vt, cache_lft)


def _block_diag(w):
    n, d, e = w.shape
    eye = jnp.eye(n, dtype=w.dtype)
    return (eye[:, None, :, None] * w[:, :, None, :]).reshape(n * d, n * e)


def _mixer_common(proj, nb, l, l_real, pos0, s0, conv0, h0, lw):
    n_seq = 4 if (l <= RET_CHUNK and nb % 4 == 0) else 1
    y_ret, s_new = _retention(proj, lw["ret_gn"], s0, nb, l, l_real, pos0, n_seq=n_seq)
    y_lru, h_last, conv_new = _lru(proj, lw["conv_w"], lw["conv_b"], lw["wr"], lw["wi"], lw["br"], lw["bi"],
                                   lw["lam"], conv0, h0, nb, l, l_real)
    return y_ret, y_lru, s_new, h_last, conv_new


def kernel(x_prompt, x_sample, cache_k, cache_v, cache_logf, state_ret, state_lru, state_conv, page_table, norm_ffn1, ffn1_gate, ffn1_up, ffn1_down, norm_mix, w_in, ret_gn, conv_w, conv_b, lru_wr, lru_br, lru_wi, lru_bi, lru_lambda, fox_qn, fox_kn, fox_bf, w_out, norm_ffn2, ffn2_gate, ffn2_up, ffn2_down):
    bp, lp, d = x_prompt.shape
    bs, ls, _ = x_sample.shape
    assert ls == 1
    depth = w_in.shape[0]
    n_pool, page = cache_k.shape[1], cache_k.shape[2]
    past_len = page_table.shape[1] * page
    w_lru = conv_w.shape[2]
    sr = SAMPLE_ROWS
    in_width = w_in.shape[2]

    g1, u1, d1 = ffn1_gate.astype(BF16), ffn1_up.astype(BF16), ffn1_down.astype(BF16)
    g2, u2, d2 = ffn2_gate.astype(BF16), ffn2_up.astype(BF16), ffn2_down.astype(BF16)
    w_out_b = w_out.astype(BF16)
    w_in_b = jnp.pad(w_in, ((0, 0), (0, 0), (0, PROJ_W - in_width))).astype(BF16)
    cache_k4 = jnp.transpose(cache_k, (0, 1, 3, 4, 2)).reshape(depth, n_pool, W_HEAD, page)
    cache_v4 = jnp.transpose(cache_v, (0, 1, 3, 4, 2)).reshape(depth, n_pool, W_HEAD, page)
    cache_lft = jnp.pad(jnp.swapaxes(cache_logf, 2, 3), ((0, 0), (0, 0), (0, 8 - N_HEADS), (0, 0)))

    xp = x_prompt.reshape(bp * lp, d)
    xs = x_sample.reshape(bs, d)
    zero_s = jnp.zeros((bp, W_HEAD, HEAD_DIM), F32)
    zero_c = jnp.zeros((bp, 8, w_lru), F32)
    zero_h = jnp.zeros((bp, 1, w_lru), F32)

    outs_p = [[] for _ in range(6)]
    outs_s = [[] for _ in range(6)]
    for l in range(depth):
        lw = dict(ret_gn=ret_gn[l], conv_w=conv_w[l], conv_b=conv_b[l],
                  wr=_block_diag(lru_wr[l]).astype(BF16), wi=_block_diag(lru_wi[l]).astype(BF16),
                  br=lru_br[l], bi=lru_bi[l], lam=lru_lambda[l])

        xp, proj = _ffn(xp, norm_ffn1[l], g1, u1, d1, l, proj=(norm_mix[l], w_in_b))
        kn_t, v_t, logf_t, qa, ka, va = _fox_prep(proj, fox_qn[l], fox_kn[l], fox_bf[l], bp, lp, True)
        y_fox = _fox_attn(qa, ka, va)
        y_ret, y_lru, s_new, h_last, conv_new = _mixer_common(proj, bp, lp, lp, 0, zero_s, zero_c, zero_h, lw)
        xp = _ffn(xp, norm_ffn2[l], g2, u2, d2, l, mix=(y_ret, y_lru, y_fox, w_out_b))
        heads_last = lambda a: jnp.transpose(a.reshape(bp, N_HEADS, HEAD_DIM, lp), (0, 3, 1, 2))
        outs_p[0].append(heads_last(kn_t))
        outs_p[1].append(heads_last(v_t))
        outs_p[2].append(jnp.transpose(logf_t[:, :N_HEADS, :], (0, 2, 1)))
        outs_p[3].append(s_new.reshape(bp, N_HEADS, HEAD_DIM, HEAD_DIM))
        outs_p[4].append(h_last.reshape(bp, w_lru))
        outs_p[5].append(conv_new[:, 8 - (CONV_W - 1):, :])

        xs, proj_tok = _ffn(xs, norm_ffn1[l], g1, u1, d1, l, proj=(norm_mix[l], w_in_b))
        proj_s = jnp.pad(proj_tok[:, None, :], ((0, 0), (0, sr - 1), (0, 0))).reshape(bs * sr, PROJ_W)
        kn_s, logf_s, qn_s = _fox_prep(proj_s, fox_qn[l], fox_kn[l], fox_bf[l], 1, bs * sr, False)
        y_fox_s = _fox_decode(l, page_table, qn_s, kn_s, proj_s, logf_s, cache_k4, cache_v4, cache_lft, bs)
        s0 = state_ret[l].reshape(bs, W_HEAD, HEAD_DIM)
        conv0 = jnp.pad(state_conv[l], ((0, 0), (8 - (CONV_W - 1), 0), (0, 0)))
        h0 = state_lru[l].reshape(bs, 1, w_lru)
        y_ret_s, y_lru_s, s_new_s, h_last_s, conv_new_s = _mixer_common(
            proj_s, bs, sr, 1, past_len, s0, conv0, h0, lw)
        first = lambda a: a.reshape(bs, sr, a.shape[-1])[:, 0, :]
        xs = _ffn(xs, norm_ffn2[l], g2, u2, d2, l,
                  mix=(first(y_ret_s), first(y_lru_s), y_fox_s.reshape(bs, W_HEAD).astype(BF16), w_out_b))
        outs_s[0].append(first(kn_s).reshape(bs, 1, N_HEADS, HEAD_DIM))
        outs_s[1].append(first(proj_s)[:, COL_FV * W_HEAD:(COL_FV + 1) * W_HEAD].reshape(bs, 1, N_HEADS, HEAD_DIM))
        outs_s[2].append(first(logf_s)[:, :N_HEADS].reshape(bs, 1, N_HEADS))
        outs_s[3].append(s_new_s.reshape(bs, N_HEADS, HEAD_DIM, HEAD_DIM))
        outs_s[4].append(h_last_s.reshape(bs, w_lru))
        outs_s[5].append(conv_new_s[:, 8 - (CONV_W - 1):, :])

    stk = lambda lst: jnp.stack(lst, axis=0)
    return (xp.reshape(bp, lp, d), xs.reshape(bs, 1, d),
            *[stk(o) for o in outs_p], *[stk(o) for o in outs_s])
```
